```python
import jax, jax.numpy as jnp
from jax import lax
import numpy as np

D_MODEL = 1024
BATCH = 8
SEQ = 2048
DEPTH = 2

EPS = 1e-6
NEG_INF = -1e30
CONV_W = 3
CONV_CH = D_MODEL // 4
CONV_GROUPS = 4
DN_HEADS = 4
DN_HEAD_DIM = 128
DN_WIDTH = DN_HEADS * DN_HEAD_DIM
DN_CHUNK = 64
SWA_HEAD_DIM = 64
SWA_HEADS = 4
SWA_PATTERNS = ((128, 1), (512, 4), (2048, 16))
SWA_N_PAT = 3
SWA_WIDTH = SWA_HEADS * SWA_HEAD_DIM
SWA_QKV_WIDTH = SWA_N_PAT * SWA_WIDTH
SWA_BLOCK = 64
ROPE_THETA = 500000.0
ROPE_DIM = SWA_HEAD_DIM // 4
D_FF = 2816
MIX_WIDTH = CONV_CH + DN_WIDTH + SWA_WIDTH
IN_SIZES = (CONV_CH, CONV_CH, CONV_CH,
            DN_WIDTH, DN_WIDTH, DN_WIDTH, DN_WIDTH,
            DN_HEADS, DN_HEADS, DN_HEADS, DN_HEADS,
            SWA_QKV_WIDTH, SWA_QKV_WIDTH, SWA_QKV_WIDTH)
IN_WIDTH = sum(IN_SIZES)

kernel_name = 'hybrid_parallel_heads_encoder'


def rmsnorm(x, g):
    xf = x.astype(jnp.float32)
    y = xf * lax.rsqrt(jnp.mean(xf * xf, axis=-1, keepdims=True) + EPS)
    return (y * g.astype(jnp.float32)).astype(x.dtype)


def group_rmsnorm(x, g, n_groups):
    shp = x.shape
    xf = x.astype(jnp.float32).reshape(shp[:-1] + (n_groups, shp[-1] // n_groups))
    y = xf * lax.rsqrt(jnp.mean(xf * xf, axis=-1, keepdims=True) + EPS)
    return (y.reshape(shp) * g.astype(jnp.float32)).astype(x.dtype)


def dwconv3(x, w):
    xp = jnp.pad(x, ((0, 0), (1, 1), (0, 0)))
    return xp[:, :-2] * w[0] + xp[:, 1:-1] * w[1] + xp[:, 2:] * w[2]


def l2norm(x):
    return x * lax.rsqrt(jnp.sum(x * x, axis=-1, keepdims=True) + EPS)


def split_cols(a, sizes):
    idx = [int(i) for i in np.cumsum(sizes)[:-1]]
    return jnp.split(a, idx, axis=-1)


def short_conv_mixer(xa, gate_b, gate_c, w_conv):
    return gate_b * dwconv3(gate_c * xa, w_conv)


def gated_delta_chunked(q, k, v, g, beta):
    b, h, t, dk = q.shape
    dv = v.shape[-1]
    c = DN_CHUNK
    n = t // c

    def chunks(a):
        return a.reshape(a.shape[:2] + (n, c) + a.shape[3:])

    q, k, v, g, beta = (chunks(a) for a in (q, k, v, g, beta))
    g = jnp.cumsum(g, axis=-1)
    incl = jnp.tril(jnp.ones((c, c), dtype=bool))
    strict = jnp.tril(jnp.ones((c, c), dtype=bool), -1)
    diff = g[..., :, None] - g[..., None, :]
    decay = jnp.where(incl, jnp.exp(jnp.where(incl, diff, 0.0)), 0.0)
    kb = k * beta[..., None]
    a_mat = jnp.where(strict, jnp.einsum('bhncd,bhnsd->bhncs', kb, k) * decay, 0.0)
    t_mat = a_mat + jnp.eye(c, dtype=a_mat.dtype)
    rhs = jnp.concatenate([v * beta[..., None], kb * jnp.exp(g)[..., None]], axis=-1)
    sol = lax.linalg.triangular_solve(t_mat, rhs, left_side=True, lower=True, unit_diagonal=True)
    u, w = sol[..., :dv], sol[..., dv:]
    qk = jnp.where(incl, jnp.einsum('bhncd,bhnsd->bhncs', q, k) * decay, 0.0)
    q_dec = q * jnp.exp(g)[..., None]
    k_dec = k * jnp.exp(g[..., -1:] - g)[..., None]
    g_last = jnp.exp(g[..., -1])
    xs = tuple(jnp.moveaxis(a, 2, 0) for a in (u, w, qk, q_dec, k_dec, g_last))

    def step(state, inp):
        u_i, w_i, qk_i, qd_i, kd_i, gl_i = inp
        v_new = u_i - jnp.einsum('bhck,bhkv->bhcv', w_i, state)
        o_i = jnp.einsum('bhck,bhkv->bhcv', qd_i, state) + jnp.einsum('bhcs,bhsv->bhcv', qk_i, v_new)
        state = state * gl_i[..., None, None] + jnp.einsum('bhck,bhcv->bhkv', kd_i, v_new)
        return state, o_i

    s0 = jnp.zeros((b, h, dk, dv), jnp.float32)
    _, o = lax.scan(step, s0, xs)
    return jnp.moveaxis(o, 0, 2).reshape(b, h, t, dv)


def gated_deltanet_mixer(q, k, v, gate, b_f, b_b, a_f, a_b, w_conv,
                         a_log_f, a_log_b, dt_bias_f, dt_bias_b, norm_g):
    bsz, t, _ = q.shape
    qkv = jax.nn.silu(dwconv3(jnp.concatenate([q, k, v], axis=-1), w_conv)).astype(jnp.float32)
    q, k, v = jnp.split(qkv, 3, axis=-1)

    def heads(a):
        return a.reshape(bsz, t, DN_HEADS, DN_HEAD_DIM).transpose(0, 2, 1, 3)

    q = l2norm(heads(q)) * (DN_HEAD_DIM ** -0.5)
    k = l2norm(heads(k))
    v = heads(v)

    def decay_beta(a, bl, a_log, dt_bias):
        g = -jnp.exp(a_log.astype(jnp.float32)) * jax.nn.softplus(a.astype(jnp.float32) + dt_bias.astype(jnp.float32))
        beta = jax.nn.sigmoid(bl.astype(jnp.float32))
        return g.transpose(0, 2, 1), beta.transpose(0, 2, 1)

    g_f, beta_f = decay_beta(a_f, b_f, a_log_f, dt_bias_f)
    g_b, beta_b = decay_beta(a_b, b_b, a_log_b, dt_bias_b)
    flip = lambda a: jnp.flip(a, axis=2)
    o_f = gated_delta_chunked(q, k, v, g_f, beta_f)
    o_b = flip(gated_delta_chunked(flip(q), flip(k), flip(v), flip(g_b), flip(beta_b)))
    o = (o_f + o_b).transpose(0, 2, 1, 3)
    o = o * lax.rsqrt(jnp.mean(o * o, axis=-1, keepdims=True) + EPS) * norm_g.astype(jnp.float32)
    o = o * jax.nn.silu(gate.astype(jnp.float32).reshape(bsz, t, DN_HEADS, DN_HEAD_DIM))
    return o.reshape(bsz, t, DN_WIDTH).astype(gate.dtype)


def partial_rope(x, positions):
    half = ROPE_DIM // 2
    inv_freq = ROPE_THETA ** (-jnp.arange(half, dtype=jnp.float32) / half)
    ang = positions.astype(jnp.float32)[:, :, None] * inv_freq
    cos = jnp.cos(ang)[:, :, None, None, :]
    sin = jnp.sin(ang)[:, :, None, None, :]
    xr = x[..., :ROPE_DIM].astype(jnp.float32)
    x1, x2 = xr[..., :half], xr[..., half:]
    rot = jnp.concatenate([x1 * cos - x2 * sin, x2 * cos + x1 * sin], axis=-1).astype(x.dtype)
    return jnp.concatenate([rot, x[..., ROPE_DIM:]], axis=-1)


def banded_attention(q, k, v, radius):
    lead = q.shape[:-2]
    L, dh = q.shape[-2], q.shape[-1]
    blk = SWA_BLOCK
    nb = -(-L // blk)
    pad = nb * blk - L
    nlead = len(lead)
    qb = jnp.pad(q, [(0, 0)] * nlead + [(0, pad), (0, 0)]).reshape(lead + (nb, blk, dh))

    def windows(a):
        ap = jnp.pad(a, [(0, 0)] * nlead + [(blk, pad + blk), (0, 0)]).reshape(lead + (nb + 2, blk, dh))
        return jnp.concatenate([ap[..., :-2, :, :], ap[..., 1:-1, :, :], ap[..., 2:, :, :]], axis=-2)

    kw, vw = windows(k), windows(v)
    q_pos = jnp.arange(nb)[:, None] * blk + jnp.arange(blk)[None, :]
    k_pos = jnp.arange(nb)[:, None] * blk - blk + jnp.arange(3 * blk)[None, :]
    kp = k_pos[:, None, :]
    valid = (jnp.abs(q_pos[:, :, None] - kp) <= radius) & (kp >= 0) & (kp < L)
    s = jnp.einsum('...nqd,...nkd->...nqk', qb, kw).astype(jnp.float32) * (dh ** -0.5)
    s = jnp.where(valid, s, NEG_INF)
    m = jnp.max(s, axis=-1, keepdims=True)
    p = jnp.exp(s - m)
    den = jnp.sum(p, axis=-1, keepdims=True)
    o = jnp.einsum('...nqk,...nkd->...nqd', p / den, vw.astype(jnp.float32))
    lse = (m + jnp.log(den))[..., 0]
    o = o.reshape(lead + (nb * blk, dh))[..., :L, :]
    lse = lse.reshape(lead + (nb * blk,))[..., :L]
    return o, lse


def dilated_window_attention(q, k, v, positions):
    b, t, _ = q.shape
    shp = (b, t, SWA_N_PAT, SWA_HEADS, SWA_HEAD_DIM)
    q = partial_rope(q.reshape(shp), positions)
    k = partial_rope(k.reshape(shp), positions)
    v = v.reshape(shp)
    outs, lses = [], []
    for p_idx, (window, dil) in enumerate(SWA_PATTERNS):
        L = t // dil
        radius = window // (2 * dil)

        def to_sub(a):
            return a.reshape(b, L, dil, SWA_HEADS, SWA_HEAD_DIM).transpose(0, 3, 2, 1, 4)

        o, lse = banded_attention(to_sub(q[:, :, p_idx]), to_sub(k[:, :, p_idx]), to_sub(v[:, :, p_idx]), radius)
        outs.append(o.transpose(0, 3, 2, 1, 4).reshape(b, t, SWA_HEADS, SWA_HEAD_DIM))
        lses.append(lse.transpose(0, 3, 2, 1).reshape(b, t, SWA_HEADS))
    alpha = jax.nn.softmax(jnp.stack(lses, axis=0), axis=0)
    o = jnp.einsum('pbth,pbthd->bthd', alpha, jnp.stack(outs, axis=0))
    return o.reshape(b, t, SWA_WIDTH).astype(q.dtype)


def conv_gated_mlp(h, w_up, w_conv, w_down):
    u = dwconv3(h @ w_up, w_conv)
    gate, val = jnp.split(u, 2, axis=-1)
    return (jax.nn.silu(gate) * val) @ w_down


def setup_inputs(seed: int = 0) -> dict:
    key = jax.random.key(seed)
    ks = jax.random.split(key, 24)
    f32 = jnp.float32

    def nrm(k, shape, scale):
        return jax.random.normal(k, shape, f32) * scale

    def gain(k, shape):
        return 1.0 + 0.02 * jax.random.normal(k, shape, f32)

    def dt_bias(k):
        dt = jnp.exp(jax.random.uniform(k, (DEPTH, DN_HEADS), f32, np.log(1e-3), np.log(1e-1)))
        return dt + jnp.log(-jnp.expm1(-dt))

    x = nrm(ks[0], (BATCH, SEQ, D_MODEL), 1.0)
    positions = (jax.random.randint(ks[1], (BATCH, 1), 0, 4096, dtype=jnp.int32)
                 + jnp.arange(SEQ, dtype=jnp.int32)[None, :])
    return {
        'x': x,
        'positions': positions,
        'norm_mix': gain(ks[2], (DEPTH, D_MODEL)),
        'w_in': nrm(ks[3], (DEPTH, D_MODEL, IN_WIDTH), D_MODEL ** -0.5),
        'conv_a': nrm(ks[4], (DEPTH, CONV_W, CONV_CH), CONV_W ** -0.5),
        'norm_a': gain(ks[5], (DEPTH, CONV_CH)),
        'conv_qkv': nrm(ks[6], (DEPTH, CONV_W, 3 * DN_WIDTH), CONV_W ** -0.5),
        'a_log_f': jnp.log(jax.random.uniform(ks[7], (DEPTH, DN_HEADS), f32, 1.0, 16.0)),
        'a_log_b': jnp.log(jax.random.uniform(ks[8], (DEPTH, DN_HEADS), f32, 1.0, 16.0)),
        'dt_bias_f': dt_bias(ks[9]),
        'dt_bias_b': dt_bias(ks[10]),
        'norm_dn': gain(ks[11], (DEPTH, DN_HEAD_DIM)),
        'norm_c': gain(ks[12], (DEPTH, SWA_WIDTH)),
        'w_o': nrm(ks[13], (DEPTH, MIX_WIDTH, D_MODEL), MIX_WIDTH ** -0.5),
        'norm_ffn': gain(ks[14], (DEPTH, D_MODEL)),
        'w_up': nrm(ks[15], (DEPTH, D_MODEL, 2 * D_FF), D_MODEL ** -0.5),
        'conv_ffn': nrm(ks[16], (DEPTH, CONV_W, 2 * D_FF), CONV_W ** -0.5),
        'w_down': nrm(ks[17], (DEPTH, D_FF, D_MODEL), D_FF ** -0.5),
        'norm_final': gain(ks[18], (D_MODEL,)),
    }


def reference(x, positions, norm_mix, w_in, conv_a, norm_a, conv_qkv, a_log_f, a_log_b,
              dt_bias_f, dt_bias_b, norm_dn, norm_c, w_o, norm_ffn, w_up, conv_ffn, w_down,
              norm_final):
    for l in range(DEPTH):
        h = rmsnorm(x, norm_mix[l])
        proj = h @ w_in[l]
        (xa, gb, gc, q_dn, k_dn, v_dn, gate_dn, b_f, b_b, a_f, a_b,
         q_c, k_c, v_c) = split_cols(proj, IN_SIZES)
        y_a = group_rmsnorm(short_conv_mixer(xa, gb, gc, conv_a[l]), norm_a[l], CONV_GROUPS)
        y_b = gated_deltanet_mixer(q_dn, k_dn, v_dn, gate_dn, b_f, b_b, a_f, a_b, conv_qkv[l],
                                   a_log_f[l], a_log_b[l], dt_bias_f[l], dt_bias_b[l], norm_dn[l])
        y_c = group_rmsnorm(dilated_window_attention(q_c, k_c, v_c, positions), norm_c[l], SWA_HEADS)
        x = x + jnp.concatenate([y_a, y_b, y_c], axis=-1) @ w_o[l]
        x = x + conv_gated_mlp(rmsnorm(x, norm_ffn[l]), w_up[l], conv_ffn[l], w_down[l])
    return rmsnorm(x, norm_final)
```

```python
import functools

import numpy as np
import jax
import jax.numpy as jnp
from jax import lax
from jax.experimental import pallas as pl
from jax.experimental.pallas import tpu as pltpu

D_MODEL = 1024
EPS = 1e-6
NEG_INF = -1e30
CONV_CH = 256
CONV_GROUPS = 4
DN_HEADS = 4
DN_HEAD_DIM = 128
DN_WIDTH = DN_HEADS * DN_HEAD_DIM
SWA_HEAD_DIM = 64
SWA_HEADS = 4
SWA_PATTERNS = ((128, 1), (512, 4), (2048, 16))
SWA_N_PAT = 3
SWA_WIDTH = SWA_HEADS * SWA_HEAD_DIM
SWA_QKV_WIDTH = SWA_N_PAT * SWA_WIDTH
ROPE_THETA = 500000.0
ROPE_DIM = SWA_HEAD_DIM // 4
D_FF = 2816
MIX_WIDTH = CONV_CH + DN_WIDTH + SWA_WIDTH

LANES = 128
SUBLANES = 8
VMEM_LIMIT = 56 * 1024 * 1024

OFF_A = 0
OFF_DN_QKV = OFF_A + 3 * CONV_CH
OFF_DN_GATE = OFF_DN_QKV + 3 * DN_WIDTH
OFF_SWA = OFF_DN_GATE + DN_WIDTH
OFF_SMALL = OFF_SWA + 3 * SWA_QKV_WIDTH
IN_COLS = 5376
PROJ_CN = 768
SMALL_BF, SMALL_BB, SMALL_AF, SMALL_AB = 0, 4, 8, 12

DN_CHUNK = 256
ROW_CHUNK = 256
HALO = SUBLANES
HALO_BF16 = 2 * SUBLANES
FF_CHUNK = 256
FFN_TM = 1024
OUT_TM = 1024
SWA_QB = 128
SWA_RADIUS = 64


def _cparams(sem):
    return pltpu.CompilerParams(dimension_semantics=sem, vmem_limit_bytes=VMEM_LIMIT)


def _bf(x):
    return x.astype(jnp.bfloat16)


def _dot(a, b):
    return jnp.dot(a, b, preferred_element_type=jnp.float32)


def _dot_nt(a, b):
    return lax.dot_general(a, b, (((1,), (1,)), ((), ())), preferred_element_type=jnp.float32)


def _silu(x):
    return x * (1.0 / (1.0 + jnp.exp(-x)))


def _conv3_ext(ext, w, rows, halo=HALO):
    n = ext.shape[0]
    prev = pltpu.roll(ext, 1, 0)[halo:halo + rows]
    nxt = pltpu.roll(ext, n - 1, 0)[halo:halo + rows]
    cur = ext[halo:halo + rows]
    return prev * w[0:1] + cur * w[1:2] + nxt * w[2:3]


def _proj_kernel(x_ref, g_ref, w_ref, o_ref, h_ref):
    @pl.when(pl.program_id(1) == 0)
    def _():
        def body(i, c):
            rows = pl.ds(pl.multiple_of(i * ROW_CHUNK, ROW_CHUNK), ROW_CHUNK)
            x = x_ref[rows, :]
            y = x * lax.rsqrt(jnp.mean(x * x, axis=-1, keepdims=True) + EPS)
            h_ref[rows, :] = _bf(y * g_ref[...])
            return c
        lax.fori_loop(0, x_ref.shape[0] // ROW_CHUNK, body, 0)

    o_ref[...] = _dot(h_ref[...], w_ref[...])


def _proj_in(x2d, g_row, w_bf, batch, seq):
    ncol = IN_COLS // PROJ_CN
    return pl.pallas_call(
        _proj_kernel,
        grid=(batch, ncol),
        in_specs=[
            pl.BlockSpec((seq, D_MODEL), lambda b, j: (b, 0)),
            pl.BlockSpec((1, D_MODEL), lambda b, j: (0, 0)),
            pl.BlockSpec((D_MODEL, PROJ_CN), lambda b, j: (0, j)),
        ],
        out_specs=pl.BlockSpec((seq, PROJ_CN), lambda b, j: (b, j)),
        out_shape=jax.ShapeDtypeStruct((batch * seq, IN_COLS), jnp.float32),
        scratch_shapes=[pltpu.VMEM((seq, D_MODEL), jnp.bfloat16)],
        compiler_params=_cparams(("arbitrary", "arbitrary")),
        name="proj_in",
    )(x2d, g_row, w_bf)


def _mixa_kernel(p_ref, cw_ref, ng_ref, gm_ref, o_ref, u_ref):
    seq = o_ref.shape[0]
    nchunk = seq // ROW_CHUNK
    zeros = jnp.zeros((HALO, CONV_CH), jnp.float32)
    u_ref[0:HALO, :] = zeros
    u_ref[HALO + seq:HALO + seq + HALO, :] = zeros

    def fill(i, c):
        rows = pl.ds(pl.multiple_of(i * ROW_CHUNK, ROW_CHUNK), ROW_CHUNK)
        xa = p_ref[rows, 0:CONV_CH]
        gc = p_ref[rows, 2 * CONV_CH:3 * CONV_CH]
        u_ref[pl.ds(pl.multiple_of(i * ROW_CHUNK + HALO, HALO), ROW_CHUNK), :] = gc * xa
        return c
    lax.fori_loop(0, nchunk, fill, 0)

    def body(i, c):
        start = pl.multiple_of(i * ROW_CHUNK, ROW_CHUNK)
        ext = u_ref[pl.ds(start, ROW_CHUNK + 2 * HALO), :]
        gb = p_ref[pl.ds(start, ROW_CHUNK), CONV_CH:2 * CONV_CH]
        y = gb * _conv3_ext(ext, cw_ref[...], ROW_CHUNK)
        sq = y * y
        hi = _bf(sq)
        lo = _bf(sq - hi.astype(jnp.float32))
        ms = (_dot(hi, gm_ref[...]) + _dot(lo, gm_ref[...])) * (1.0 / (CONV_CH // CONV_GROUPS))
        o_ref[pl.ds(start, ROW_CHUNK), :] = y * lax.rsqrt(ms + EPS) * ng_ref[...]
        return c
    lax.fori_loop(0, nchunk, body, 0)


def _mix_a(proj, conv_a, norm_a_row, group_mat, batch, seq):
    return pl.pallas_call(
        _mixa_kernel,
        grid=(batch,),
        in_specs=[
            pl.BlockSpec((seq, 3 * CONV_CH), lambda b: (b, OFF_A // (3 * CONV_CH))),
            pl.BlockSpec((3, CONV_CH), lambda b: (0, 0)),
            pl.BlockSpec((1, CONV_CH), lambda b: (0, 0)),
            pl.BlockSpec((CONV_CH, CONV_CH), lambda b: (0, 0)),
        ],
        out_specs=pl.BlockSpec((seq, CONV_CH), lambda b: (b, 0)),
        out_shape=jax.ShapeDtypeStruct((batch * seq, CONV_CH), jnp.float32),
        scratch_shapes=[pltpu.VMEM((seq + 2 * HALO, CONV_CH), jnp.float32)],
        compiler_params=_cparams(("arbitrary",)),
        name="mix_a",
    )(proj, conv_a, norm_a_row, group_mat)


def _tri_inverse(a, xor_ij, eye):
    n = a.shape[0]
    d = eye - jnp.where(xor_ij == 1, a, 0.0)
    level = 1
    while (1 << level) < n:
        e = jnp.where((xor_ij >> level) == 1, a, 0.0)
        de = _dot(_bf(d), _bf(e))
        d = d - _dot(_bf(de), _bf(d))
        level += 1
    return d


def _dn_chunk(q, k, v, beta, gcol, grow, s, lower, ii, jj, xor_ij, eye):
    c = q.shape[0]
    if lower:
        incl, strict, glast = ii >= jj, ii > jj, gcol[c - 1:c, :]
    else:
        incl, strict, glast = ii <= jj, ii < jj, gcol[0:1, :]
    diff = jnp.concatenate([gcol] * (c // LANES), axis=1) - grow
    dec = jnp.where(incl, jnp.exp(jnp.where(incl, diff, 0.0)), 0.0)
    kb = k * beta
    k_bf = _bf(k)
    a = jnp.where(strict, _dot_nt(_bf(kb), k_bf) * dec, 0.0)
    qk = _dot_nt(_bf(q), k_bf) * dec
    tinv = _tri_inverse(a, xor_ij, eye)
    eg = jnp.exp(gcol)
    rhs = jnp.concatenate([v * beta, kb * eg], axis=1)
    sol = _dot(_bf(tinv), _bf(rhs))
    u, w = sol[:, :DN_HEAD_DIM], sol[:, DN_HEAD_DIM:]
    s_bf = _bf(s)
    v_new = u - _dot(_bf(w), s_bf)
    o = _dot(_bf(q * eg), s_bf) + _dot(_bf(qk), _bf(v_new))
    kd = k * jnp.exp(glast - gcol)
    s_new = s * jnp.exp(glast) + _dot(_bf(kd.T), _bf(v_new))
    return o, s_new


def _dn_kernel(q_ref, k_ref, v_ref, gate_ref, small_ref, cq_ref, ck_ref, cv_ref, alog_ref, dtb_ref,
               ng_ref, o_ref,
               pad_ref, qn_ref, kn_ref, vn_ref, bf_ref, bb_ref, gf_ref, gb_ref, rf_ref, rb_ref,
               of_ref, ob_ref, s_ref):
    seq = o_ref.shape[0]
    head = pl.program_id(1)
    nrow = seq // ROW_CHUNK
    nchunk = seq // DN_CHUNK

    zeros = jnp.zeros((HALO, DN_HEAD_DIM), jnp.float32)
    pad_ref[0:HALO, :] = zeros
    pad_ref[HALO + seq:HALO + seq + HALO, :] = zeros

    def conv_silu(src_ref, cw_ref, dst_ref, mode):
        def fill(i, c):
            rows = pl.ds(pl.multiple_of(i * ROW_CHUNK, ROW_CHUNK), ROW_CHUNK)
            pad_ref[pl.ds(pl.multiple_of(i * ROW_CHUNK + HALO, HALO), ROW_CHUNK), :] = src_ref[rows, :]
            return c
        lax.fori_loop(0, nrow, fill, 0)

        def body(i, c):
            start = pl.multiple_of(i * ROW_CHUNK, ROW_CHUNK)
            ext = pad_ref[pl.ds(start, ROW_CHUNK + 2 * HALO), :]
            y = _silu(_conv3_ext(ext, cw_ref[...], ROW_CHUNK))
            if mode != "v":
                y = y * lax.rsqrt(jnp.sum(y * y, axis=-1, keepdims=True) + EPS)
            if mode == "q":
                y = y * (DN_HEAD_DIM ** -0.5)
            dst_ref[pl.ds(start, ROW_CHUNK), :] = y
            return c
        lax.fori_loop(0, nrow, body, 0)

    conv_silu(q_ref, cq_ref, qn_ref, "q")
    conv_silu(k_ref, ck_ref, kn_ref, "k")
    conv_silu(v_ref, cv_ref, vn_ref, "v")

    lane = lax.broadcasted_iota(jnp.int32, (DN_CHUNK, LANES), 1)
    rin = lax.broadcasted_iota(jnp.int32, (DN_CHUNK, LANES), 0)

    def pick(x, col):
        sel = jnp.sum(jnp.where(lane == col, x, 0.0), axis=-1, keepdims=True)
        return jnp.broadcast_to(sel, x.shape)

    neg_a = -jnp.exp(alog_ref[...])

    def gates(ci, c):
        rows = pl.ds(pl.multiple_of(ci * DN_CHUNK, DN_CHUNK), DN_CHUNK)
        sm = small_ref[rows, :]
        beta = 1.0 / (1.0 + jnp.exp(-sm))
        z = sm + dtb_ref[...]
        g = neg_a * (jnp.maximum(z, 0.0) + jnp.log1p(jnp.exp(-jnp.abs(z))))
        cf, cb = g, g
        step = 1
        while step < DN_CHUNK:
            cf = cf + jnp.where(rin >= step, pltpu.roll(cf, step, 0), 0.0)
            cb = cb + jnp.where(rin < DN_CHUNK - step, pltpu.roll(cb, DN_CHUNK - step, 0), 0.0)
            step *= 2
        bf_ref[rows, :] = pick(beta, SMALL_BF + head)
        bb_ref[rows, :] = pick(beta, SMALL_BB + head)
        gf_ref[rows, :] = pick(cf, SMALL_AF + head)
        gb_ref[rows, :] = pick(cb, SMALL_AB + head)
        rf_ref[ci] = cf.T
        rb_ref[ci] = cb.T
        return c
    lax.fori_loop(0, nchunk, gates, 0)

    ii = lax.broadcasted_iota(jnp.int32, (DN_CHUNK, DN_CHUNK), 0)
    jj = lax.broadcasted_iota(jnp.int32, (DN_CHUNK, DN_CHUNK), 1)
    xor_ij = ii ^ jj
    eye = jnp.where(ii == jj, 1.0, 0.0)
    s_ref[...] = jnp.zeros_like(s_ref)

    def scan(i, c):
        for direction in (0, 1):
            ci = i if direction == 0 else nchunk - 1 - i
            rows = pl.ds(pl.multiple_of(ci * DN_CHUNK, DN_CHUNK), DN_CHUNK)
            if direction == 0:
                beta, gcol = bf_ref[rows, :], gf_ref[rows, :]
                grow = rf_ref[ci, pl.ds(SMALL_AF + head, 1), :]
            else:
                beta, gcol = bb_ref[rows, :], gb_ref[rows, :]
                grow = rb_ref[ci, pl.ds(SMALL_AB + head, 1), :]
            o, s_new = _dn_chunk(qn_ref[rows, :], kn_ref[rows, :], vn_ref[rows, :], beta, gcol, grow,
                                 s_ref[direction], direction == 0, ii, jj, xor_ij, eye)
            s_ref[direction] = s_new
            if direction == 0:
                of_ref[rows, :] = o
            else:
                ob_ref[rows, :] = o
        return c
    lax.fori_loop(0, nchunk, scan, 0)

    def fin(i, c):
        rows = pl.ds(pl.multiple_of(i * ROW_CHUNK, ROW_CHUNK), ROW_CHUNK)
        o = of_ref[rows, :] + ob_ref[rows, :]
        o = o * lax.rsqrt(jnp.mean(o * o, axis=-1, keepdims=True) + EPS) * ng_ref[...]
        o_ref[rows, :] = o * _silu(gate_ref[rows, :])
        return c
    lax.fori_loop(0, nrow, fin, 0)


def _deltanet(proj, conv_qkv, alog_row, dtb_row, norm_dn_row, batch, seq):
    nchunk = seq // DN_CHUNK
    hd = DN_HEAD_DIM
    qb, gb, sb = OFF_DN_QKV // hd, OFF_DN_GATE // hd, OFF_SMALL // LANES
    seq_buf = pltpu.VMEM((seq, hd), jnp.float32)
    return pl.pallas_call(
        _dn_kernel,
        grid=(batch, DN_HEADS),
        in_specs=[
            pl.BlockSpec((seq, hd), lambda b, h: (b, qb + h)),
            pl.BlockSpec((seq, hd), lambda b, h: (b, qb + DN_HEADS + h)),
            pl.BlockSpec((seq, hd), lambda b, h: (b, qb + 2 * DN_HEADS + h)),
            pl.BlockSpec((seq, hd), lambda b, h: (b, gb + h)),
            pl.BlockSpec((seq, LANES), lambda b, h: (b, sb)),
            pl.BlockSpec((3, hd), lambda b, h: (0, h)),
            pl.BlockSpec((3, hd), lambda b, h: (0, DN_HEADS + h)),
            pl.BlockSpec((3, hd), lambda b, h: (0, 2 * DN_HEADS + h)),
            pl.BlockSpec((1, LANES), lambda b, h: (0, 0)),
            pl.BlockSpec((1, LANES), lambda b, h: (0, 0)),
            pl.BlockSpec((1, hd), lambda b, h: (0, 0)),
        ],
        out_specs=pl.BlockSpec((seq, hd), lambda b, h: (b, h)),
        out_shape=jax.ShapeDtypeStruct((batch * seq, DN_WIDTH), jnp.float32),
        scratch_shapes=[
            pltpu.VMEM((seq + 2 * HALO, hd), jnp.float32),
            seq_buf, seq_buf, seq_buf,
            seq_buf, seq_buf, seq_buf, seq_buf,
            pltpu.VMEM((nchunk, LANES, DN_CHUNK), jnp.float32),
            pltpu.VMEM((nchunk, LANES, DN_CHUNK), jnp.float32),
            seq_buf, seq_buf,
            pltpu.VMEM((2, hd, hd), jnp.float32),
        ],
        compiler_params=_cparams(("arbitrary", "arbitrary")),
        name="deltanet",
    )(proj, proj, proj, proj, proj, conv_qkv, conv_qkv, conv_qkv, alog_row, dtb_row, norm_dn_row)


def _swa_kernel(q_ref, k_ref, v_ref, pos_ref, invf_ref, ng_ref, o_ref,
                cos_ref, s1_ref, s2_ref, qr_ref, kr_ref, m_ref, l_ref, acc_ref):
    seq = o_ref.shape[0]
    hp = pl.program_id(1)
    pat = pl.program_id(2)
    nrow = seq // ROW_CHUNK
    lane = lax.broadcasted_iota(jnp.int32, (ROW_CHUNK, LANES), 1)
    dim = lane & (SWA_HEAD_DIM - 1)
    half = ROPE_DIM // 2

    @pl.when((hp == 0) & (pat == 0))
    def _():
        def body(i, c):
            rows = pl.ds(pl.multiple_of(i * ROW_CHUNK, ROW_CHUNK), ROW_CHUNK)
            ang = pos_ref[rows, :].astype(jnp.float32) * invf_ref[...]
            cs, sn = jnp.cos(ang), jnp.sin(ang)
            cos_ref[rows, :] = cs
            s1_ref[rows, :] = jnp.where(dim < half, -sn, 0.0)
            s2_ref[rows, :] = jnp.where((dim >= half) & (dim < ROPE_DIM), sn, 0.0)
            return c
        lax.fori_loop(0, nrow, body, 0)

    @pl.when(pat == 0)
    def _():
        m_ref[...] = jnp.full(m_ref.shape, NEG_INF, jnp.float32)
        l_ref[...] = jnp.zeros_like(l_ref)
        acc_ref[...] = jnp.zeros_like(acc_ref)

    def rope(i, c):
        rows = pl.ds(pl.multiple_of(i * ROW_CHUNK, ROW_CHUNK), ROW_CHUNK)
        cs, s1, s2 = cos_ref[rows, :], s1_ref[rows, :], s2_ref[rows, :]
        for src, dst, scale in ((q_ref, qr_ref, SWA_HEAD_DIM ** -0.5), (k_ref, kr_ref, None)):
            x = src[rows, :]
            y = x * cs + pltpu.roll(x, LANES - half, 1) * s1 + pltpu.roll(x, half, 1) * s2
            dst[rows, :] = y if scale is None else y * scale
        return c
    lax.fori_loop(0, nrow, rope, 0)

    def run_pattern(dil):
        sub_len = seq // dil
        nblk = sub_len // SWA_QB
        win = min(sub_len, SWA_QB + 2 * SWA_RADIUS)
        lane_q = lax.broadcasted_iota(jnp.int32, (SWA_QB, LANES), 1)
        head_lo = lane_q < SWA_HEAD_DIM
        qi = lax.broadcasted_iota(jnp.int32, (SWA_QB, win), 0)
        kj = lax.broadcasted_iota(jnp.int32, (SWA_QB, win), 1)

        def sl(start, size):
            return pl.ds(start, size) if dil == 1 else pl.ds(start, size, stride=dil)

        def body(it, c):
            res = it // nblk
            qs = (it % nblk) * SWA_QB
            ws = jnp.clip(qs - SWA_RADIUS, 0, sub_len - win)
            rq = sl(res + qs * dil, SWA_QB)
            rk = sl(res + ws * dil, win)
            qb = qr_ref[rq, :]
            kw = _bf(kr_ref[rk, :])
            vw = _bf(v_ref[rk, :])
            valid = jnp.abs((qi + qs) - (kj + ws)) <= SWA_RADIUS
            m_old, l_old, acc_old = m_ref[rq, :], l_ref[rq, :], acc_ref[rq, :]
            m_out, l_out, acc_out = [], [], []
            for hh in (0, 1):
                lm = head_lo if hh == 0 else jnp.logical_not(head_lo)
                s = _dot_nt(_bf(jnp.where(lm, qb, 0.0)), kw)
                s = jnp.where(valid, s, NEG_INF)
                col = hh * SWA_HEAD_DIM
                m_o, l_o = m_old[:, col:col + 1], l_old[:, col:col + 1]
                m_n = jnp.maximum(m_o, jnp.max(s, axis=-1, keepdims=True))
                alpha = jnp.exp(m_o - m_n)
                p = jnp.exp(s - m_n)
                l_n = alpha * l_o + jnp.sum(p, axis=-1, keepdims=True)
                pv = _dot(_bf(p), vw)
                m_out.append(m_n)
                l_out.append(l_n)
                acc_out.append(alpha * acc_old + pv)
            m_ref[rq, :] = jnp.where(head_lo, m_out[0], m_out[1])
            l_ref[rq, :] = jnp.where(head_lo, l_out[0], l_out[1])
            acc_ref[rq, :] = jnp.where(head_lo, acc_out[0], acc_out[1])
            return c
        lax.fori_loop(0, dil * nblk, body, 0)

    for p_idx, (_, dil) in enumerate(SWA_PATTERNS):
        pl.when(pat == p_idx)(functools.partial(run_pattern, dil))

    @pl.when(pat == SWA_N_PAT - 1)
    def _():
        lo = lane < SWA_HEAD_DIM

        def fin(i, c):
            rows = pl.ds(pl.multiple_of(i * ROW_CHUNK, ROW_CHUNK), ROW_CHUNK)
            o = acc_ref[rows, :] / l_ref[rows, :]
            sq = o * o
            s_lo = jnp.sum(jnp.where(lo, sq, 0.0), axis=-1, keepdims=True)
            s_hi = jnp.sum(jnp.where(lo, 0.0, sq), axis=-1, keepdims=True)
            ms = jnp.where(lo, s_lo, s_hi) * (1.0 / SWA_HEAD_DIM)
            o_ref[rows, :] = o * lax.rsqrt(ms + EPS) * ng_ref[...]
            return c
        lax.fori_loop(0, nrow, fin, 0)


def _swa(proj, pos_col, invf_row, norm_c_row, batch, seq):
    base = OFF_SWA // LANES
    per_pat = SWA_WIDTH // LANES
    qkv = SWA_QKV_WIDTH // LANES
    pairs = SWA_HEADS // 2
    seq_buf = pltpu.VMEM((seq, LANES), jnp.float32)

    def spec(which):
        return pl.BlockSpec((seq, LANES), lambda b, hp, p: (b, base + which * qkv + p * per_pat + hp))

    return pl.pallas_call(
        _swa_kernel,
        grid=(batch, pairs, SWA_N_PAT),
        in_specs=[
            spec(0), spec(1), spec(2),
            pl.BlockSpec((seq, 1), lambda b, hp, p: (b, 0)),
            pl.BlockSpec((1, LANES), lambda b, hp, p: (0, 0)),
            pl.BlockSpec((1, LANES), lambda b, hp, p: (0, hp)),
        ],
        out_specs=pl.BlockSpec((seq, LANES), lambda b, hp, p: (b, hp)),
        out_shape=jax.ShapeDtypeStruct((batch * seq, SWA_WIDTH), jnp.float32),
        scratch_shapes=[seq_buf] * 8,
        compiler_params=_cparams(("arbitrary", "arbitrary", "arbitrary")),
        name="swa",
    )(proj, proj, proj, pos_col, invf_row, norm_c_row)


def _outproj_kernel(x_ref, ya_ref, yb_ref, yc_ref, w_ref, o_ref):
    a0, a1, a2 = CONV_CH, CONV_CH + DN_WIDTH, MIX_WIDTH
    acc = _dot(_bf(ya_ref[...]), w_ref[0:a0, :])
    acc = acc + _dot(_bf(yb_ref[...]), w_ref[a0:a1, :])
    acc = acc + _dot(_bf(yc_ref[...]), w_ref[a1:a2, :])
    o_ref[...] = x_ref[...] + acc


def _out_proj(x2d, y_a, y_b, y_c, w_bf):
    n = x2d.shape[0]
    tm = OUT_TM
    return pl.pallas_call(
        _outproj_kernel,
        grid=(n // tm,),
        in_specs=[
            pl.BlockSpec((tm, D_MODEL), lambda i: (i, 0)),
            pl.BlockSpec((tm, CONV_CH), lambda i: (i, 0)),
            pl.BlockSpec((tm, DN_WIDTH), lambda i: (i, 0)),
            pl.BlockSpec((tm, SWA_WIDTH), lambda i: (i, 0)),
            pl.BlockSpec((MIX_WIDTH, D_MODEL), lambda i: (0, 0)),
        ],
        out_specs=pl.BlockSpec((tm, D_MODEL), lambda i: (i, 0)),
        out_shape=jax.ShapeDtypeStruct((n, D_MODEL), jnp.float32),
        compiler_params=_cparams(("arbitrary",)),
        name="out_proj",
    )(x2d, y_a, y_b, y_c, w_bf)


def _ffn_kernel(xp_ref, x_ref, xn_ref, g_ref, wg_ref, wv_ref, cg_ref, cv_ref, wd_ref, gf_ref, o_ref,
                h_ref, acc_ref, *, tiles_per_seq, final_norm):
    tm = x_ref.shape[0]
    t = pl.program_id(0) % tiles_per_seq

    def norm(x):
        return x * lax.rsqrt(jnp.mean(x * x, axis=-1, keepdims=True) + EPS) * g_ref[...]

    halo = HALO_BF16
    hp = norm(xp_ref[...])
    hn = norm(xn_ref[...])
    h_ref[0:halo, :] = _bf(jnp.where(t == 0, 0.0, hp))
    h_ref[halo + tm:halo + tm + halo, :] = _bf(jnp.where(t == tiles_per_seq - 1, 0.0, hn))

    def fill(i, c):
        rows = pl.ds(pl.multiple_of(i * ROW_CHUNK, ROW_CHUNK), ROW_CHUNK)
        h_ref[pl.ds(pl.multiple_of(i * ROW_CHUNK + halo, halo), ROW_CHUNK), :] = _bf(norm(x_ref[rows, :]))
        return c
    lax.fori_loop(0, tm // ROW_CHUNK, fill, 0)

    acc_ref[...] = jnp.zeros_like(acc_ref)

    def body(j, c):
        h = h_ref[...]
        gate = _conv3_ext(_dot(h, wg_ref[j]), cg_ref[j], tm, halo)
        val = _conv3_ext(_dot(h, wv_ref[j]), cv_ref[j], tm, halo)
        acc_ref[...] += _dot(_bf(_silu(gate) * val), wd_ref[j])
        return c
    lax.fori_loop(0, wg_ref.shape[0], body, 0)

    def fin(i, c):
        rows = pl.ds(pl.multiple_of(i * ROW_CHUNK, ROW_CHUNK), ROW_CHUNK)
        y = x_ref[rows, :] + acc_ref[rows, :]
        if final_norm:
            y = y * lax.rsqrt(jnp.mean(y * y, axis=-1, keepdims=True) + EPS) * gf_ref[...]
        o_ref[rows, :] = y
        return c
    lax.fori_loop(0, tm // ROW_CHUNK, fin, 0)


def _ffn(x2d, g_row, wg, wv, cg, cv, wd, gf_row, seq, final_norm):
    n = x2d.shape[0]
    tm = FFN_TM
    tiles_per_seq = seq // tm
    halo = HALO_BF16
    hb = tm // halo
    nhb = n // halo
    nff = wg.shape[0]
    whole = lambda shape: pl.BlockSpec(shape, lambda i: (0,) * len(shape), pipeline_mode=pl.Buffered(1))
    kern = functools.partial(_ffn_kernel, tiles_per_seq=tiles_per_seq, final_norm=final_norm)
    return pl.pallas_call(
        kern,
        grid=(n // tm,),
        in_specs=[
            pl.BlockSpec((halo, D_MODEL), lambda i: (jnp.maximum(i * hb - 1, 0), 0)),
            pl.BlockSpec((tm, D_MODEL), lambda i: (i, 0)),
            pl.BlockSpec((halo, D_MODEL), lambda i: (jnp.minimum((i + 1) * hb, nhb - 1), 0)),
            whole((1, D_MODEL)),
            whole((nff, D_MODEL, FF_CHUNK)),
            whole((nff, D_MODEL, FF_CHUNK)),
            whole((nff, 3, FF_CHUNK)),
            whole((nff, 3, FF_CHUNK)),
            whole((nff, FF_CHUNK, D_MODEL)),
            whole((1, D_MODEL)),
        ],
        out_specs=pl.BlockSpec((tm, D_MODEL), lambda i: (i, 0)),
        out_shape=jax.ShapeDtypeStruct((n, D_MODEL), jnp.float32),
        scratch_shapes=[
            pltpu.VMEM((tm + 2 * halo, D_MODEL), jnp.bfloat16),
            pltpu.VMEM((tm, D_MODEL), jnp.float32),
        ],
        compiler_params=_cparams(("arbitrary",)),
        name="ffn",
    )(x2d, x2d, x2d, g_row, wg, wv, cg, cv, wd, gf_row)


def _reorder_w_in(w):
    sizes = (CONV_CH,) * 3 + (DN_WIDTH,) * 4 + (DN_HEADS,) * 4 + (SWA_QKV_WIDTH,) * 3
    offs = np.concatenate([[0], np.cumsum(sizes)])
    seg = [w[:, int(offs[i]):int(offs[i + 1])] for i in range(len(sizes))]
    xa, gb, gc, q, k, v, gate, b_f, b_b, a_f, a_b, qc, kc, vc = seg
    pad = jnp.zeros((w.shape[0], IN_COLS - OFF_SMALL - 4 * DN_HEADS), w.dtype)
    out = jnp.concatenate([xa, gb, gc, q, k, v, gate, qc, kc, vc, b_f, b_b, a_f, a_b, pad], axis=1)
    return _bf(out)


def _small_row(f_vals, b_vals):
    row = jnp.zeros((1, LANES), jnp.float32)
    row = row.at[0, SMALL_AF:SMALL_AF + DN_HEADS].set(f_vals.astype(jnp.float32))
    return row.at[0, SMALL_AB:SMALL_AB + DN_HEADS].set(b_vals.astype(jnp.float32))


def _ff_chunks(w, axis):
    n = w.shape[axis] // FF_CHUNK
    if axis == 1:
        return w.reshape(w.shape[0], n, FF_CHUNK).transpose(1, 0, 2)
    return w.reshape(n, FF_CHUNK, w.shape[1])


def kernel(x, positions, norm_mix, w_in, conv_a, norm_a, conv_qkv, a_log_f, a_log_b, dt_bias_f, dt_bias_b,
           norm_dn, norm_c, w_o, norm_ffn, w_up, conv_ffn, w_down, norm_final):
    batch, seq, _ = x.shape
    depth = w_in.shape[0]
    x2d = x.reshape(batch * seq, D_MODEL)
    pos_col = positions.reshape(batch * seq, 1)

    half = ROPE_DIM // 2
    inv_freq = ROPE_THETA ** (-jnp.arange(half, dtype=jnp.float32) / half)
    d_idx = np.arange(LANES) % SWA_HEAD_DIM
    invf_row = jnp.where(jnp.asarray(d_idx < ROPE_DIM), inv_freq[d_idx % half], 0.0).reshape(1, LANES)
    grp = np.arange(CONV_CH) // (CONV_CH // CONV_GROUPS)
    group_mat = _bf(jnp.asarray(grp[:, None] == grp[None, :], jnp.float32))

    for l in range(depth):
        proj = _proj_in(x2d, norm_mix[l].reshape(1, D_MODEL), _reorder_w_in(w_in[l]), batch, seq)
        y_a = _mix_a(proj, conv_a[l], norm_a[l].reshape(1, CONV_CH), group_mat, batch, seq)
        y_b = _deltanet(proj, conv_qkv[l], _small_row(a_log_f[l], a_log_b[l]),
                        _small_row(dt_bias_f[l], dt_bias_b[l]), norm_dn[l].reshape(1, DN_HEAD_DIM), batch, seq)
        y_c = _swa(proj, pos_col, invf_row, norm_c[l].reshape(1, SWA_WIDTH), batch, seq)
        x2d = _out_proj(x2d, y_a, y_b, y_c, _bf(w_o[l]))
        wu = w_up[l]
        cf = conv_ffn[l]
        x2d = _ffn(
            x2d, norm_ffn[l].reshape(1, D_MODEL),
            _bf(_ff_chunks(wu[:, :D_FF], 1)), _bf(_ff_chunks(wu[:, D_FF:], 1)),
            _ff_chunks(cf[:, :D_FF], 1), _ff_chunks(cf[:, D_FF:], 1),
            _bf(_ff_chunks(w_down[l], 0)), norm_final.reshape(1, D_MODEL), seq, l == depth - 1)
    return x2d.reshape(batch, seq, D_MODEL)
```

```python
import functools

import numpy as np
import jax
import jax.numpy as jnp
from jax import lax
from jax.experimental import pallas as pl
from jax.experimental.pallas import tpu as pltpu

D_MODEL = 1024
EPS = 1e-6
NEG_INF = -1e30
CONV_CH = 256
CONV_GROUPS = 4
DN_HEADS = 4
DN_HEAD_DIM = 128
DN_WIDTH = DN_HEADS * DN_HEAD_DIM
SWA_HEAD_DIM = 64
SWA_HEADS = 4
SWA_PATTERNS = ((128, 1), (512, 4), (2048, 16))
SWA_N_PAT = 3
SWA_WIDTH = SWA_HEADS * SWA_HEAD_DIM
SWA_QKV_WIDTH = SWA_N_PAT * SWA_WIDTH
ROPE_THETA = 500000.0
ROPE_DIM = SWA_HEAD_DIM // 4
D_FF = 2816
MIX_WIDTH = CONV_CH + DN_WIDTH + SWA_WIDTH

LANES = 128
SUBLANES = 8
VMEM_LIMIT = 56 * 1024 * 1024

OFF_A = 0
OFF_DN_QKV = OFF_A + 3 * CONV_CH
OFF_DN_GATE = OFF_DN_QKV + 3 * DN_WIDTH
OFF_SWA = OFF_DN_GATE + DN_WIDTH
OFF_SMALL = OFF_SWA + 3 * SWA_QKV_WIDTH
IN_COLS = 5376
PROJ_CN = 768
SMALL_BF, SMALL_BB, SMALL_AF, SMALL_AB = 0, 4, 8, 12

DN_CHUNK = 256
DN_INTRA_UNROLL = 2
ROW_CHUNK = 256
HALO = SUBLANES
HALO_BF16 = 2 * SUBLANES
FF_CHUNK = 256
FFN_TM = 1024
OUT_TM = 1024
SWA_QB = 128
SWA_RADIUS = 64
SWA_UNROLL = 4


def _cparams(sem):
    return pltpu.CompilerParams(dimension_semantics=sem, vmem_limit_bytes=VMEM_LIMIT)


def _bf(x):
    return x.astype(jnp.bfloat16)


def _dot(a, b):
    return jnp.dot(a, b, preferred_element_type=jnp.float32)


def _dot_nt(a, b):
    return lax.dot_general(a, b, (((1,), (1,)), ((), ())), preferred_element_type=jnp.float32)


def _silu(x):
    return x * (1.0 / (1.0 + jnp.exp(-x)))


def _conv3_val(ext, w, rows, halo):
    n = ext.shape[0]
    prev = pltpu.roll(ext, 1, 0)[halo:halo + rows]
    nxt = pltpu.roll(ext, n - 1, 0)[halo:halo + rows]
    return prev * w[0:1] + ext[halo:halo + rows] * w[1:2] + nxt * w[2:3]


def _conv3_ref(ref, lead, start, rows, w):
    def tap(off):
        return ref[lead + (pl.ds(start + off, rows), slice(None))]
    return tap(-1) * w[0:1] + tap(0) * w[1:2] + tap(1) * w[2:3]


def _proj_kernel(x_ref, g_ref, w_ref, o_ref, h_ref):
    @pl.when(pl.program_id(1) == 0)
    def _():
        def body(i, c):
            rows = pl.ds(pl.multiple_of(i * ROW_CHUNK, ROW_CHUNK), ROW_CHUNK)
            x = x_ref[rows, :]
            y = x * lax.rsqrt(jnp.mean(x * x, axis=-1, keepdims=True) + EPS)
            h_ref[rows, :] = _bf(y * g_ref[...])
            return c
        lax.fori_loop(0, x_ref.shape[0] // ROW_CHUNK, body, 0)

    o_ref[...] = _dot(h_ref[...], w_ref[...])


def _proj_in(x2d, g_row, w_bf, batch, seq):
    ncol = IN_COLS // PROJ_CN
    return pl.pallas_call(
        _proj_kernel,
        grid=(batch, ncol),
        in_specs=[
            pl.BlockSpec((seq, D_MODEL), lambda b, j: (b, 0)),
            pl.BlockSpec((1, D_MODEL), lambda b, j: (0, 0)),
            pl.BlockSpec((D_MODEL, PROJ_CN), lambda b, j: (0, j)),
        ],
        out_specs=pl.BlockSpec((seq, PROJ_CN), lambda b, j: (b, j)),
        out_shape=jax.ShapeDtypeStruct((batch * seq, IN_COLS), jnp.float32),
        scratch_shapes=[pltpu.VMEM((seq, D_MODEL), jnp.bfloat16)],
        compiler_params=_cparams(("arbitrary", "arbitrary")),
        name="proj_in",
    )(x2d, g_row, w_bf)


def _mixa_kernel(p_ref, cw_ref, ng_ref, gm_ref, o_ref, u_ref):
    seq = o_ref.shape[0]
    nchunk = seq // ROW_CHUNK
    zeros = jnp.zeros((HALO, CONV_CH), jnp.float32)
    u_ref[0:HALO, :] = zeros
    u_ref[HALO + seq:HALO + seq + HALO, :] = zeros

    def fill(i, c):
        rows = pl.ds(pl.multiple_of(i * ROW_CHUNK, ROW_CHUNK), ROW_CHUNK)
        xa = p_ref[rows, 0:CONV_CH]
        gc = p_ref[rows, 2 * CONV_CH:3 * CONV_CH]
        u_ref[pl.ds(pl.multiple_of(i * ROW_CHUNK + HALO, HALO), ROW_CHUNK), :] = gc * xa
        return c
    lax.fori_loop(0, nchunk, fill, 0)

    for i in range(nchunk):
        start = i * ROW_CHUNK
        gb = p_ref[start:start + ROW_CHUNK, CONV_CH:2 * CONV_CH]
        y = gb * _conv3_ref(u_ref, (), start + HALO, ROW_CHUNK, cw_ref[...])
        sq = y * y
        hi = _bf(sq)
        lo = _bf(sq - hi.astype(jnp.float32))
        ms = (_dot(hi, gm_ref[...]) + _dot(lo, gm_ref[...])) * (1.0 / (CONV_CH // CONV_GROUPS))
        o_ref[start:start + ROW_CHUNK, :] = y * lax.rsqrt(ms + EPS) * ng_ref[...]


def _mix_a(proj, conv_a, norm_a_row, group_mat, batch, seq):
    return pl.pallas_call(
        _mixa_kernel,
        grid=(batch,),
        in_specs=[
            pl.BlockSpec((seq, 3 * CONV_CH), lambda b: (b, OFF_A // (3 * CONV_CH))),
            pl.BlockSpec((3, CONV_CH), lambda b: (0, 0)),
            pl.BlockSpec((1, CONV_CH), lambda b: (0, 0)),
            pl.BlockSpec((CONV_CH, CONV_CH), lambda b: (0, 0)),
        ],
        out_specs=pl.BlockSpec((seq, CONV_CH), lambda b: (b, 0)),
        out_shape=jax.ShapeDtypeStruct((batch * seq, CONV_CH), jnp.float32),
        scratch_shapes=[pltpu.VMEM((seq + 2 * HALO, CONV_CH), jnp.float32)],
        compiler_params=_cparams(("arbitrary",)),
        name="mix_a",
    )(proj, conv_a, norm_a_row, group_mat)


def _tri_inverse_group(a_list, xor_ij, eye):
    n = a_list[0].shape[0]
    d_list = [eye - jnp.where(xor_ij == 1, a, 0.0) for a in a_list]
    level = 1
    while (1 << level) < n:
        mask = (xor_ij >> level) == 1
        de_list = [_dot(_bf(d), _bf(jnp.where(mask, a, 0.0))) for d, a in zip(d_list, a_list)]
        d_list = [d - _dot(_bf(de), _bf(d)) for d, de in zip(d_list, de_list)]
        level += 1
    return d_list


def _dn_intra_group(items, ii, jj, xor_ij, eye):
    c = items[0][0].shape[0]
    pre = []
    for q, k, v, beta, gcol, grow, lower in items:
        if lower:
            incl, strict, glast = ii >= jj, ii > jj, gcol[c - 1:c, :]
        else:
            incl, strict, glast = ii <= jj, ii < jj, gcol[0:1, :]
        diff = jnp.concatenate([gcol] * (c // LANES), axis=1) - grow
        dec = jnp.where(incl, jnp.exp(jnp.where(incl, diff, 0.0)), 0.0)
        kb = k * beta
        k_bf = _bf(k)
        a = jnp.where(strict, _dot_nt(_bf(kb), k_bf) * dec, 0.0)
        qk = _bf(_dot_nt(_bf(q), k_bf) * dec)
        pre.append((a, qk, kb, glast))
    tinv_list = _tri_inverse_group([p[0] for p in pre], xor_ij, eye)
    out = []
    for (q, k, v, beta, gcol, grow, lower), (a, qk, kb, glast), tinv in zip(items, pre, tinv_list):
        eg = jnp.exp(gcol)
        rhs = jnp.concatenate([v * beta, kb * eg], axis=1)
        sol = _dot(_bf(tinv), _bf(rhs))
        u, w = sol[:, :DN_HEAD_DIM], sol[:, DN_HEAD_DIM:]
        kd = k * jnp.exp(glast - gcol)
        out.append((u, _bf(w), qk, _bf(q * eg), _bf(kd.T), jnp.exp(glast)))
    return out


def _dn_kernel(q_ref, k_ref, v_ref, gate_ref, small_ref, cq_ref, ck_ref, cv_ref, alog_ref, dtb_ref,
               ng_ref, o_ref,
               pad_ref, qn_ref, kn_ref, vn_ref, bf_ref, bb_ref, gf_ref, gb_ref, rf_ref, rb_ref,
               of_ref, ob_ref, s_ref, u_ref, w_ref, qk_ref, qd_ref, kdt_ref, gl_ref):
    seq = o_ref.shape[0]
    head = pl.program_id(1)
    nrow = seq // ROW_CHUNK
    nchunk = seq // DN_CHUNK

    zeros = jnp.zeros((HALO, DN_HEAD_DIM), jnp.float32)
    pad_ref[0:HALO, :] = zeros
    pad_ref[HALO + seq:HALO + seq + HALO, :] = zeros

    def conv_silu(src_ref, cw_ref, dst_ref, mode):
        def fill(i, c):
            rows = pl.ds(pl.multiple_of(i * ROW_CHUNK, ROW_CHUNK), ROW_CHUNK)
            pad_ref[pl.ds(pl.multiple_of(i * ROW_CHUNK + HALO, HALO), ROW_CHUNK), :] = src_ref[rows, :]
            return c
        lax.fori_loop(0, nrow, fill, 0)

        for i in range(nrow):
            start = i * ROW_CHUNK
            y = _silu(_conv3_ref(pad_ref, (), start + HALO, ROW_CHUNK, cw_ref[...]))
            if mode != "v":
                y = y * lax.rsqrt(jnp.sum(y * y, axis=-1, keepdims=True) + EPS)
            if mode == "q":
                y = y * (DN_HEAD_DIM ** -0.5)
            dst_ref[start:start + ROW_CHUNK, :] = y

    conv_silu(q_ref, cq_ref, qn_ref, "q")
    conv_silu(k_ref, ck_ref, kn_ref, "k")
    conv_silu(v_ref, cv_ref, vn_ref, "v")

    lane = lax.broadcasted_iota(jnp.int32, (DN_CHUNK, LANES), 1)
    rin = lax.broadcasted_iota(jnp.int32, (DN_CHUNK, LANES), 0)

    def pick(x, col):
        sel = jnp.sum(jnp.where(lane == col, x, 0.0), axis=-1, keepdims=True)
        return jnp.broadcast_to(sel, x.shape)

    neg_a = -jnp.exp(alog_ref[...])

    def gates(ci, c):
        rows = pl.ds(pl.multiple_of(ci * DN_CHUNK, DN_CHUNK), DN_CHUNK)
        sm = small_ref[rows, :]
        beta = 1.0 / (1.0 + jnp.exp(-sm))
        z = sm + dtb_ref[...]
        g = neg_a * (jnp.maximum(z, 0.0) + jnp.log1p(jnp.exp(-jnp.abs(z))))
        cf, cb = g, g
        step = 1
        while step < DN_CHUNK:
            cf = cf + jnp.where(rin >= step, pltpu.roll(cf, step, 0), 0.0)
            cb = cb + jnp.where(rin < DN_CHUNK - step, pltpu.roll(cb, DN_CHUNK - step, 0), 0.0)
            step *= 2
        bf_ref[rows, :] = pick(beta, SMALL_BF + head)
        bb_ref[rows, :] = pick(beta, SMALL_BB + head)
        gf_ref[rows, :] = pick(cf, SMALL_AF + head)
        gb_ref[rows, :] = pick(cb, SMALL_AB + head)
        rf_ref[ci] = cf.T
        rb_ref[ci] = cb.T
        return c
    lax.fori_loop(0, nchunk, gates, 0)

    ii = lax.broadcasted_iota(jnp.int32, (DN_CHUNK, DN_CHUNK), 0)
    jj = lax.broadcasted_iota(jnp.int32, (DN_CHUNK, DN_CHUNK), 1)
    xor_ij = ii ^ jj
    eye = jnp.where(ii == jj, 1.0, 0.0)

    def intra(i, c):
        items, where = [], []
        for sub in range(DN_INTRA_UNROLL):
            ci = i * DN_INTRA_UNROLL + sub
            rows = pl.ds(pl.multiple_of(ci * DN_CHUNK, DN_CHUNK), DN_CHUNK)
            for direction in (0, 1):
                if direction == 0:
                    beta, gcol = bf_ref[rows, :], gf_ref[rows, :]
                    grow = rf_ref[ci, pl.ds(SMALL_AF + head, 1), :]
                else:
                    beta, gcol = bb_ref[rows, :], gb_ref[rows, :]
                    grow = rb_ref[ci, pl.ds(SMALL_AB + head, 1), :]
                items.append((qn_ref[rows, :], kn_ref[rows, :], vn_ref[rows, :], beta, gcol, grow, direction == 0))
                where.append((direction, ci, rows))
        for (direction, ci, rows), (u, w, qk, qd, kdt, gl) in zip(where, _dn_intra_group(items, ii, jj, xor_ij, eye)):
            u_ref[direction, rows, :] = u
            w_ref[direction, rows, :] = w
            qk_ref[direction, rows, :] = qk
            qd_ref[direction, rows, :] = qd
            kdt_ref[direction, ci] = kdt
            gl_ref[direction, ci] = jnp.broadcast_to(gl, (SUBLANES, LANES))
        return c
    lax.fori_loop(0, nchunk // DN_INTRA_UNROLL, intra, 0)

    s_ref[...] = jnp.zeros_like(s_ref)

    def scan(i, c):
        cis = (i, nchunk - 1 - i)
        rows = [pl.ds(pl.multiple_of(ci * DN_CHUNK, DN_CHUNK), DN_CHUNK) for ci in cis]
        s_old = [s_ref[d] for d in (0, 1)]
        s_bf = [_bf(s) for s in s_old]
        v_new = [_bf(u_ref[d, rows[d], :] - _dot(w_ref[d, rows[d], :], s_bf[d])) for d in (0, 1)]
        s_new = [s_old[d] * gl_ref[d, cis[d], 0:1, :] + _dot(kdt_ref[d, cis[d]], v_new[d]) for d in (0, 1)]
        o = [_dot(qd_ref[d, rows[d], :], s_bf[d]) + _dot(qk_ref[d, rows[d], :], v_new[d]) for d in (0, 1)]
        for d, out_ref in ((0, of_ref), (1, ob_ref)):
            s_ref[d] = s_new[d]
            out_ref[rows[d], :] = o[d]
        return c
    lax.fori_loop(0, nchunk, scan, 0)

    def fin(i, c):
        rows = pl.ds(pl.multiple_of(i * ROW_CHUNK, ROW_CHUNK), ROW_CHUNK)
        o = of_ref[rows, :] + ob_ref[rows, :]
        o = o * lax.rsqrt(jnp.mean(o * o, axis=-1, keepdims=True) + EPS) * ng_ref[...]
        o_ref[rows, :] = o * _silu(gate_ref[rows, :])
        return c
    lax.fori_loop(0, nrow, fin, 0)


def _deltanet(proj, conv_qkv, alog_row, dtb_row, norm_dn_row, batch, seq):
    nchunk = seq // DN_CHUNK
    hd = DN_HEAD_DIM
    qb, gb, sb = OFF_DN_QKV // hd, OFF_DN_GATE // hd, OFF_SMALL // LANES
    seq_buf = pltpu.VMEM((seq, hd), jnp.float32)
    return pl.pallas_call(
        _dn_kernel,
        grid=(batch, DN_HEADS),
        in_specs=[
            pl.BlockSpec((seq, hd), lambda b, h: (b, qb + h)),
            pl.BlockSpec((seq, hd), lambda b, h: (b, qb + DN_HEADS + h)),
            pl.BlockSpec((seq, hd), lambda b, h: (b, qb + 2 * DN_HEADS + h)),
            pl.BlockSpec((seq, hd), lambda b, h: (b, gb + h)),
            pl.BlockSpec((seq, LANES), lambda b, h: (b, sb)),
            pl.BlockSpec((3, hd), lambda b, h: (0, h)),
            pl.BlockSpec((3, hd), lambda b, h: (0, DN_HEADS + h)),
            pl.BlockSpec((3, hd), lambda b, h: (0, 2 * DN_HEADS + h)),
            pl.BlockSpec((1, LANES), lambda b, h: (0, 0)),
            pl.BlockSpec((1, LANES), lambda b, h: (0, 0)),
            pl.BlockSpec((1, hd), lambda b, h: (0, 0)),
        ],
        out_specs=pl.BlockSpec((seq, hd), lambda b, h: (b, h)),
        out_shape=jax.ShapeDtypeStruct((batch * seq, DN_WIDTH), jnp.float32),
        scratch_shapes=[
            pltpu.VMEM((seq + 2 * HALO, hd), jnp.float32),
            seq_buf, seq_buf, seq_buf,
            seq_buf, seq_buf, seq_buf, seq_buf,
            pltpu.VMEM((nchunk, LANES, DN_CHUNK), jnp.float32),
            pltpu.VMEM((nchunk, LANES, DN_CHUNK), jnp.float32),
            seq_buf, seq_buf,
            pltpu.VMEM((2, hd, hd), jnp.float32),
            pltpu.VMEM((2, seq, hd), jnp.float32),
            pltpu.VMEM((2, seq, hd), jnp.bfloat16),
            pltpu.VMEM((2, seq, DN_CHUNK), jnp.bfloat16),
            pltpu.VMEM((2, seq, hd), jnp.bfloat16),
            pltpu.VMEM((2, nchunk, hd, DN_CHUNK), jnp.bfloat16),
            pltpu.VMEM((2, nchunk, SUBLANES, LANES), jnp.float32),
        ],
        compiler_params=_cparams(("arbitrary", "arbitrary")),
        name="deltanet",
    )(proj, proj, proj, proj, proj, conv_qkv, conv_qkv, conv_qkv, alog_row, dtb_row, norm_dn_row)


def _swa_kernel(q_ref, k_ref, v_ref, pos_ref, invf_ref, ng_ref, o_ref,
                cos_ref, s1_ref, s2_ref, qr_ref, kr_ref, m_ref, a_ref):
    seq = o_ref.shape[0]
    hp = pl.program_id(1)
    pat = pl.program_id(2)
    nrow = seq // ROW_CHUNK
    lane = lax.broadcasted_iota(jnp.int32, (ROW_CHUNK, LANES), 1)
    dim = lane & (SWA_HEAD_DIM - 1)
    half = ROPE_DIM // 2

    @pl.when((hp == 0) & (pat == 0))
    def _():
        def body(i, c):
            rows = pl.ds(pl.multiple_of(i * ROW_CHUNK, ROW_CHUNK), ROW_CHUNK)
            ang = pos_ref[rows, :].astype(jnp.float32) * invf_ref[...]
            cs, sn = jnp.cos(ang), jnp.sin(ang)
            cos_ref[rows, :] = cs
            s1_ref[rows, :] = jnp.where(dim < half, -sn, 0.0)
            s2_ref[rows, :] = jnp.where((dim >= half) & (dim < ROPE_DIM), sn, 0.0)
            return c
        lax.fori_loop(0, nrow, body, 0)

    @pl.when(pat == 0)
    def _():
        m_ref[...] = jnp.full(m_ref.shape, NEG_INF, jnp.float32)
        a_ref[...] = jnp.zeros_like(a_ref)

    def rope(i, c):
        rows = pl.ds(pl.multiple_of(i * ROW_CHUNK, ROW_CHUNK), ROW_CHUNK)
        cs, s1, s2 = cos_ref[rows, :], s1_ref[rows, :], s2_ref[rows, :]
        for src, dst, scale in ((q_ref, qr_ref, SWA_HEAD_DIM ** -0.5), (k_ref, kr_ref, None)):
            x = src[rows, :]
            y = x * cs + pltpu.roll(x, LANES - half, 1) * s1 + pltpu.roll(x, half, 1) * s2
            dst[rows, :] = y if scale is None else y * scale
        return c
    lax.fori_loop(0, nrow, rope, 0)

    def run_pattern(dil):
        sub_len = seq // dil
        nblk = sub_len // SWA_QB
        win = min(sub_len, SWA_QB + 2 * SWA_RADIUS)
        lane_q = lax.broadcasted_iota(jnp.int32, (SWA_QB, LANES), 1)
        head_lo = lane_q < SWA_HEAD_DIM
        lane_w = lax.broadcasted_iota(jnp.int32, (win, LANES), 1)
        own_v = (lane_w < SWA_HEAD_DIM, lane_w >= SWA_HEAD_DIM)
        dij = (lax.broadcasted_iota(jnp.int32, (SWA_QB, win), 0)
               - lax.broadcasted_iota(jnp.int32, (SWA_QB, win), 1))

        def sl(start, size):
            return pl.ds(start, size) if dil == 1 else pl.ds(start, size, stride=dil)

        def body(i, c):
            blocks = []
            for sub in range(SWA_UNROLL):
                it = i * SWA_UNROLL + sub
                res = it // nblk
                qs = (it % nblk) * SWA_QB
                ws = jnp.clip(qs - SWA_RADIUS, 0, sub_len - win)
                rq = sl(res + qs * dil, SWA_QB)
                rk = sl(res + ws * dil, win)
                valid = jnp.abs(dij + (qs - ws)) <= SWA_RADIUS
                blocks.append((rq, qr_ref[rq, :], _bf(kr_ref[rk, :]), v_ref[rk, :], valid))
            heads = [(b, hh) for b in range(SWA_UNROLL) for hh in (0, 1)]
            own = (head_lo, jnp.logical_not(head_lo))
            s_all = [jnp.where(blocks[b][4], _dot_nt(_bf(jnp.where(own[hh], blocks[b][1], 0.0)), blocks[b][2]),
                               NEG_INF) for b, hh in heads]
            v_ext = [_bf(jnp.where(own_v[hh], blocks[b][3], 1.0)) for b, hh in heads]
            m_old = [m_ref[hh, blocks[b][0], :] for b, hh in heads]
            m_new = [jnp.maximum(m_o, jnp.max(s, axis=-1, keepdims=True)) for m_o, s in zip(m_old, s_all)]
            p_all = [jnp.exp(s - jnp.concatenate([m_n] * (win // LANES), axis=1)) for s, m_n in zip(s_all, m_new)]
            pv_all = [_dot(_bf(p), v) for p, v in zip(p_all, v_ext)]
            for (b, hh), m_o, m_n, pv in zip(heads, m_old, m_new, pv_all):
                rq = blocks[b][0]
                a_ref[hh, rq, :] = jnp.exp(m_o - m_n) * a_ref[hh, rq, :] + pv
                m_ref[hh, rq, :] = m_n
            return c
        lax.fori_loop(0, dil * nblk // SWA_UNROLL, body, 0)

    for p_idx, (_, dil) in enumerate(SWA_PATTERNS):
        pl.when(pat == p_idx)(functools.partial(run_pattern, dil))

    @pl.when(pat == SWA_N_PAT - 1)
    def _():
        lo = lane < SWA_HEAD_DIM

        def fin(i, c):
            rows = pl.ds(pl.multiple_of(i * ROW_CHUNK, ROW_CHUNK), ROW_CHUNK)
            a0, a1 = a_ref[0, rows, :], a_ref[1, rows, :]
            o = jnp.where(lo, a0 / pltpu.roll(a0, SWA_HEAD_DIM, 1), a1 / pltpu.roll(a1, SWA_HEAD_DIM, 1))
            sq = o * o
            s_lo = jnp.sum(jnp.where(lo, sq, 0.0), axis=-1, keepdims=True)
            s_hi = jnp.sum(jnp.where(lo, 0.0, sq), axis=-1, keepdims=True)
            ms = jnp.where(lo, s_lo, s_hi) * (1.0 / SWA_HEAD_DIM)
            o_ref[rows, :] = o * lax.rsqrt(ms + EPS) * ng_ref[...]
            return c
        lax.fori_loop(0, nrow, fin, 0)


def _swa(proj, pos_col, invf_row, norm_c_row, batch, seq):
    base = OFF_SWA // LANES
    per_pat = SWA_WIDTH // LANES
    qkv = SWA_QKV_WIDTH // LANES
    pairs = SWA_HEADS // 2
    seq_buf = pltpu.VMEM((seq, LANES), jnp.float32)

    def spec(which):
        return pl.BlockSpec((seq, LANES), lambda b, hp, p: (b, base + which * qkv + p * per_pat + hp))

    return pl.pallas_call(
        _swa_kernel,
        grid=(batch, pairs, SWA_N_PAT),
        in_specs=[
            spec(0), spec(1), spec(2),
            pl.BlockSpec((seq, 1), lambda b, hp, p: (b, 0)),
            pl.BlockSpec((1, LANES), lambda b, hp, p: (0, 0)),
            pl.BlockSpec((1, LANES), lambda b, hp, p: (0, hp)),
        ],
        out_specs=pl.BlockSpec((seq, LANES), lambda b, hp, p: (b, hp)),
        out_shape=jax.ShapeDtypeStruct((batch * seq, SWA_WIDTH), jnp.float32),
        scratch_shapes=[seq_buf] * 5 + [pltpu.VMEM((2, seq, LANES), jnp.float32)] * 2,
        compiler_params=_cparams(("arbitrary", "arbitrary", "arbitrary")),
        name="swa",
    )(proj, proj, proj, pos_col, invf_row, norm_c_row)


def _outproj_kernel(x_ref, ya_ref, yb_ref, yc_ref, w_ref, o_ref):
    a0, a1, a2 = CONV_CH, CONV_CH + DN_WIDTH, MIX_WIDTH
    acc = _dot(_bf(ya_ref[...]), w_ref[0:a0, :])
    acc = acc + _dot(_bf(yb_ref[...]), w_ref[a0:a1, :])
    acc = acc + _dot(_bf(yc_ref[...]), w_ref[a1:a2, :])
    o_ref[...] = x_ref[...] + acc


def _out_proj(x2d, y_a, y_b, y_c, w_bf):
    n = x2d.shape[0]
    tm = OUT_TM
    return pl.pallas_call(
        _outproj_kernel,
        grid=(n // tm,),
        in_specs=[
            pl.BlockSpec((tm, D_MODEL), lambda i: (i, 0)),
            pl.BlockSpec((tm, CONV_CH), lambda i: (i, 0)),
            pl.BlockSpec((tm, DN_WIDTH), lambda i: (i, 0)),
            pl.BlockSpec((tm, SWA_WIDTH), lambda i: (i, 0)),
            pl.BlockSpec((MIX_WIDTH, D_MODEL), lambda i: (0, 0)),
        ],
        out_specs=pl.BlockSpec((tm, D_MODEL), lambda i: (i, 0)),
        out_shape=jax.ShapeDtypeStruct((n, D_MODEL), jnp.float32),
        compiler_params=_cparams(("arbitrary",)),
        name="out_proj",
    )(x2d, y_a, y_b, y_c, w_bf)


def _ffn_kernel(xp_ref, x_ref, xn_ref, g_ref, wg_ref, wv_ref, cg_ref, cv_ref, wd_ref, gf_ref, o_ref,
                h_ref, acc_ref, *, tiles_per_seq, final_norm):
    tm = x_ref.shape[0]
    t = pl.program_id(0) % tiles_per_seq

    def norm(x):
        return x * lax.rsqrt(jnp.mean(x * x, axis=-1, keepdims=True) + EPS) * g_ref[...]

    halo = HALO_BF16
    hp = norm(xp_ref[...])
    hn = norm(xn_ref[...])
    h_ref[0:halo, :] = _bf(jnp.where(t == 0, 0.0, hp))
    h_ref[halo + tm:halo + tm + halo, :] = _bf(jnp.where(t == tiles_per_seq - 1, 0.0, hn))

    def fill(i, c):
        rows = pl.ds(pl.multiple_of(i * ROW_CHUNK, ROW_CHUNK), ROW_CHUNK)
        h_ref[pl.ds(pl.multiple_of(i * ROW_CHUNK + halo, halo), ROW_CHUNK), :] = _bf(norm(x_ref[rows, :]))
        return c
    lax.fori_loop(0, tm // ROW_CHUNK, fill, 0)

    acc_ref[...] = jnp.zeros_like(acc_ref)

    def body(j, c):
        h = h_ref[...]
        gate = _conv3_val(_dot(h, wg_ref[j]), cg_ref[j], tm, halo)
        val = _conv3_val(_dot(h, wv_ref[j]), cv_ref[j], tm, halo)
        acc_ref[...] += _dot(_bf(_silu(gate) * val), wd_ref[j])
        return c
    lax.fori_loop(0, wg_ref.shape[0], body, 0)

    def fin(i, c):
        rows = pl.ds(pl.multiple_of(i * ROW_CHUNK, ROW_CHUNK), ROW_CHUNK)
        y = x_ref[rows, :] + acc_ref[rows, :]
        if final_norm:
            y = y * lax.rsqrt(jnp.mean(y * y, axis=-1, keepdims=True) + EPS) * gf_ref[...]
        o_ref[rows, :] = y
        return c
    lax.fori_loop(0, tm // ROW_CHUNK, fin, 0)


def _ffn(x2d, g_row, wg, wv, cg, cv, wd, gf_row, seq, final_norm):
    n = x2d.shape[0]
    tm = FFN_TM
    tiles_per_seq = seq // tm
    halo = HALO_BF16
    hb = tm // halo
    nhb = n // halo
    nff = wg.shape[0]
    whole = lambda shape: pl.BlockSpec(shape, lambda i: (0,) * len(shape), pipeline_mode=pl.Buffered(1))
    kern = functools.partial(_ffn_kernel, tiles_per_seq=tiles_per_seq, final_norm=final_norm)
    return pl.pallas_call(
        kern,
        grid=(n // tm,),
        in_specs=[
            pl.BlockSpec((halo, D_MODEL), lambda i: (jnp.maximum(i * hb - 1, 0), 0)),
            pl.BlockSpec((tm, D_MODEL), lambda i: (i, 0)),
            pl.BlockSpec((halo, D_MODEL), lambda i: (jnp.minimum((i + 1) * hb, nhb - 1), 0)),
            whole((1, D_MODEL)),
            whole((nff, D_MODEL, FF_CHUNK)),
            whole((nff, D_MODEL, FF_CHUNK)),
            whole((nff, 3, FF_CHUNK)),
            whole((nff, 3, FF_CHUNK)),
            whole((nff, FF_CHUNK, D_MODEL)),
            whole((1, D_MODEL)),
        ],
        out_specs=pl.BlockSpec((tm, D_MODEL), lambda i: (i, 0)),
        out_shape=jax.ShapeDtypeStruct((n, D_MODEL), jnp.float32),
        scratch_shapes=[
            pltpu.VMEM((tm + 2 * halo, D_MODEL), jnp.bfloat16),
            pltpu.VMEM((tm, D_MODEL), jnp.float32),
        ],
        compiler_params=_cparams(("arbitrary",)),
        name="ffn",
    )(x2d, x2d, x2d, g_row, wg, wv, cg, cv, wd, gf_row)


def _reorder_w_in(w):
    sizes = (CONV_CH,) * 3 + (DN_WIDTH,) * 4 + (DN_HEADS,) * 4 + (SWA_QKV_WIDTH,) * 3
    offs = np.concatenate([[0], np.cumsum(sizes)])
    seg = [w[:, int(offs[i]):int(offs[i + 1])] for i in range(len(sizes))]
    xa, gb, gc, q, k, v, gate, b_f, b_b, a_f, a_b, qc, kc, vc = seg
    pad = jnp.zeros((w.shape[0], IN_COLS - OFF_SMALL - 4 * DN_HEADS), w.dtype)
    out = jnp.concatenate([xa, gb, gc, q, k, v, gate, qc, kc, vc, b_f, b_b, a_f, a_b, pad], axis=1)
    return _bf(out)


def _small_row(f_vals, b_vals):
    row = jnp.zeros((1, LANES), jnp.float32)
    row = row.at[0, SMALL_AF:SMALL_AF + DN_HEADS].set(f_vals.astype(jnp.float32))
    return row.at[0, SMALL_AB:SMALL_AB + DN_HEADS].set(b_vals.astype(jnp.float32))


def _ff_chunks(w, axis):
    n = w.shape[axis] // FF_CHUNK
    if axis == 1:
        return w.reshape(w.shape[0], n, FF_CHUNK).transpose(1, 0, 2)
    return w.reshape(n, FF_CHUNK, w.shape[1])


def kernel(x, positions, norm_mix, w_in, conv_a, norm_a, conv_qkv, a_log_f, a_log_b, dt_bias_f, dt_bias_b,
           norm_dn, norm_c, w_o, norm_ffn, w_up, conv_ffn, w_down, norm_final):
    batch, seq, _ = x.shape
    depth = w_in.shape[0]
    x2d = x.reshape(batch * seq, D_MODEL)
    pos_col = positions.reshape(batch * seq, 1)

    half = ROPE_DIM // 2
    inv_freq = ROPE_THETA ** (-jnp.arange(half, dtype=jnp.float32) / half)
    d_idx = np.arange(LANES) % SWA_HEAD_DIM
    invf_row = jnp.where(jnp.asarray(d_idx < ROPE_DIM), inv_freq[d_idx % half], 0.0).reshape(1, LANES)
    grp = np.arange(CONV_CH) // (CONV_CH // CONV_GROUPS)
    group_mat = _bf(jnp.asarray(grp[:, None] == grp[None, :], jnp.float32))

    for l in range(depth):
        proj = _proj_in(x2d, norm_mix[l].reshape(1, D_MODEL), _reorder_w_in(w_in[l]), batch, seq)
        y_a = _mix_a(proj, conv_a[l], norm_a[l].reshape(1, CONV_CH), group_mat, batch, seq)
        y_b = _deltanet(proj, conv_qkv[l], _small_row(a_log_f[l], a_log_b[l]),
                        _small_row(dt_bias_f[l], dt_bias_b[l]), norm_dn[l].reshape(1, DN_HEAD_DIM), batch, seq)
        y_c = _swa(proj, pos_col, invf_row, norm_c[l].reshape(1, SWA_WIDTH), batch, seq)
        x2d = _out_proj(x2d, y_a, y_b, y_c, _bf(w_o[l]))
        wu = w_up[l]
        cf = conv_ffn[l]
        x2d = _ffn(
            x2d, norm_ffn[l].reshape(1, D_MODEL),
            _bf(_ff_chunks(wu[:, :D_FF], 1)), _bf(_ff_chunks(wu[:, D_FF:], 1)),
            _ff_chunks(cf[:, :D_FF], 1), _ff_chunks(cf[:, D_FF:], 1),
            _bf(_ff_chunks(w_down[l], 0)), norm_final.reshape(1, D_MODEL), seq, l == depth - 1)
    return x2d.reshape(batch, seq, D_MODEL)
```

```python
import functools

import numpy as np
import jax
import jax.numpy as jnp
from jax import lax
from jax.experimental import pallas as pl
from jax.experimental.pallas import tpu as pltpu

D_MODEL = 1024
EPS = 1e-6
NEG_INF = -1e30
CONV_CH = 256
CONV_GROUPS = 4
DN_HEADS = 4
DN_HEAD_DIM = 128
DN_WIDTH = DN_HEADS * DN_HEAD_DIM
SWA_HEAD_DIM = 64
SWA_HEADS = 4
SWA_PATTERNS = ((128, 1), (512, 4), (2048, 16))
SWA_N_PAT = 3
SWA_WIDTH = SWA_HEADS * SWA_HEAD_DIM
SWA_QKV_WIDTH = SWA_N_PAT * SWA_WIDTH
ROPE_THETA = 500000.0
ROPE_DIM = SWA_HEAD_DIM // 4
D_FF = 2816
MIX_WIDTH = CONV_CH + DN_WIDTH + SWA_WIDTH

LANES = 128
SUBLANES = 8
VMEM_LIMIT = 56 * 1024 * 1024

OFF_A = 0
OFF_DN_QKV = OFF_A + 3 * CONV_CH
OFF_DN_GATE = OFF_DN_QKV + 3 * DN_WIDTH
OFF_SWA = OFF_DN_GATE + DN_WIDTH
OFF_SMALL = OFF_SWA + 3 * SWA_QKV_WIDTH
IN_COLS = 5376
PROJ_CN = 768
SMALL_BF, SMALL_BB, SMALL_AF, SMALL_AB = 0, 4, 8, 12

DN_CHUNK = 256
DN_INTRA_UNROLL = 4
ROW_CHUNK = 256
HALO = SUBLANES
HALO_BF16 = 2 * SUBLANES
FF_CHUNK = 256
FFN_TM = 1024
OUT_TM = 1024
SWA_QB = 128
SWA_RADIUS = 64
SWA_UNROLL = 4
ROPE_ROWS = 512
SWA_ORDER = (2, 1, 0)


def _cparams(sem, vmem_limit=VMEM_LIMIT):
    return pltpu.CompilerParams(dimension_semantics=sem, vmem_limit_bytes=vmem_limit)


def _bf(x):
    return x.astype(jnp.bfloat16)


def _dot(a, b):
    return jnp.dot(a, b, preferred_element_type=jnp.float32)


def _dot_nt(a, b):
    return lax.dot_general(a, b, (((1,), (1,)), ((), ())), preferred_element_type=jnp.float32)


def _silu(x):
    return x * (1.0 / (1.0 + jnp.exp(-x)))


def _conv3_val(ext, w, rows, halo):
    n = ext.shape[0]
    prev = pltpu.roll(ext, 1, 0)[halo:halo + rows]
    nxt = pltpu.roll(ext, n - 1, 0)[halo:halo + rows]
    return prev * w[0:1] + ext[halo:halo + rows] * w[1:2] + nxt * w[2:3]


def _conv3_ref(ref, lead, start, rows, w):
    def tap(off):
        return ref[lead + (pl.ds(start + off, rows), slice(None))]
    return tap(-1) * w[0:1] + tap(0) * w[1:2] + tap(1) * w[2:3]


def _proj_kernel(x_ref, g_ref, w_ref, o_ref, h_ref):
    @pl.when(pl.program_id(1) == 0)
    def _():
        def body(i, c):
            rows = pl.ds(pl.multiple_of(i * ROW_CHUNK, ROW_CHUNK), ROW_CHUNK)
            x = x_ref[rows, :]
            y = x * lax.rsqrt(jnp.mean(x * x, axis=-1, keepdims=True) + EPS)
            h_ref[rows, :] = _bf(y * g_ref[...])
            return c
        lax.fori_loop(0, x_ref.shape[0] // ROW_CHUNK, body, 0)

    o_ref[...] = _dot(h_ref[...], w_ref[...])


def _proj_in(x2d, g_row, w_bf, batch, seq):
    ncol = IN_COLS // PROJ_CN
    return pl.pallas_call(
        _proj_kernel,
        grid=(batch, ncol),
        in_specs=[
            pl.BlockSpec((seq, D_MODEL), lambda b, j: (b, 0)),
            pl.BlockSpec((1, D_MODEL), lambda b, j: (0, 0)),
            pl.BlockSpec((D_MODEL, PROJ_CN), lambda b, j: (0, j)),
        ],
        out_specs=pl.BlockSpec((seq, PROJ_CN), lambda b, j: (b, j)),
        out_shape=jax.ShapeDtypeStruct((batch * seq, IN_COLS), jnp.float32),
        scratch_shapes=[pltpu.VMEM((seq, D_MODEL), jnp.bfloat16)],
        compiler_params=_cparams(("arbitrary", "arbitrary")),
        name="proj_in",
    )(x2d, g_row, w_bf)


def _mixa_kernel(p_ref, cw_ref, ng_ref, gm_ref, o_ref, u_ref):
    seq = o_ref.shape[0]
    nchunk = seq // ROW_CHUNK
    zeros = jnp.zeros((HALO, CONV_CH), jnp.float32)
    u_ref[0:HALO, :] = zeros
    u_ref[HALO + seq:HALO + seq + HALO, :] = zeros

    def fill(i, c):
        rows = pl.ds(pl.multiple_of(i * ROW_CHUNK, ROW_CHUNK), ROW_CHUNK)
        xa = p_ref[rows, 0:CONV_CH]
        gc = p_ref[rows, 2 * CONV_CH:3 * CONV_CH]
        u_ref[pl.ds(pl.multiple_of(i * ROW_CHUNK + HALO, HALO), ROW_CHUNK), :] = gc * xa
        return c
    lax.fori_loop(0, nchunk, fill, 0)

    for i in range(nchunk):
        start = i * ROW_CHUNK
        gb = p_ref[start:start + ROW_CHUNK, CONV_CH:2 * CONV_CH]
        y = gb * _conv3_ref(u_ref, (), start + HALO, ROW_CHUNK, cw_ref[...])
        sq = y * y
        hi = _bf(sq)
        lo = _bf(sq - hi.astype(jnp.float32))
        ms = (_dot(hi, gm_ref[...]) + _dot(lo, gm_ref[...])) * (1.0 / (CONV_CH // CONV_GROUPS))
        o_ref[start:start + ROW_CHUNK, :] = y * lax.rsqrt(ms + EPS) * ng_ref[...]


def _mix_a(proj, conv_a, norm_a_row, group_mat, batch, seq):
    return pl.pallas_call(
        _mixa_kernel,
        grid=(batch,),
        in_specs=[
            pl.BlockSpec((seq, 3 * CONV_CH), lambda b: (b, OFF_A // (3 * CONV_CH))),
            pl.BlockSpec((3, CONV_CH), lambda b: (0, 0)),
            pl.BlockSpec((1, CONV_CH), lambda b: (0, 0)),
            pl.BlockSpec((CONV_CH, CONV_CH), lambda b: (0, 0)),
        ],
        out_specs=pl.BlockSpec((seq, CONV_CH), lambda b: (b, 0)),
        out_shape=jax.ShapeDtypeStruct((batch * seq, CONV_CH), jnp.float32),
        scratch_shapes=[pltpu.VMEM((seq + 2 * HALO, CONV_CH), jnp.float32)],
        compiler_params=_cparams(("arbitrary",)),
        name="mix_a",
    )(proj, conv_a, norm_a_row, group_mat)


def _tri_inverse_group(a_list, lowers, xor_ij, eye):
    n = a_list[0].shape[0]
    d_list = [eye - jnp.where(xor_ij == 1, a, 0.0) for a in a_list]
    level = 1
    while (1 << level) < n:
        s = 1 << level
        mask = (xor_ij >> level) == 1
        e_list = [_bf(jnp.where(mask, a, 0.0)) for a in a_list]
        d_bf = [_bf(d) for d in d_list]
        if s < SUBLANES:
            de_list = [_dot(db, e) for db, e in zip(d_bf, e_list)]
            d_list = [d - _dot(_bf(de), db) for d, de, db in zip(d_list, de_list, d_bf)]
        else:
            blocks = [[d[b * s:(b + 1) * s] for b in range(n // s)] for d in d_list]
            picks = [range(1, n // s, 2) if lower else range(0, n // s, 2) for lower in lowers]
            sel = [_bf(jnp.concatenate([blk[b] for b in pk], axis=0)) for blk, pk in zip(blocks, picks)]
            de_list = [_dot(x, e) for x, e in zip(sel, e_list)]
            upd = [_dot(_bf(de), db) for de, db in zip(de_list, d_bf)]
            d_list = []
            for blk, pk, u in zip(blocks, picks, upd):
                for k, b in enumerate(pk):
                    blk[b] = blk[b] - u[k * s:(k + 1) * s]
                d_list.append(jnp.concatenate(blk, axis=0))
        level += 1
    return d_list


def _dn_intra_group(items, ii, jj, xor_ij, eye):
    c = items[0][0].shape[0]
    pre = []
    for q, k, v, beta, gcol, grow, lower in items:
        if lower:
            incl, strict, glast = ii >= jj, ii > jj, gcol[c - 1:c, :]
        else:
            incl, strict, glast = ii <= jj, ii < jj, gcol[0:1, :]
        diff = jnp.concatenate([gcol] * (c // LANES), axis=1) - grow
        dec = jnp.where(incl, jnp.exp(jnp.where(incl, diff, 0.0)), 0.0)
        kb = k * beta
        k_bf = _bf(k)
        a = jnp.where(strict, _dot_nt(_bf(kb), k_bf) * dec, 0.0)
        qk = _bf(_dot_nt(_bf(q), k_bf) * dec)
        pre.append((a, qk, kb, glast))
    tinv_list = _tri_inverse_group([p[0] for p in pre], [item[6] for item in items], xor_ij, eye)
    out = []
    for (q, k, v, beta, gcol, grow, lower), (a, qk, kb, glast), tinv in zip(items, pre, tinv_list):
        eg = jnp.exp(gcol)
        rhs = jnp.concatenate([v * beta, kb * eg], axis=1)
        sol = _dot(_bf(tinv), _bf(rhs))
        u, w = sol[:, :DN_HEAD_DIM], sol[:, DN_HEAD_DIM:]
        kd = k * jnp.exp(glast - gcol)
        out.append((u, _bf(w), qk, _bf(q * eg), _bf(kd.T), jnp.exp(glast)))
    return out


def _dn_kernel(q_ref, k_ref, v_ref, gate_ref, small_ref, cq_ref, ck_ref, cv_ref, alog_ref, dtb_ref,
               ng_ref, o_ref,
               pad_ref, qn_ref, kn_ref, vn_ref, bf_ref, bb_ref, gf_ref, gb_ref, rf_ref, rb_ref,
               of_ref, ob_ref, s_ref, u_ref, w_ref, qk_ref, qd_ref, kdt_ref, gl_ref):
    seq = o_ref.shape[0]
    head = pl.program_id(1)
    nrow = seq // ROW_CHUNK
    nchunk = seq // DN_CHUNK

    zeros = jnp.zeros((HALO, DN_HEAD_DIM), jnp.float32)
    pad_ref[0:HALO, :] = zeros
    pad_ref[HALO + seq:HALO + seq + HALO, :] = zeros

    def conv_silu(src_ref, cw_ref, dst_ref, mode):
        def fill(i, c):
            rows = pl.ds(pl.multiple_of(i * ROW_CHUNK, ROW_CHUNK), ROW_CHUNK)
            pad_ref[pl.ds(pl.multiple_of(i * ROW_CHUNK + HALO, HALO), ROW_CHUNK), :] = src_ref[rows, :]
            return c
        lax.fori_loop(0, nrow, fill, 0)

        for i in range(nrow):
            start = i * ROW_CHUNK
            y = _silu(_conv3_ref(pad_ref, (), start + HALO, ROW_CHUNK, cw_ref[...]))
            if mode != "v":
                y = y * lax.rsqrt(jnp.sum(y * y, axis=-1, keepdims=True) + EPS)
            if mode == "q":
                y = y * (DN_HEAD_DIM ** -0.5)
            dst_ref[start:start + ROW_CHUNK, :] = y

    conv_silu(q_ref, cq_ref, qn_ref, "q")
    conv_silu(k_ref, ck_ref, kn_ref, "k")
    conv_silu(v_ref, cv_ref, vn_ref, "v")

    lane = lax.broadcasted_iota(jnp.int32, (DN_CHUNK, LANES), 1)
    ii = lax.broadcasted_iota(jnp.int32, (DN_CHUNK, DN_CHUNK), 0)
    jj = lax.broadcasted_iota(jnp.int32, (DN_CHUNK, DN_CHUNK), 1)
    tri2 = jnp.concatenate([jnp.where(ii >= jj, 1.0, 0.0), jnp.where(ii <= jj, 1.0, 0.0)],
                           axis=0).astype(jnp.bfloat16)

    def pick(x, col):
        sel = jnp.sum(jnp.where(lane == col, x, 0.0), axis=-1, keepdims=True)
        return jnp.broadcast_to(sel, x.shape)

    neg_a = -jnp.exp(alog_ref[...])

    def gates(i, c):
        cis = [2 * i, 2 * i + 1]
        rows = [pl.ds(pl.multiple_of(ci * DN_CHUNK, DN_CHUNK), DN_CHUNK) for ci in cis]
        sm = [small_ref[r, :] for r in rows]
        z = [s + dtb_ref[...] for s in sm]
        g = [neg_a * (jnp.maximum(zz, 0.0) + jnp.log1p(jnp.exp(-jnp.abs(zz)))) for zz in z]
        g1 = [_bf(x) for x in g]
        r1 = [x - x1.astype(jnp.float32) for x, x1 in zip(g, g1)]
        g2 = [_bf(x) for x in r1]
        g3 = [_bf(x - x2.astype(jnp.float32)) for x, x2 in zip(r1, g2)]
        r = [_dot(tri2, jnp.concatenate(parts, axis=1)) for parts in zip(g1, g2, g3)]
        cf = [x[:DN_CHUNK, 2 * LANES:] + x[:DN_CHUNK, LANES:2 * LANES] + x[:DN_CHUNK, :LANES] for x in r]
        cb = [x[DN_CHUNK:, 2 * LANES:] + x[DN_CHUNK:, LANES:2 * LANES] + x[DN_CHUNK:, :LANES] for x in r]
        beta = [1.0 / (1.0 + jnp.exp(-s)) for s in sm]
        for k in range(2):
            bf_ref[rows[k], :] = pick(beta[k], SMALL_BF + head)
            bb_ref[rows[k], :] = pick(beta[k], SMALL_BB + head)
        for k in range(2):
            gf_ref[rows[k], :] = pick(cf[k], SMALL_AF + head)
            gb_ref[rows[k], :] = pick(cb[k], SMALL_AB + head)
        for k in range(2):
            rf_ref[cis[k]] = cf[k].T
            rb_ref[cis[k]] = cb[k].T
        return c
    lax.fori_loop(0, nchunk // 2, gates, 0)

    xor_ij = ii ^ jj
    eye = jnp.where(ii == jj, 1.0, 0.0)

    def intra(i, c):
        items, where = [], []
        for sub in range(DN_INTRA_UNROLL):
            ci = i * DN_INTRA_UNROLL + sub
            rows = pl.ds(pl.multiple_of(ci * DN_CHUNK, DN_CHUNK), DN_CHUNK)
            for direction in (0, 1):
                if direction == 0:
                    beta, gcol = bf_ref[rows, :], gf_ref[rows, :]
                    grow = rf_ref[ci, pl.ds(SMALL_AF + head, 1), :]
                else:
                    beta, gcol = bb_ref[rows, :], gb_ref[rows, :]
                    grow = rb_ref[ci, pl.ds(SMALL_AB + head, 1), :]
                items.append((qn_ref[rows, :], kn_ref[rows, :], vn_ref[rows, :], beta, gcol, grow, direction == 0))
                where.append((direction, ci, rows))
        for (direction, ci, rows), (u, w, qk, qd, kdt, gl) in zip(where, _dn_intra_group(items, ii, jj, xor_ij, eye)):
            u_ref[direction, rows, :] = u
            w_ref[direction, rows, :] = w
            qk_ref[direction, rows, :] = qk
            qd_ref[direction, rows, :] = qd
            kdt_ref[direction, ci] = kdt
            gl_ref[direction, ci] = jnp.broadcast_to(gl, (SUBLANES, LANES))
        return c
    lax.fori_loop(0, nchunk // DN_INTRA_UNROLL, intra, 0)

    s_ref[...] = jnp.zeros_like(s_ref)

    def scan(i, c):
        cis = (i, nchunk - 1 - i)
        rows = [pl.ds(pl.multiple_of(ci * DN_CHUNK, DN_CHUNK), DN_CHUNK) for ci in cis]
        s_old = [s_ref[d] for d in (0, 1)]
        s_bf = [_bf(s) for s in s_old]
        v_new = [_bf(u_ref[d, rows[d], :] - _dot(w_ref[d, rows[d], :], s_bf[d])) for d in (0, 1)]
        s_new = [s_old[d] * gl_ref[d, cis[d], 0:1, :] + _dot(kdt_ref[d, cis[d]], v_new[d]) for d in (0, 1)]
        o = [_dot(qd_ref[d, rows[d], :], s_bf[d]) + _dot(qk_ref[d, rows[d], :], v_new[d]) for d in (0, 1)]
        for d, out_ref in ((0, of_ref), (1, ob_ref)):
            s_ref[d] = s_new[d]
            out_ref[rows[d], :] = o[d]
        return c
    lax.fori_loop(0, nchunk, scan, 0)

    def fin(i, c):
        rows = pl.ds(pl.multiple_of(i * ROW_CHUNK, ROW_CHUNK), ROW_CHUNK)
        o = of_ref[rows, :] + ob_ref[rows, :]
        o = o * lax.rsqrt(jnp.mean(o * o, axis=-1, keepdims=True) + EPS) * ng_ref[...]
        o_ref[rows, :] = o * _silu(gate_ref[rows, :])
        return c
    lax.fori_loop(0, nrow, fin, 0)


def _deltanet(proj, conv_qkv, alog_row, dtb_row, norm_dn_row, batch, seq):
    nchunk = seq // DN_CHUNK
    hd = DN_HEAD_DIM
    qb, gb, sb = OFF_DN_QKV // hd, OFF_DN_GATE // hd, OFF_SMALL // LANES
    seq_buf = pltpu.VMEM((seq, hd), jnp.float32)
    return pl.pallas_call(
        _dn_kernel,
        grid=(batch, DN_HEADS),
        in_specs=[
            pl.BlockSpec((seq, hd), lambda b, h: (b, qb + h)),
            pl.BlockSpec((seq, hd), lambda b, h: (b, qb + DN_HEADS + h)),
            pl.BlockSpec((seq, hd), lambda b, h: (b, qb + 2 * DN_HEADS + h)),
            pl.BlockSpec((seq, hd), lambda b, h: (b, gb + h)),
            pl.BlockSpec((seq, LANES), lambda b, h: (b, sb)),
            pl.BlockSpec((3, hd), lambda b, h: (0, h)),
            pl.BlockSpec((3, hd), lambda b, h: (0, DN_HEADS + h)),
            pl.BlockSpec((3, hd), lambda b, h: (0, 2 * DN_HEADS + h)),
            pl.BlockSpec((1, LANES), lambda b, h: (0, 0)),
            pl.BlockSpec((1, LANES), lambda b, h: (0, 0)),
            pl.BlockSpec((1, hd), lambda b, h: (0, 0)),
        ],
        out_specs=pl.BlockSpec((seq, hd), lambda b, h: (b, h)),
        out_shape=jax.ShapeDtypeStruct((batch * seq, DN_WIDTH), jnp.float32),
        scratch_shapes=[
            pltpu.VMEM((seq + 2 * HALO, hd), jnp.float32),
            seq_buf, seq_buf, seq_buf,
            seq_buf, seq_buf, seq_buf, seq_buf,
            pltpu.VMEM((nchunk, LANES, DN_CHUNK), jnp.float32),
            pltpu.VMEM((nchunk, LANES, DN_CHUNK), jnp.float32),
            seq_buf, seq_buf,
            pltpu.VMEM((2, hd, hd), jnp.float32),
            pltpu.VMEM((2, seq, hd), jnp.float32),
            pltpu.VMEM((2, seq, hd), jnp.bfloat16),
            pltpu.VMEM((2, seq, DN_CHUNK), jnp.bfloat16),
            pltpu.VMEM((2, seq, hd), jnp.bfloat16),
            pltpu.VMEM((2, nchunk, hd, DN_CHUNK), jnp.bfloat16),
            pltpu.VMEM((2, nchunk, SUBLANES, LANES), jnp.float32),
        ],
        compiler_params=_cparams(("arbitrary", "arbitrary")),
        name="deltanet",
    )(proj, proj, proj, proj, proj, conv_qkv, conv_qkv, conv_qkv, alog_row, dtb_row, norm_dn_row)


def _swa_kernel(q_ref, k_ref, v_ref, pos_ref, invf_ref, rot_ref, ng_ref, o_ref,
                cos_ref, sin_ref, qr_ref, kr_ref, m_ref, a_ref):
    seq = o_ref.shape[0]
    hp = pl.program_id(1)
    step = pl.program_id(2)
    nrow = seq // ROW_CHUNK

    @pl.when((hp == 0) & (step == 0))
    def _():
        def body(i, c):
            rows = pl.ds(pl.multiple_of(i * ROW_CHUNK, ROW_CHUNK), ROW_CHUNK)
            ang = pos_ref[rows, :].astype(jnp.float32) * invf_ref[...]
            cos_ref[rows, :] = jnp.cos(ang)
            sin_ref[rows, :] = jnp.sin(ang)
            return c
        lax.fori_loop(0, nrow, body, 0)

    def rope(i, c):
        rows = pl.ds(pl.multiple_of(i * ROPE_ROWS, ROPE_ROWS), ROPE_ROWS)
        cs, sn = cos_ref[rows, :], sin_ref[rows, :]
        x = jnp.concatenate([q_ref[rows, :], k_ref[rows, :]], axis=1)
        hi = _bf(x)
        lo = _bf(x - hi.astype(jnp.float32))
        xr = _dot(hi, rot_ref[...]) + _dot(lo, rot_ref[...])
        y = x * jnp.concatenate([cs, cs], axis=1) + xr * jnp.concatenate([sn, sn], axis=1)
        qr_ref[rows, :] = y[:, :LANES] * (SWA_HEAD_DIM ** -0.5)
        kr_ref[rows, :] = y[:, LANES:]
        return c
    lax.fori_loop(0, seq // ROPE_ROWS, rope, 0)

    def run_pattern(dil, first):
        sub_len = seq // dil
        nblk = sub_len // SWA_QB
        win = min(sub_len, SWA_QB + 2 * SWA_RADIUS)
        lane_q = lax.broadcasted_iota(jnp.int32, (SWA_QB, LANES), 1)
        head_lo = lane_q < SWA_HEAD_DIM
        lane_w = lax.broadcasted_iota(jnp.int32, (win, LANES), 1)
        own_v = (lane_w < SWA_HEAD_DIM, lane_w >= SWA_HEAD_DIM)
        dij = (lax.broadcasted_iota(jnp.int32, (SWA_QB, win), 0)
               - lax.broadcasted_iota(jnp.int32, (SWA_QB, win), 1))

        def sl(start, size):
            return pl.ds(start, size) if dil == 1 else pl.ds(start, size, stride=dil)

        def body(i, c):
            blocks = []
            for sub in range(SWA_UNROLL):
                it = i * SWA_UNROLL + sub
                res = it // nblk
                qs = (it % nblk) * SWA_QB
                ws = jnp.clip(qs - SWA_RADIUS, 0, sub_len - win)
                rq = sl(res + qs * dil, SWA_QB)
                rk = sl(res + ws * dil, win)
                valid = jnp.abs(dij + (qs - ws)) <= SWA_RADIUS
                blocks.append((rq, qr_ref[rq, :], _bf(kr_ref[rk, :]), v_ref[rk, :], valid))
            heads = [(b, hh) for b in range(SWA_UNROLL) for hh in (0, 1)]
            own = (head_lo, jnp.logical_not(head_lo))
            s_all = [jnp.where(blocks[b][4], _dot_nt(_bf(jnp.where(own[hh], blocks[b][1], 0.0)), blocks[b][2]),
                               NEG_INF) for b, hh in heads]
            v_ext = [_bf(jnp.where(own_v[hh], blocks[b][3], 1.0)) for b, hh in heads]
            m_cur = [jnp.broadcast_to(jnp.max(s, axis=-1, keepdims=True), (SWA_QB, LANES)) for s in s_all]
            if first:
                m_new = m_cur
            else:
                m_old = [m_ref[hh, blocks[b][0], :] for b, hh in heads]
                m_new = [jnp.maximum(m_o, m_c) for m_o, m_c in zip(m_old, m_cur)]
            p_all = [jnp.exp(s - jnp.concatenate([m_n] * (win // LANES), axis=1)) for s, m_n in zip(s_all, m_new)]
            pv_all = [_dot(_bf(p), v) for p, v in zip(p_all, v_ext)]
            for idx, (b, hh) in enumerate(heads):
                rq = blocks[b][0]
                if first:
                    a_ref[hh, rq, :] = pv_all[idx]
                else:
                    a_ref[hh, rq, :] = jnp.exp(m_old[idx] - m_new[idx]) * a_ref[hh, rq, :] + pv_all[idx]
                m_ref[hh, rq, :] = m_new[idx]
            return c
        lax.fori_loop(0, dil * nblk // SWA_UNROLL, body, 0)

    for s_idx, p_idx in enumerate(SWA_ORDER):
        pl.when(step == s_idx)(functools.partial(run_pattern, SWA_PATTERNS[p_idx][1], s_idx == 0))

    @pl.when(step == SWA_N_PAT - 1)
    def _():
        lo = lax.broadcasted_iota(jnp.int32, (ROW_CHUNK, LANES), 1) < SWA_HEAD_DIM

        def fin(i, c):
            rows = pl.ds(pl.multiple_of(i * ROW_CHUNK, ROW_CHUNK), ROW_CHUNK)
            a0, a1 = a_ref[0, rows, :], a_ref[1, rows, :]
            o = jnp.where(lo, a0 / pltpu.roll(a0, SWA_HEAD_DIM, 1), a1 / pltpu.roll(a1, SWA_HEAD_DIM, 1))
            sq = o * o
            s_lo = jnp.sum(jnp.where(lo, sq, 0.0), axis=-1, keepdims=True)
            s_hi = jnp.sum(jnp.where(lo, 0.0, sq), axis=-1, keepdims=True)
            ms = jnp.where(lo, s_lo, s_hi) * (1.0 / SWA_HEAD_DIM)
            o_ref[rows, :] = o * lax.rsqrt(ms + EPS) * ng_ref[...]
            return c
        lax.fori_loop(0, nrow, fin, 0)


def _swa(proj, pos_col, invf_row, rot_mat, norm_c_row, batch, seq):
    base = OFF_SWA // LANES
    per_pat = SWA_WIDTH // LANES
    qkv = SWA_QKV_WIDTH // LANES
    pairs = SWA_HEADS // 2
    seq_buf = pltpu.VMEM((seq, LANES), jnp.float32)
    first, stride = SWA_ORDER[0], SWA_ORDER[1] - SWA_ORDER[0]
    assert tuple(first + p * stride for p in range(SWA_N_PAT)) == SWA_ORDER

    def spec(which):
        return pl.BlockSpec((seq, LANES),
                            lambda b, hp, p: (b, base + which * qkv + (first + p * stride) * per_pat + hp))

    return pl.pallas_call(
        _swa_kernel,
        grid=(batch, pairs, SWA_N_PAT),
        in_specs=[
            spec(0), spec(1), spec(2),
            pl.BlockSpec((seq, 1), lambda b, hp, p: (b, 0)),
            pl.BlockSpec((1, LANES), lambda b, hp, p: (0, 0)),
            pl.BlockSpec((2 * LANES, 2 * LANES), lambda b, hp, p: (0, 0)),
            pl.BlockSpec((1, LANES), lambda b, hp, p: (0, hp)),
        ],
        out_specs=pl.BlockSpec((seq, LANES), lambda b, hp, p: (b, hp)),
        out_shape=jax.ShapeDtypeStruct((batch * seq, SWA_WIDTH), jnp.float32),
        scratch_shapes=[seq_buf] * 4 + [pltpu.VMEM((2, seq, LANES), jnp.float32)] * 2,
        compiler_params=_cparams(("arbitrary", "arbitrary", "arbitrary")),
        name="swa",
    )(proj, proj, proj, pos_col, invf_row, rot_mat, norm_c_row)


def _outproj_kernel(x_ref, ya_ref, yb_ref, yc_ref, w_ref, o_ref):
    a0, a1, a2 = CONV_CH, CONV_CH + DN_WIDTH, MIX_WIDTH
    acc = _dot(_bf(ya_ref[...]), w_ref[0:a0, :])
    acc = acc + _dot(_bf(yb_ref[...]), w_ref[a0:a1, :])
    acc = acc + _dot(_bf(yc_ref[...]), w_ref[a1:a2, :])
    o_ref[...] = x_ref[...] + acc


def _out_proj(x2d, y_a, y_b, y_c, w_bf):
    n = x2d.shape[0]
    tm = OUT_TM
    return pl.pallas_call(
        _outproj_kernel,
        grid=(n // tm,),
        in_specs=[
            pl.BlockSpec((tm, D_MODEL), lambda i: (i, 0)),
            pl.BlockSpec((tm, CONV_CH), lambda i: (i, 0)),
            pl.BlockSpec((tm, DN_WIDTH), lambda i: (i, 0)),
            pl.BlockSpec((tm, SWA_WIDTH), lambda i: (i, 0)),
            pl.BlockSpec((MIX_WIDTH, D_MODEL), lambda i: (0, 0)),
        ],
        out_specs=pl.BlockSpec((tm, D_MODEL), lambda i: (i, 0)),
        out_shape=jax.ShapeDtypeStruct((n, D_MODEL), jnp.float32),
        compiler_params=_cparams(("arbitrary",)),
        name="out_proj",
    )(x2d, y_a, y_b, y_c, w_bf)


def _ffn_kernel(xp_ref, x_ref, xn_ref, g_ref, wu_ref, cf_ref, wd_ref, gf_ref, o_ref,
                h_ref, a_ref, *, tiles_per_seq, final_norm):
    tm = x_ref.shape[0]
    t = pl.program_id(0) % tiles_per_seq

    def norm(x):
        return x * lax.rsqrt(jnp.mean(x * x, axis=-1, keepdims=True) + EPS) * g_ref[...]

    halo = HALO_BF16
    hp = norm(xp_ref[...])
    hn = norm(xn_ref[...])
    h_ref[0:halo, :] = _bf(jnp.where(t == 0, 0.0, hp))
    h_ref[halo + tm:halo + tm + halo, :] = _bf(jnp.where(t == tiles_per_seq - 1, 0.0, hn))

    def fill(i, c):
        rows = pl.ds(pl.multiple_of(i * ROW_CHUNK, ROW_CHUNK), ROW_CHUNK)
        h_ref[pl.ds(pl.multiple_of(i * ROW_CHUNK + halo, halo), ROW_CHUNK), :] = _bf(norm(x_ref[rows, :]))
        return c
    lax.fori_loop(0, tm // ROW_CHUNK, fill, 0)

    for j in range(D_FF // FF_CHUNK):
        h = h_ref[...]
        gcols = slice(j * FF_CHUNK, (j + 1) * FF_CHUNK)
        vcols = slice(D_FF + j * FF_CHUNK, D_FF + (j + 1) * FF_CHUNK)
        gate = _conv3_val(_dot(h, wu_ref[:, gcols]), cf_ref[:, gcols], tm, halo)
        val = _conv3_val(_dot(h, wu_ref[:, vcols]), cf_ref[:, vcols], tm, halo)
        a_ref[:, gcols] = _bf(_silu(gate) * val)
    o_ref[...] = _dot(a_ref[...], wd_ref[...])

    def fin(i, c):
        rows = pl.ds(pl.multiple_of(i * ROW_CHUNK, ROW_CHUNK), ROW_CHUNK)
        y = x_ref[rows, :] + o_ref[rows, :]
        if final_norm:
            y = y * lax.rsqrt(jnp.mean(y * y, axis=-1, keepdims=True) + EPS) * gf_ref[...]
        o_ref[rows, :] = y
        return c
    lax.fori_loop(0, tm // ROW_CHUNK, fin, 0)


def _ffn(x2d, g_row, wu, cf, wd, gf_row, seq, final_norm):
    n = x2d.shape[0]
    tm = FFN_TM
    tiles_per_seq = seq // tm
    halo = HALO_BF16
    hb = tm // halo
    nhb = n // halo
    whole = lambda shape: pl.BlockSpec(shape, lambda i: (0,) * len(shape), pipeline_mode=pl.Buffered(1))
    kern = functools.partial(_ffn_kernel, tiles_per_seq=tiles_per_seq, final_norm=final_norm)
    return pl.pallas_call(
        kern,
        grid=(n // tm,),
        in_specs=[
            pl.BlockSpec((halo, D_MODEL), lambda i: (jnp.maximum(i * hb - 1, 0), 0)),
            pl.BlockSpec((tm, D_MODEL), lambda i: (i, 0)),
            pl.BlockSpec((halo, D_MODEL), lambda i: (jnp.minimum((i + 1) * hb, nhb - 1), 0)),
            whole((1, D_MODEL)),
            whole((D_MODEL, 2 * D_FF)),
            whole((3, 2 * D_FF)),
            whole((D_FF, D_MODEL)),
            whole((1, D_MODEL)),
        ],
        out_specs=pl.BlockSpec((tm, D_MODEL), lambda i: (i, 0)),
        out_shape=jax.ShapeDtypeStruct((n, D_MODEL), jnp.float32),
        scratch_shapes=[
            pltpu.VMEM((tm + 2 * halo, D_MODEL), jnp.bfloat16),
            pltpu.VMEM((tm, D_FF), jnp.bfloat16),
        ],
        compiler_params=_cparams(("arbitrary",)),
        name="ffn",
    )(x2d, x2d, x2d, g_row, wu, cf, wd, gf_row)


def _reorder_w_in(w):
    sizes = (CONV_CH,) * 3 + (DN_WIDTH,) * 4 + (DN_HEADS,) * 4 + (SWA_QKV_WIDTH,) * 3
    offs = np.concatenate([[0], np.cumsum(sizes)])
    seg = [w[:, int(offs[i]):int(offs[i + 1])] for i in range(len(sizes))]
    xa, gb, gc, q, k, v, gate, b_f, b_b, a_f, a_b, qc, kc, vc = seg
    pad = jnp.zeros((w.shape[0], IN_COLS - OFF_SMALL - 4 * DN_HEADS), w.dtype)
    out = jnp.concatenate([xa, gb, gc, q, k, v, gate, qc, kc, vc, b_f, b_b, a_f, a_b, pad], axis=1)
    return _bf(out)


def _small_row(f_vals, b_vals):
    row = jnp.zeros((1, LANES), jnp.float32)
    row = row.at[0, SMALL_AF:SMALL_AF + DN_HEADS].set(f_vals.astype(jnp.float32))
    return row.at[0, SMALL_AB:SMALL_AB + DN_HEADS].set(b_vals.astype(jnp.float32))


def kernel(x, positions, norm_mix, w_in, conv_a, norm_a, conv_qkv, a_log_f, a_log_b, dt_bias_f, dt_bias_b,
           norm_dn, norm_c, w_o, norm_ffn, w_up, conv_ffn, w_down, norm_final):
    batch, seq, _ = x.shape
    depth = w_in.shape[0]
    x2d = x.reshape(batch * seq, D_MODEL)
    pos_col = positions.reshape(batch * seq, 1)

    half = ROPE_DIM // 2
    inv_freq = ROPE_THETA ** (-jnp.arange(half, dtype=jnp.float32) / half)
    d_idx = np.arange(LANES) % SWA_HEAD_DIM
    invf_row = jnp.where(jnp.asarray(d_idx < ROPE_DIM), inv_freq[d_idx % half], 0.0).reshape(1, LANES)
    rot_np = np.zeros((2 * LANES, 2 * LANES), np.float32)
    for lane_idx in range(2 * LANES):
        if d_idx[lane_idx % LANES] < half:
            rot_np[lane_idx + half, lane_idx] = -1.0
        elif d_idx[lane_idx % LANES] < ROPE_DIM:
            rot_np[lane_idx - half, lane_idx] = 1.0
    rot_mat = jnp.asarray(rot_np, jnp.bfloat16)
    grp = np.arange(CONV_CH) // (CONV_CH // CONV_GROUPS)
    group_mat = _bf(jnp.asarray(grp[:, None] == grp[None, :], jnp.float32))

    for l in range(depth):
        proj = _proj_in(x2d, norm_mix[l].reshape(1, D_MODEL), _reorder_w_in(w_in[l]), batch, seq)
        y_a = _mix_a(proj, conv_a[l], norm_a[l].reshape(1, CONV_CH), group_mat, batch, seq)
        y_b = _deltanet(proj, conv_qkv[l], _small_row(a_log_f[l], a_log_b[l]),
                        _small_row(dt_bias_f[l], dt_bias_b[l]), norm_dn[l].reshape(1, DN_HEAD_DIM), batch, seq)
        y_c = _swa(proj, pos_col, invf_row, rot_mat, norm_c[l].reshape(1, SWA_WIDTH), batch, seq)
        x2d = _out_proj(x2d, y_a, y_b, y_c, _bf(w_o[l]))
        x2d = _ffn(x2d, norm_ffn[l].reshape(1, D_MODEL), _bf(w_up[l]), conv_ffn[l], _bf(w_down[l]),
                   norm_final.reshape(1, D_MODEL), seq, l == depth - 1)
    return x2d.reshape(batch, seq, D_MODEL)
```

```python
import functools

import numpy as np
import jax
import jax.numpy as jnp
from jax import lax
from jax.experimental import pallas as pl
from jax.experimental.pallas import tpu as pltpu

D_MODEL = 1024
EPS = 1e-6
NEG_INF = -1e30
CONV_CH = 256
CONV_GROUPS = 4
DN_HEADS = 4
DN_HEAD_DIM = 128
DN_WIDTH = DN_HEADS * DN_HEAD_DIM
SWA_HEAD_DIM = 64
SWA_HEADS = 4
SWA_PATTERNS = ((128, 1), (512, 4), (2048, 16))
SWA_N_PAT = 3
SWA_WIDTH = SWA_HEADS * SWA_HEAD_DIM
SWA_QKV_WIDTH = SWA_N_PAT * SWA_WIDTH
ROPE_THETA = 500000.0
ROPE_DIM = SWA_HEAD_DIM // 4
D_FF = 2816
MIX_WIDTH = CONV_CH + DN_WIDTH + SWA_WIDTH

LANES = 128
SUBLANES = 8
VMEM_LIMIT = 56 * 1024 * 1024

OFF_A = 0
OFF_DN_QKV = OFF_A + 3 * CONV_CH
OFF_DN_GATE = OFF_DN_QKV + 3 * DN_WIDTH
OFF_SWA = OFF_DN_GATE + DN_WIDTH
OFF_SMALL = OFF_SWA + 3 * SWA_QKV_WIDTH
IN_COLS = 5376
PROJ_CN = 768
SMALL_BF, SMALL_BB, SMALL_AF, SMALL_AB = 0, 4, 8, 12

DN_CHUNK = 256
DN_INTRA_UNROLL = 4
ROW_CHUNK = 256
HALO = SUBLANES
HALO_BF16 = 2 * SUBLANES
FF_CHUNK = 256
FFN_TM = 1024
OUT_TM = 1024
SWA_QB = 128
SWA_RADIUS = 64
SWA_UNROLL = 8
ROPE_ROWS = 512
SWA_Q_SCALE = SWA_HEAD_DIM ** -0.5 * float(np.log2(np.e))
SWA_ORDER = (2, 1, 0)


def _cparams(sem, vmem_limit=VMEM_LIMIT):
    return pltpu.CompilerParams(dimension_semantics=sem, vmem_limit_bytes=vmem_limit)


def _bf(x):
    return x.astype(jnp.bfloat16)


def _dot(a, b):
    return jnp.dot(a, b, preferred_element_type=jnp.float32)


def _dot_nt(a, b):
    return lax.dot_general(a, b, (((1,), (1,)), ((), ())), preferred_element_type=jnp.float32)


def _silu(x):
    return x * (1.0 / (1.0 + jnp.exp(-x)))


def _conv3_val(ext, w, rows, halo):
    n = ext.shape[0]
    prev = pltpu.roll(ext, 1, 0)[halo:halo + rows]
    nxt = pltpu.roll(ext, n - 1, 0)[halo:halo + rows]
    return prev * w[0:1] + ext[halo:halo + rows] * w[1:2] + nxt * w[2:3]


def _conv3_ref(ref, lead, start, rows, w):
    def tap(off):
        return ref[lead + (pl.ds(start + off, rows), slice(None))]
    return tap(-1) * w[0:1] + tap(0) * w[1:2] + tap(1) * w[2:3]


def _proj_kernel(x_ref, g_ref, w_ref, o_ref, h_ref):
    @pl.when(pl.program_id(1) == 0)
    def _():
        def body(i, c):
            rows = pl.ds(pl.multiple_of(i * ROW_CHUNK, ROW_CHUNK), ROW_CHUNK)
            x = x_ref[rows, :]
            y = x * lax.rsqrt(jnp.mean(x * x, axis=-1, keepdims=True) + EPS)
            h_ref[rows, :] = _bf(y * g_ref[...])
            return c
        lax.fori_loop(0, x_ref.shape[0] // ROW_CHUNK, body, 0)

    o_ref[...] = _dot(h_ref[...], w_ref[...])


def _proj_in(x2d, g_row, w_bf, layer, batch, seq):
    ncol = IN_COLS // PROJ_CN
    return pl.pallas_call(
        _proj_kernel,
        grid=(batch, ncol),
        in_specs=[
            pl.BlockSpec((seq, D_MODEL), lambda b, j: (b, 0)),
            pl.BlockSpec((None, 1, D_MODEL), lambda b, j: (layer, 0, 0)),
            pl.BlockSpec((None, D_MODEL, PROJ_CN), lambda b, j: (layer, 0, j)),
        ],
        out_specs=pl.BlockSpec((seq, PROJ_CN), lambda b, j: (b, j)),
        out_shape=jax.ShapeDtypeStruct((batch * seq, IN_COLS), jnp.float32),
        scratch_shapes=[pltpu.VMEM((seq, D_MODEL), jnp.bfloat16)],
        compiler_params=_cparams(("arbitrary", "arbitrary")),
        name="proj_in",
    )(x2d, g_row, w_bf)


def _mixa_kernel(p_ref, cw_ref, ng_ref, gm_ref, o_ref, u_ref):
    seq = o_ref.shape[0]
    nchunk = seq // ROW_CHUNK
    zeros = jnp.zeros((HALO, CONV_CH), jnp.float32)
    u_ref[0:HALO, :] = zeros
    u_ref[HALO + seq:HALO + seq + HALO, :] = zeros

    def fill(i, c):
        rows = pl.ds(pl.multiple_of(i * ROW_CHUNK, ROW_CHUNK), ROW_CHUNK)
        xa = p_ref[rows, 0:CONV_CH]
        gc = p_ref[rows, 2 * CONV_CH:3 * CONV_CH]
        u_ref[pl.ds(pl.multiple_of(i * ROW_CHUNK + HALO, HALO), ROW_CHUNK), :] = gc * xa
        return c
    lax.fori_loop(0, nchunk, fill, 0)

    for i in range(nchunk):
        start = i * ROW_CHUNK
        gb = p_ref[start:start + ROW_CHUNK, CONV_CH:2 * CONV_CH]
        y = gb * _conv3_ref(u_ref, (), start + HALO, ROW_CHUNK, cw_ref[...])
        sq = y * y
        hi = _bf(sq)
        lo = _bf(sq - hi.astype(jnp.float32))
        ms = (_dot(hi, gm_ref[...]) + _dot(lo, gm_ref[...])) * (1.0 / (CONV_CH // CONV_GROUPS))
        o_ref[start:start + ROW_CHUNK, :] = y * lax.rsqrt(ms + EPS) * ng_ref[...]


def _mix_a(proj, conv_a, norm_a_row, group_mat, layer, batch, seq):
    return pl.pallas_call(
        _mixa_kernel,
        grid=(batch,),
        in_specs=[
            pl.BlockSpec((seq, 3 * CONV_CH), lambda b: (b, OFF_A // (3 * CONV_CH))),
            pl.BlockSpec((None, 3, CONV_CH), lambda b: (layer, 0, 0)),
            pl.BlockSpec((None, 1, CONV_CH), lambda b: (layer, 0, 0)),
            pl.BlockSpec((CONV_CH, CONV_CH), lambda b: (0, 0)),
        ],
        out_specs=pl.BlockSpec((seq, CONV_CH), lambda b: (b, 0)),
        out_shape=jax.ShapeDtypeStruct((batch * seq, CONV_CH), jnp.float32),
        scratch_shapes=[pltpu.VMEM((seq + 2 * HALO, CONV_CH), jnp.float32)],
        compiler_params=_cparams(("arbitrary",)),
        name="mix_a",
    )(proj, conv_a, norm_a_row, group_mat)


def _tri_inverse_group(a_list, lowers, xor_ij, eye):
    n = a_list[0].shape[0]
    d_list = [eye - jnp.where(xor_ij == 1, a, 0.0) for a in a_list]
    level = 1
    while (1 << level) < n:
        s = 1 << level
        mask = (xor_ij >> level) == 1
        e_list = [_bf(jnp.where(mask, a, 0.0)) for a in a_list]
        d_bf = [_bf(d) for d in d_list]
        if s < SUBLANES:
            de_list = [_dot(db, e) for db, e in zip(d_bf, e_list)]
            d_list = [d - _dot(_bf(de), db) for d, de, db in zip(d_list, de_list, d_bf)]
        else:
            blocks = [[d[b * s:(b + 1) * s] for b in range(n // s)] for d in d_list]
            picks = [range(1, n // s, 2) if lower else range(0, n // s, 2) for lower in lowers]
            sel = [_bf(jnp.concatenate([blk[b] for b in pk], axis=0)) for blk, pk in zip(blocks, picks)]
            de_list = [_dot(x, e) for x, e in zip(sel, e_list)]
            upd = [_dot(_bf(de), db) for de, db in zip(de_list, d_bf)]
            d_list = []
            for blk, pk, u in zip(blocks, picks, upd):
                for k, b in enumerate(pk):
                    blk[b] = blk[b] - u[k * s:(k + 1) * s]
                d_list.append(jnp.concatenate(blk, axis=0))
        level += 1
    return d_list


def _dn_intra_group(items, ii, jj, xor_ij, eye):
    c = items[0][0].shape[0]
    pre = []
    for q, k, v, beta, gcol, grow, lower in items:
        if lower:
            incl, strict, glast = ii >= jj, ii > jj, gcol[c - 1:c, :]
        else:
            incl, strict, glast = ii <= jj, ii < jj, gcol[0:1, :]
        diff = jnp.concatenate([gcol] * (c // LANES), axis=1) - grow
        dec = jnp.where(incl, jnp.exp(jnp.where(incl, diff, 0.0)), 0.0)
        kb = k * beta
        k_bf = _bf(k)
        a = jnp.where(strict, _dot_nt(_bf(kb), k_bf) * dec, 0.0)
        qk = _bf(_dot_nt(_bf(q), k_bf) * dec)
        pre.append((a, qk, kb, glast))
    tinv_list = _tri_inverse_group([p[0] for p in pre], [item[6] for item in items], xor_ij, eye)
    out = []
    for (q, k, v, beta, gcol, grow, lower), (a, qk, kb, glast), tinv in zip(items, pre, tinv_list):
        eg = jnp.exp(gcol)
        rhs = jnp.concatenate([v * beta, kb * eg], axis=1)
        sol = _dot(_bf(tinv), _bf(rhs))
        u, w = sol[:, :DN_HEAD_DIM], sol[:, DN_HEAD_DIM:]
        kd = k * jnp.exp(glast - gcol)
        out.append((u, _bf(w), qk, _bf(q * eg), _bf(kd.T), jnp.exp(glast)))
    return out


def _dn_kernel(q_ref, k_ref, v_ref, gate_ref, small_ref, cq_ref, ck_ref, cv_ref, alog_ref, dtb_ref,
               ng_ref, o_ref,
               pad_ref, qn_ref, kn_ref, vn_ref, bf_ref, bb_ref, gf_ref, gb_ref, rf_ref, rb_ref,
               of_ref, ob_ref, s_ref, u_ref, w_ref, qk_ref, qd_ref, kdt_ref, gl_ref, beta_ref, cf_ref, cb_ref):
    seq = o_ref.shape[0]
    head = pl.program_id(1)
    nrow = seq // ROW_CHUNK
    nchunk = seq // DN_CHUNK

    zeros = jnp.zeros((HALO, DN_HEAD_DIM), jnp.float32)
    pad_ref[0:HALO, :] = zeros
    pad_ref[HALO + seq:HALO + seq + HALO, :] = zeros

    def conv_silu(src_ref, cw_ref, dst_ref, mode):
        def fill(i, c):
            rows = pl.ds(pl.multiple_of(i * ROW_CHUNK, ROW_CHUNK), ROW_CHUNK)
            pad_ref[pl.ds(pl.multiple_of(i * ROW_CHUNK + HALO, HALO), ROW_CHUNK), :] = src_ref[rows, :]
            return c
        lax.fori_loop(0, nrow, fill, 0)

        for i in range(nrow):
            start = i * ROW_CHUNK
            y = _silu(_conv3_ref(pad_ref, (), start + HALO, ROW_CHUNK, cw_ref[...]))
            if mode != "v":
                y = y * lax.rsqrt(jnp.sum(y * y, axis=-1, keepdims=True) + EPS)
            if mode == "q":
                y = y * (DN_HEAD_DIM ** -0.5)
            dst_ref[start:start + ROW_CHUNK, :] = y

    conv_silu(q_ref, cq_ref, qn_ref, "q")
    conv_silu(k_ref, ck_ref, kn_ref, "k")
    conv_silu(v_ref, cv_ref, vn_ref, "v")

    lane = lax.broadcasted_iota(jnp.int32, (DN_CHUNK, LANES), 1)
    ii = lax.broadcasted_iota(jnp.int32, (DN_CHUNK, DN_CHUNK), 0)
    jj = lax.broadcasted_iota(jnp.int32, (DN_CHUNK, DN_CHUNK), 1)
    tri2 = jnp.concatenate([jnp.where(ii >= jj, 1.0, 0.0), jnp.where(ii <= jj, 1.0, 0.0)],
                           axis=0).astype(jnp.bfloat16)

    def pick(x, col):
        sel = jnp.sum(jnp.where(lane == col, x, 0.0), axis=-1, keepdims=True)
        return jnp.broadcast_to(sel, x.shape)

    @pl.when(head == 0)
    def _():
        neg_a = -jnp.exp(alog_ref[...])

        def gates(i, c):
            cis = [2 * i, 2 * i + 1]
            rows = [pl.ds(pl.multiple_of(ci * DN_CHUNK, DN_CHUNK), DN_CHUNK) for ci in cis]
            sm = [small_ref[r, :] for r in rows]
            z = [s + dtb_ref[...] for s in sm]
            g = [neg_a * (jnp.maximum(zz, 0.0) + jnp.log1p(jnp.exp(-jnp.abs(zz)))) for zz in z]
            g1 = [_bf(x) for x in g]
            r1 = [x - x1.astype(jnp.float32) for x, x1 in zip(g, g1)]
            g2 = [_bf(x) for x in r1]
            g3 = [_bf(x - x2.astype(jnp.float32)) for x, x2 in zip(r1, g2)]
            r = [_dot(tri2, jnp.concatenate(parts, axis=1)) for parts in zip(g1, g2, g3)]
            cf = [x[:DN_CHUNK, 2 * LANES:] + x[:DN_CHUNK, LANES:2 * LANES] + x[:DN_CHUNK, :LANES] for x in r]
            cb = [x[DN_CHUNK:, 2 * LANES:] + x[DN_CHUNK:, LANES:2 * LANES] + x[DN_CHUNK:, :LANES] for x in r]
            for k in range(2):
                beta_ref[rows[k], :] = 1.0 / (1.0 + jnp.exp(-sm[k]))
                cf_ref[rows[k], :] = cf[k]
                cb_ref[rows[k], :] = cb[k]
                rf_ref[cis[k]] = cf[k].T
                rb_ref[cis[k]] = cb[k].T
            return c
        lax.fori_loop(0, nchunk // 2, gates, 0)

    def picks(ci, c):
        rows = pl.ds(pl.multiple_of(ci * DN_CHUNK, DN_CHUNK), DN_CHUNK)
        beta = beta_ref[rows, :]
        bf_ref[rows, :] = pick(beta, SMALL_BF + head)
        bb_ref[rows, :] = pick(beta, SMALL_BB + head)
        gf_ref[rows, :] = pick(cf_ref[rows, :], SMALL_AF + head)
        gb_ref[rows, :] = pick(cb_ref[rows, :], SMALL_AB + head)
        return c
    lax.fori_loop(0, nchunk, picks, 0)

    xor_ij = ii ^ jj
    eye = jnp.where(ii == jj, 1.0, 0.0)

    def intra(i, c):
        items, where = [], []
        for sub in range(DN_INTRA_UNROLL):
            ci = i * DN_INTRA_UNROLL + sub
            rows = pl.ds(pl.multiple_of(ci * DN_CHUNK, DN_CHUNK), DN_CHUNK)
            for direction in (0, 1):
                if direction == 0:
                    beta, gcol = bf_ref[rows, :], gf_ref[rows, :]
                    grow = rf_ref[ci, pl.ds(SMALL_AF + head, 1), :]
                else:
                    beta, gcol = bb_ref[rows, :], gb_ref[rows, :]
                    grow = rb_ref[ci, pl.ds(SMALL_AB + head, 1), :]
                items.append((qn_ref[rows, :], kn_ref[rows, :], vn_ref[rows, :], beta, gcol, grow, direction == 0))
                where.append((direction, ci, rows))
        for (direction, ci, rows), (u, w, qk, qd, kdt, gl) in zip(where, _dn_intra_group(items, ii, jj, xor_ij, eye)):
            u_ref[direction, rows, :] = u
            w_ref[direction, rows, :] = w
            qk_ref[direction, rows, :] = qk
            qd_ref[direction, rows, :] = qd
            kdt_ref[direction, ci] = kdt
            gl_ref[direction, ci] = jnp.broadcast_to(gl, (SUBLANES, LANES))
        return c
    lax.fori_loop(0, nchunk // DN_INTRA_UNROLL, intra, 0)

    s_ref[...] = jnp.zeros_like(s_ref)

    def scan(i, c):
        cis = (i, nchunk - 1 - i)
        rows = [pl.ds(pl.multiple_of(ci * DN_CHUNK, DN_CHUNK), DN_CHUNK) for ci in cis]
        s_old = [s_ref[d] for d in (0, 1)]
        s_bf = [_bf(s) for s in s_old]
        v_new = [_bf(u_ref[d, rows[d], :] - _dot(w_ref[d, rows[d], :], s_bf[d])) for d in (0, 1)]
        s_new = [s_old[d] * gl_ref[d, cis[d], 0:1, :] + _dot(kdt_ref[d, cis[d]], v_new[d]) for d in (0, 1)]
        o = [_dot(qd_ref[d, rows[d], :], s_bf[d]) + _dot(qk_ref[d, rows[d], :], v_new[d]) for d in (0, 1)]
        for d, out_ref in ((0, of_ref), (1, ob_ref)):
            s_ref[d] = s_new[d]
            out_ref[rows[d], :] = o[d]
        return c
    lax.fori_loop(0, nchunk, scan, 0)

    def fin(i, c):
        rows = pl.ds(pl.multiple_of(i * ROW_CHUNK, ROW_CHUNK), ROW_CHUNK)
        o = of_ref[rows, :] + ob_ref[rows, :]
        o = o * lax.rsqrt(jnp.mean(o * o, axis=-1, keepdims=True) + EPS) * ng_ref[...]
        o_ref[rows, :] = o * _silu(gate_ref[rows, :])
        return c
    lax.fori_loop(0, nrow, fin, 0)


def _deltanet(proj, conv_qkv, alog_row, dtb_row, norm_dn_row, layer, batch, seq):
    nchunk = seq // DN_CHUNK
    hd = DN_HEAD_DIM
    qb, gb, sb = OFF_DN_QKV // hd, OFF_DN_GATE // hd, OFF_SMALL // LANES
    seq_buf = pltpu.VMEM((seq, hd), jnp.float32)
    return pl.pallas_call(
        _dn_kernel,
        grid=(batch, DN_HEADS),
        in_specs=[
            pl.BlockSpec((seq, hd), lambda b, h: (b, qb + h)),
            pl.BlockSpec((seq, hd), lambda b, h: (b, qb + DN_HEADS + h)),
            pl.BlockSpec((seq, hd), lambda b, h: (b, qb + 2 * DN_HEADS + h)),
            pl.BlockSpec((seq, hd), lambda b, h: (b, gb + h)),
            pl.BlockSpec((seq, LANES), lambda b, h: (b, sb)),
            pl.BlockSpec((None, 3, hd), lambda b, h: (layer, 0, h)),
            pl.BlockSpec((None, 3, hd), lambda b, h: (layer, 0, DN_HEADS + h)),
            pl.BlockSpec((None, 3, hd), lambda b, h: (layer, 0, 2 * DN_HEADS + h)),
            pl.BlockSpec((None, 1, LANES), lambda b, h: (layer, 0, 0)),
            pl.BlockSpec((None, 1, LANES), lambda b, h: (layer, 0, 0)),
            pl.BlockSpec((None, 1, hd), lambda b, h: (layer, 0, 0)),
        ],
        out_specs=pl.BlockSpec((seq, hd), lambda b, h: (b, h)),
        out_shape=jax.ShapeDtypeStruct((batch * seq, DN_WIDTH), jnp.float32),
        scratch_shapes=[
            pltpu.VMEM((seq + 2 * HALO, hd), jnp.float32),
            seq_buf, seq_buf, seq_buf,
            seq_buf, seq_buf, seq_buf, seq_buf,
            pltpu.VMEM((nchunk, LANES, DN_CHUNK), jnp.float32),
            pltpu.VMEM((nchunk, LANES, DN_CHUNK), jnp.float32),
            seq_buf, seq_buf,
            pltpu.VMEM((2, hd, hd), jnp.float32),
            pltpu.VMEM((2, seq, hd), jnp.float32),
            pltpu.VMEM((2, seq, hd), jnp.bfloat16),
            pltpu.VMEM((2, seq, DN_CHUNK), jnp.bfloat16),
            pltpu.VMEM((2, seq, hd), jnp.bfloat16),
            pltpu.VMEM((2, nchunk, hd, DN_CHUNK), jnp.bfloat16),
            pltpu.VMEM((2, nchunk, SUBLANES, LANES), jnp.float32),
            seq_buf, seq_buf, seq_buf,
        ],
        compiler_params=_cparams(("arbitrary", "arbitrary")),
        name="deltanet",
    )(proj, proj, proj, proj, proj, conv_qkv, conv_qkv, conv_qkv, alog_row, dtb_row, norm_dn_row)


def _swa_kernel(q_ref, k_ref, v_ref, pos_ref, invf_ref, rot_ref, ng_ref, o_ref,
                cos_ref, sin_ref, qr_ref, kr_ref, m_ref, a_ref, bias_ref):
    seq = o_ref.shape[0]
    hp = pl.program_id(1)
    step = pl.program_id(2)
    nrow = seq // ROW_CHUNK

    @pl.when((hp == 0) & (step == 0))
    def _():
        def body(i, c):
            rows = pl.ds(pl.multiple_of(i * ROW_CHUNK, ROW_CHUNK), ROW_CHUNK)
            ang = pos_ref[rows, :].astype(jnp.float32) * invf_ref[...]
            cos_ref[rows, :] = jnp.cos(ang)
            sin_ref[rows, :] = jnp.sin(ang)
            return c
        lax.fori_loop(0, nrow, body, 0)

    def rope(i, c):
        rows = pl.ds(pl.multiple_of(i * ROPE_ROWS, ROPE_ROWS), ROPE_ROWS)
        cs, sn = cos_ref[rows, :], sin_ref[rows, :]
        x = jnp.concatenate([q_ref[rows, :], k_ref[rows, :]], axis=1)
        hi = _bf(x)
        lo = _bf(x - hi.astype(jnp.float32))
        xr = _dot(hi, rot_ref[...]) + _dot(lo, rot_ref[...])
        y = x * jnp.concatenate([cs, cs], axis=1) + xr * jnp.concatenate([sn, sn], axis=1)
        qr_ref[rows, :] = y[:, :LANES] * SWA_Q_SCALE
        kr_ref[rows, :] = y[:, LANES:]
        return c
    lax.fori_loop(0, seq // ROPE_ROWS, rope, 0)

    def run_pattern(dil, first):
        sub_len = seq // dil
        nblk = sub_len // SWA_QB
        win = min(sub_len, SWA_QB + 2 * SWA_RADIUS)
        lane_q = lax.broadcasted_iota(jnp.int32, (SWA_QB, LANES), 1)
        head_lo = lane_q < SWA_HEAD_DIM
        lane_w = lax.broadcasted_iota(jnp.int32, (win, LANES), 1)
        own_v = (lane_w < SWA_HEAD_DIM, lane_w >= SWA_HEAD_DIM)
        dij = (lax.broadcasted_iota(jnp.int32, (SWA_QB, win), 0)
               - lax.broadcasted_iota(jnp.int32, (SWA_QB, win), 1))
        for k in range(3 if sub_len > win else 1):
            bias_ref[k, :, 0:win] = jnp.where(jnp.abs(dij + k * SWA_RADIUS) <= SWA_RADIUS, 0.0, NEG_INF)

        def sl(start, size):
            return pl.ds(start, size) if dil == 1 else pl.ds(start, size, stride=dil)

        def body(i, c):
            blocks = []
            for sub in range(SWA_UNROLL):
                it = i * SWA_UNROLL + sub
                res = it // nblk
                qs = (it % nblk) * SWA_QB
                ws = jnp.clip(qs - SWA_RADIUS, 0, sub_len - win)
                rq = sl(res + qs * dil, SWA_QB)
                rk = sl(res + ws * dil, win)
                bias = bias_ref[(qs - ws) // SWA_RADIUS, :, 0:win]
                blocks.append((rq, qr_ref[rq, :], _bf(kr_ref[rk, :]), v_ref[rk, :], bias))
            heads = [(b, hh) for b in range(SWA_UNROLL) for hh in (0, 1)]
            own = (head_lo, jnp.logical_not(head_lo))
            s_all = [_dot_nt(_bf(jnp.where(own[hh], blocks[b][1], 0.0)), blocks[b][2]) + blocks[b][4]
                     for b, hh in heads]
            v_ext = [_bf(jnp.where(own_v[hh], blocks[b][3], 1.0)) for b, hh in heads]
            m_cur = [jnp.broadcast_to(jnp.max(s, axis=-1, keepdims=True), (SWA_QB, LANES)) for s in s_all]
            if first:
                m_new = m_cur
            else:
                m_old = [m_ref[hh, blocks[b][0], :] for b, hh in heads]
                m_new = [jnp.maximum(m_o, m_c) for m_o, m_c in zip(m_old, m_cur)]
            p_all = [jnp.exp2(s - jnp.concatenate([m_n] * (win // LANES), axis=1)) for s, m_n in zip(s_all, m_new)]
            pv_all = [_dot(_bf(p), v) for p, v in zip(p_all, v_ext)]
            for idx, (b, hh) in enumerate(heads):
                rq = blocks[b][0]
                if first:
                    a_ref[hh, rq, :] = pv_all[idx]
                else:
                    a_ref[hh, rq, :] = jnp.exp2(m_old[idx] - m_new[idx]) * a_ref[hh, rq, :] + pv_all[idx]
                m_ref[hh, rq, :] = m_new[idx]
            return c
        lax.fori_loop(0, dil * nblk // SWA_UNROLL, body, 0)

    for s_idx, p_idx in enumerate(SWA_ORDER):
        pl.when(step == s_idx)(functools.partial(run_pattern, SWA_PATTERNS[p_idx][1], s_idx == 0))

    @pl.when(step == SWA_N_PAT - 1)
    def _():
        lo = lax.broadcasted_iota(jnp.int32, (ROW_CHUNK, LANES), 1) < SWA_HEAD_DIM

        def fin(i, c):
            rows = pl.ds(pl.multiple_of(i * ROW_CHUNK, ROW_CHUNK), ROW_CHUNK)
            a0, a1 = a_ref[0, rows, :], a_ref[1, rows, :]
            o = jnp.where(lo, a0 / pltpu.roll(a0, SWA_HEAD_DIM, 1), a1 / pltpu.roll(a1, SWA_HEAD_DIM, 1))
            sq = o * o
            s_lo = jnp.sum(jnp.where(lo, sq, 0.0), axis=-1, keepdims=True)
            s_hi = jnp.sum(jnp.where(lo, 0.0, sq), axis=-1, keepdims=True)
            ms = jnp.where(lo, s_lo, s_hi) * (1.0 / SWA_HEAD_DIM)
            o_ref[rows, :] = o * lax.rsqrt(ms + EPS) * ng_ref[...]
            return c
        lax.fori_loop(0, nrow, fin, 0)


def _swa(proj, pos_col, invf_row, rot_mat, norm_c_row, layer, batch, seq):
    base = OFF_SWA // LANES
    per_pat = SWA_WIDTH // LANES
    qkv = SWA_QKV_WIDTH // LANES
    pairs = SWA_HEADS // 2
    seq_buf = pltpu.VMEM((seq, LANES), jnp.float32)
    first, stride = SWA_ORDER[0], SWA_ORDER[1] - SWA_ORDER[0]
    assert tuple(first + p * stride for p in range(SWA_N_PAT)) == SWA_ORDER

    def spec(which):
        return pl.BlockSpec((seq, LANES),
                            lambda b, hp, p: (b, base + which * qkv + (first + p * stride) * per_pat + hp))

    return pl.pallas_call(
        _swa_kernel,
        grid=(batch, pairs, SWA_N_PAT),
        in_specs=[
            spec(0), spec(1), spec(2),
            pl.BlockSpec((seq, 1), lambda b, hp, p: (b, 0)),
            pl.BlockSpec((1, LANES), lambda b, hp, p: (0, 0)),
            pl.BlockSpec((2 * LANES, 2 * LANES), lambda b, hp, p: (0, 0)),
            pl.BlockSpec((None, 1, LANES), lambda b, hp, p: (layer, 0, hp)),
        ],
        out_specs=pl.BlockSpec((seq, LANES), lambda b, hp, p: (b, hp)),
        out_shape=jax.ShapeDtypeStruct((batch * seq, SWA_WIDTH), jnp.float32),
        scratch_shapes=([seq_buf] * 4 + [pltpu.VMEM((2, seq, LANES), jnp.float32)] * 2
                        + [pltpu.VMEM((3, SWA_QB, SWA_QB + 2 * SWA_RADIUS), jnp.float32)]),
        compiler_params=_cparams(("arbitrary", "arbitrary", "arbitrary")),
        name="swa",
    )(proj, proj, proj, pos_col, invf_row, rot_mat, norm_c_row)


def _outproj_kernel(x_ref, ya_ref, yb_ref, yc_ref, w_ref, o_ref):
    a0, a1, a2 = CONV_CH, CONV_CH + DN_WIDTH, MIX_WIDTH
    acc = _dot(_bf(ya_ref[...]), w_ref[0:a0, :])
    acc = acc + _dot(_bf(yb_ref[...]), w_ref[a0:a1, :])
    acc = acc + _dot(_bf(yc_ref[...]), w_ref[a1:a2, :])
    o_ref[...] = x_ref[...] + acc


def _out_proj(x2d, y_a, y_b, y_c, w_bf, layer):
    n = x2d.shape[0]
    tm = OUT_TM
    return pl.pallas_call(
        _outproj_kernel,
        grid=(n // tm,),
        in_specs=[
            pl.BlockSpec((tm, D_MODEL), lambda i: (i, 0)),
            pl.BlockSpec((tm, CONV_CH), lambda i: (i, 0)),
            pl.BlockSpec((tm, DN_WIDTH), lambda i: (i, 0)),
            pl.BlockSpec((tm, SWA_WIDTH), lambda i: (i, 0)),
            pl.BlockSpec((None, MIX_WIDTH, D_MODEL), lambda i: (layer, 0, 0)),
        ],
        out_specs=pl.BlockSpec((tm, D_MODEL), lambda i: (i, 0)),
        out_shape=jax.ShapeDtypeStruct((n, D_MODEL), jnp.float32),
        compiler_params=_cparams(("arbitrary",)),
        name="out_proj",
    )(x2d, y_a, y_b, y_c, w_bf)


def _ffn_kernel(xp_ref, x_ref, xn_ref, g_ref, wu_ref, cf_ref, wd_ref, gf_ref, o_ref,
                h_ref, a_ref, *, tiles_per_seq, final_norm):
    tm = x_ref.shape[0]
    t = pl.program_id(0) % tiles_per_seq

    def norm(x):
        return x * lax.rsqrt(jnp.mean(x * x, axis=-1, keepdims=True) + EPS) * g_ref[...]

    halo = HALO_BF16
    hp = norm(xp_ref[...])
    hn = norm(xn_ref[...])
    h_ref[0:halo, :] = _bf(jnp.where(t == 0, 0.0, hp))
    h_ref[halo + tm:halo + tm + halo, :] = _bf(jnp.where(t == tiles_per_seq - 1, 0.0, hn))

    def fill(i, c):
        rows = pl.ds(pl.multiple_of(i * ROW_CHUNK, ROW_CHUNK), ROW_CHUNK)
        h_ref[pl.ds(pl.multiple_of(i * ROW_CHUNK + halo, halo), ROW_CHUNK), :] = _bf(norm(x_ref[rows, :]))
        return c
    lax.fori_loop(0, tm // ROW_CHUNK, fill, 0)

    for j in range(D_FF // FF_CHUNK):
        h = h_ref[...]
        gcols = slice(j * FF_CHUNK, (j + 1) * FF_CHUNK)
        vcols = slice(D_FF + j * FF_CHUNK, D_FF + (j + 1) * FF_CHUNK)
        gate = _conv3_val(_dot(h, wu_ref[:, gcols]), cf_ref[:, gcols], tm, halo)
        val = _conv3_val(_dot(h, wu_ref[:, vcols]), cf_ref[:, vcols], tm, halo)
        a_ref[:, gcols] = _bf(_silu(gate) * val)
    o_ref[...] = _dot(a_ref[...], wd_ref[...])

    def fin(i, c):
        rows = pl.ds(pl.multiple_of(i * ROW_CHUNK, ROW_CHUNK), ROW_CHUNK)
        y = x_ref[rows, :] + o_ref[rows, :]
        if final_norm:
            y = y * lax.rsqrt(jnp.mean(y * y, axis=-1, keepdims=True) + EPS) * gf_ref[...]
        o_ref[rows, :] = y
        return c
    lax.fori_loop(0, tm // ROW_CHUNK, fin, 0)


def _ffn(x2d, g_row, wu, cf, wd, gf_row, layer, seq, final_norm):
    n = x2d.shape[0]
    tm = FFN_TM
    tiles_per_seq = seq // tm
    halo = HALO_BF16
    hb = tm // halo
    nhb = n // halo
    whole = lambda shape: pl.BlockSpec((None,) + shape, lambda i: (layer,) + (0,) * len(shape),
                                       pipeline_mode=pl.Buffered(1))
    kern = functools.partial(_ffn_kernel, tiles_per_seq=tiles_per_seq, final_norm=final_norm)
    return pl.pallas_call(
        kern,
        grid=(n // tm,),
        in_specs=[
            pl.BlockSpec((halo, D_MODEL), lambda i: (jnp.maximum(i * hb - 1, 0), 0)),
            pl.BlockSpec((tm, D_MODEL), lambda i: (i, 0)),
            pl.BlockSpec((halo, D_MODEL), lambda i: (jnp.minimum((i + 1) * hb, nhb - 1), 0)),
            whole((1, D_MODEL)),
            whole((D_MODEL, 2 * D_FF)),
            whole((3, 2 * D_FF)),
            whole((D_FF, D_MODEL)),
            pl.BlockSpec((1, D_MODEL), lambda i: (0, 0)),
        ],
        out_specs=pl.BlockSpec((tm, D_MODEL), lambda i: (i, 0)),
        out_shape=jax.ShapeDtypeStruct((n, D_MODEL), jnp.float32),
        scratch_shapes=[
            pltpu.VMEM((tm + 2 * halo, D_MODEL), jnp.bfloat16),
            pltpu.VMEM((tm, D_FF), jnp.bfloat16),
        ],
        compiler_params=_cparams(("arbitrary",)),
        name="ffn",
    )(x2d, x2d, x2d, g_row, wu, cf, wd, gf_row)


def _reorder_w_in(w):
    n_small = 4 * DN_HEADS
    w = _bf(w)
    pad = jnp.zeros(w.shape[:-1] + (IN_COLS - OFF_SMALL - n_small,), w.dtype)
    return jnp.concatenate([w[..., :OFF_SWA], w[..., OFF_SWA + n_small:], w[..., OFF_SWA:OFF_SWA + n_small], pad],
                           axis=-1)


def _small_rows(f_vals, b_vals):
    row = jnp.concatenate([f_vals, b_vals], axis=-1).astype(jnp.float32)
    return jnp.pad(row, ((0, 0), (SMALL_AF, LANES - SMALL_AF - row.shape[-1])))[:, None, :]


def kernel(x, positions, norm_mix, w_in, conv_a, norm_a, conv_qkv, a_log_f, a_log_b, dt_bias_f, dt_bias_b,
           norm_dn, norm_c, w_o, norm_ffn, w_up, conv_ffn, w_down, norm_final):
    batch, seq, _ = x.shape
    depth = w_in.shape[0]
    x2d = x.reshape(batch * seq, D_MODEL)
    pos_col = positions.reshape(batch * seq, 1)

    half = ROPE_DIM // 2
    inv_freq = ROPE_THETA ** (-jnp.arange(half, dtype=jnp.float32) / half)
    d_idx = np.arange(LANES) % SWA_HEAD_DIM
    invf_row = jnp.where(jnp.asarray(d_idx < ROPE_DIM), inv_freq[d_idx % half], 0.0).reshape(1, LANES)
    rot_np = np.zeros((2 * LANES, 2 * LANES), np.float32)
    for lane_idx in range(2 * LANES):
        if d_idx[lane_idx % LANES] < half:
            rot_np[lane_idx + half, lane_idx] = -1.0
        elif d_idx[lane_idx % LANES] < ROPE_DIM:
            rot_np[lane_idx - half, lane_idx] = 1.0
    rot_mat = jnp.asarray(rot_np, jnp.bfloat16)
    grp = np.arange(CONV_CH) // (CONV_CH // CONV_GROUPS)
    group_mat = _bf(jnp.asarray(grp[:, None] == grp[None, :], jnp.float32))

    rows = lambda a: a[:, None, :]
    w_in_bf, w_o_bf, w_up_bf, w_down_bf = _reorder_w_in(w_in), _bf(w_o), _bf(w_up), _bf(w_down)
    alog_rows, dtb_rows = _small_rows(a_log_f, a_log_b), _small_rows(dt_bias_f, dt_bias_b)
    for l in range(depth):
        proj = _proj_in(x2d, rows(norm_mix), w_in_bf, l, batch, seq)
        y_a = _mix_a(proj, conv_a, rows(norm_a), group_mat, l, batch, seq)
        y_b = _deltanet(proj, conv_qkv, alog_rows, dtb_rows, rows(norm_dn), l, batch, seq)
        y_c = _swa(proj, pos_col, invf_row, rot_mat, rows(norm_c), l, batch, seq)
        x2d = _out_proj(x2d, y_a, y_b, y_c, w_o_bf, l)
        x2d = _ffn(x2d, rows(norm_ffn), w_up_bf, conv_ffn, w_down_bf, norm_final.reshape(1, D_MODEL), l, seq,
                   l == depth - 1)
    return x2d.reshape(batch, seq, D_MODEL)
```

```python
import functools

import numpy as np
import jax
import jax.numpy as jnp
from jax import lax
from jax.experimental import pallas as pl
from jax.experimental.pallas import tpu as pltpu

D_MODEL = 1024
EPS = 1e-6
NEG_INF = -1e30
CONV_CH = 256
CONV_GROUPS = 4
DN_HEADS = 4
DN_HEAD_DIM = 128
DN_WIDTH = DN_HEADS * DN_HEAD_DIM
SWA_HEAD_DIM = 64
SWA_HEADS = 4
SWA_PATTERNS = ((128, 1), (512, 4), (2048, 16))
SWA_N_PAT = 3
SWA_WIDTH = SWA_HEADS * SWA_HEAD_DIM
SWA_QKV_WIDTH = SWA_N_PAT * SWA_WIDTH
ROPE_THETA = 500000.0
ROPE_DIM = SWA_HEAD_DIM // 4
D_FF = 2816
MIX_WIDTH = CONV_CH + DN_WIDTH + SWA_WIDTH

MIX_DTYPE = jnp.bfloat16
LANES = 128
SUBLANES = 8
VMEM_LIMIT = 56 * 1024 * 1024

OFF_A = 0
OFF_DN_QKV = OFF_A + 3 * CONV_CH
OFF_DN_GATE = OFF_DN_QKV + 3 * DN_WIDTH
OFF_SWA = OFF_DN_GATE + DN_WIDTH
OFF_SMALL = OFF_SWA + 3 * SWA_QKV_WIDTH
IN_COLS = 5376
PROJ_CN = 768
SMALL_BF, SMALL_BB, SMALL_AF, SMALL_AB = 0, 4, 8, 12

DN_CHUNK = 256
DN_INTRA_UNROLL = 4
ROW_CHUNK = 256
HALO = SUBLANES
HALO_BF16 = 2 * SUBLANES
FF_CHUNK = 256
FFN_TM = 1024
OUT_TM = 1024
SWA_QB = 128
SWA_RADIUS = 64
SWA_UNROLL = 8
ROPE_ROWS = 512
SWA_Q_SCALE = SWA_HEAD_DIM ** -0.5 * float(np.log2(np.e))
SWA_ORDER = (2, 1, 0)


def _cparams(sem, vmem_limit=VMEM_LIMIT):
    return pltpu.CompilerParams(dimension_semantics=sem, vmem_limit_bytes=vmem_limit)


def _bf(x):
    return x.astype(jnp.bfloat16)


def _dot(a, b):
    return jnp.dot(a, b, preferred_element_type=jnp.float32)


def _dot_nt(a, b):
    return lax.dot_general(a, b, (((1,), (1,)), ((), ())), preferred_element_type=jnp.float32)


def _silu(x):
    return x * (1.0 / (1.0 + jnp.exp(-x)))


def _conv3_val(ext, w, rows, halo):
    n = ext.shape[0]
    prev = pltpu.roll(ext, 1, 0)[halo:halo + rows]
    nxt = pltpu.roll(ext, n - 1, 0)[halo:halo + rows]
    return prev * w[0:1] + ext[halo:halo + rows] * w[1:2] + nxt * w[2:3]


def _conv3_ref(ref, lead, start, rows, w):
    def tap(off):
        return ref[lead + (pl.ds(start + off, rows), slice(None))]
    return tap(-1) * w[0:1] + tap(0) * w[1:2] + tap(1) * w[2:3]


def _proj_kernel(x_ref, g_ref, w_ref, o_ref, h_ref):
    @pl.when(pl.program_id(1) == 0)
    def _():
        def body(i, c):
            rows = pl.ds(pl.multiple_of(i * ROW_CHUNK, ROW_CHUNK), ROW_CHUNK)
            x = x_ref[rows, :]
            y = x * lax.rsqrt(jnp.mean(x * x, axis=-1, keepdims=True) + EPS)
            h_ref[rows, :] = _bf(y * g_ref[...])
            return c
        lax.fori_loop(0, x_ref.shape[0] // ROW_CHUNK, body, 0)

    o_ref[...] = _dot(h_ref[...], w_ref[...])


def _proj_in(x2d, g_row, w_bf, layer, batch, seq):
    ncol = IN_COLS // PROJ_CN
    return pl.pallas_call(
        _proj_kernel,
        grid=(batch, ncol),
        in_specs=[
            pl.BlockSpec((seq, D_MODEL), lambda b, j: (b, 0)),
            pl.BlockSpec((None, 1, D_MODEL), lambda b, j: (layer, 0, 0)),
            pl.BlockSpec((None, D_MODEL, PROJ_CN), lambda b, j: (layer, 0, j)),
        ],
        out_specs=pl.BlockSpec((seq, PROJ_CN), lambda b, j: (b, j)),
        out_shape=jax.ShapeDtypeStruct((batch * seq, IN_COLS), jnp.float32),
        scratch_shapes=[pltpu.VMEM((seq, D_MODEL), jnp.bfloat16)],
        compiler_params=_cparams(("arbitrary", "arbitrary")),
        name="proj_in",
    )(x2d, g_row, w_bf)


def _mixa_kernel(p_ref, cw_ref, ng_ref, gm_ref, o_ref, u_ref):
    seq = o_ref.shape[0]
    nchunk = seq // ROW_CHUNK
    zeros = jnp.zeros((HALO, CONV_CH), jnp.float32)
    u_ref[0:HALO, :] = zeros
    u_ref[HALO + seq:HALO + seq + HALO, :] = zeros

    def fill(i, c):
        rows = pl.ds(pl.multiple_of(i * ROW_CHUNK, ROW_CHUNK), ROW_CHUNK)
        xa = p_ref[rows, 0:CONV_CH]
        gc = p_ref[rows, 2 * CONV_CH:3 * CONV_CH]
        u_ref[pl.ds(pl.multiple_of(i * ROW_CHUNK + HALO, HALO), ROW_CHUNK), :] = gc * xa
        return c
    lax.fori_loop(0, nchunk, fill, 0)

    for i in range(nchunk):
        start = i * ROW_CHUNK
        gb = p_ref[start:start + ROW_CHUNK, CONV_CH:2 * CONV_CH]
        y = gb * _conv3_ref(u_ref, (), start + HALO, ROW_CHUNK, cw_ref[...])
        sq = y * y
        hi = _bf(sq)
        lo = _bf(sq - hi.astype(jnp.float32))
        ms = (_dot(hi, gm_ref[...]) + _dot(lo, gm_ref[...])) * (1.0 / (CONV_CH // CONV_GROUPS))
        o_ref[start:start + ROW_CHUNK, :] = _bf(y * lax.rsqrt(ms + EPS) * ng_ref[...])


def _mix_a(proj, conv_a, norm_a_row, group_mat, layer, batch, seq):
    return pl.pallas_call(
        _mixa_kernel,
        grid=(batch,),
        in_specs=[
            pl.BlockSpec((seq, 3 * CONV_CH), lambda b: (b, OFF_A // (3 * CONV_CH))),
            pl.BlockSpec((None, 3, CONV_CH), lambda b: (layer, 0, 0)),
            pl.BlockSpec((None, 1, CONV_CH), lambda b: (layer, 0, 0)),
            pl.BlockSpec((CONV_CH, CONV_CH), lambda b: (0, 0)),
        ],
        out_specs=pl.BlockSpec((seq, CONV_CH), lambda b: (b, 0)),
        out_shape=jax.ShapeDtypeStruct((batch * seq, CONV_CH), MIX_DTYPE),
        scratch_shapes=[pltpu.VMEM((seq + 2 * HALO, CONV_CH), jnp.float32)],
        compiler_params=_cparams(("arbitrary",)),
        name="mix_a",
    )(proj, conv_a, norm_a_row, group_mat)


def _tri_inverse_group(a_list, lowers, xor_ij, eye):
    n = a_list[0].shape[0]
    d_list = [eye - jnp.where(xor_ij == 1, a, 0.0) for a in a_list]
    level = 1
    while (1 << level) < n:
        s = 1 << level
        mask = (xor_ij >> level) == 1
        e_list = [_bf(jnp.where(mask, a, 0.0)) for a in a_list]
        d_bf = [_bf(d) for d in d_list]
        if s < SUBLANES:
            de_list = [_dot(db, e) for db, e in zip(d_bf, e_list)]
            d_list = [d - _dot(_bf(de), db) for d, de, db in zip(d_list, de_list, d_bf)]
        else:
            blocks = [[d[b * s:(b + 1) * s] for b in range(n // s)] for d in d_list]
            picks = [range(1, n // s, 2) if lower else range(0, n // s, 2) for lower in lowers]
            sel = [_bf(jnp.concatenate([blk[b] for b in pk], axis=0)) for blk, pk in zip(blocks, picks)]
            de_list = [_dot(x, e) for x, e in zip(sel, e_list)]
            upd = [_dot(_bf(de), db) for de, db in zip(de_list, d_bf)]
            d_list = []
            for blk, pk, u in zip(blocks, picks, upd):
                for k, b in enumerate(pk):
                    blk[b] = blk[b] - u[k * s:(k + 1) * s]
                d_list.append(jnp.concatenate(blk, axis=0))
        level += 1
    return d_list


def _dn_intra_group(items, ii, jj, xor_ij, eye):
    c = items[0][0].shape[0]
    pre = []
    for q, k, v, beta, gcol, grow, lower in items:
        if lower:
            incl, strict, glast = ii >= jj, ii > jj, gcol[c - 1:c, :]
        else:
            incl, strict, glast = ii <= jj, ii < jj, gcol[0:1, :]
        diff = jnp.concatenate([gcol] * (c // LANES), axis=1) - grow
        dec = jnp.where(incl, jnp.exp(jnp.where(incl, diff, 0.0)), 0.0)
        kb = k * beta
        k_bf = _bf(k)
        a = jnp.where(strict, _dot_nt(_bf(kb), k_bf) * dec, 0.0)
        qk = _bf(_dot_nt(_bf(q), k_bf) * dec)
        pre.append((a, qk, kb, glast))
    lhs_list = [jnp.concatenate([_bf((item[1] * jnp.exp(p[3] - item[4])).T), p[1]], axis=0)
                for item, p in zip(items, pre)]
    tinv_list = _tri_inverse_group([p[0] for p in pre], [item[6] for item in items], xor_ij, eye)
    hd = DN_HEAD_DIM
    eg_list = [jnp.exp(item[4]) for item in items]
    sol_list = [_bf(_dot(_bf(tinv), _bf(jnp.concatenate([item[2] * item[3], p[2] * eg], axis=1))))
                for item, p, eg, tinv in zip(items, pre, eg_list, tinv_list)]
    r_list = [_dot(lhs, sol) for lhs, sol in zip(lhs_list, sol_list)]
    return [(r[:hd, :hd], _bf(r[:hd, hd:]), r[hd:, :hd], _bf(item[0] * eg - r[hd:, hd:]), jnp.exp(p[3]))
            for item, p, eg, r in zip(items, pre, eg_list, r_list)]


def _dn_kernel(q_ref, k_ref, v_ref, gate_ref, small_ref, cq_ref, ck_ref, cv_ref, alog_ref, dtb_ref,
               ng_ref, o_ref,
               pad_ref, qn_ref, kn_ref, vn_ref, bf_ref, bb_ref, gf_ref, gb_ref, rf_ref, rb_ref,
               of_ref, ob_ref, s_ref, b_ref, p_ref, qp_ref, gl_ref, beta_ref, cf_ref, cb_ref):
    seq = o_ref.shape[0]
    head = pl.program_id(1)
    nrow = seq // ROW_CHUNK
    nchunk = seq // DN_CHUNK

    zeros = jnp.zeros((HALO, DN_HEAD_DIM), jnp.float32)
    pad_ref[0:HALO, :] = zeros
    pad_ref[HALO + seq:HALO + seq + HALO, :] = zeros

    def conv_silu(src_ref, cw_ref, dst_ref, mode):
        def fill(i, c):
            rows = pl.ds(pl.multiple_of(i * ROW_CHUNK, ROW_CHUNK), ROW_CHUNK)
            pad_ref[pl.ds(pl.multiple_of(i * ROW_CHUNK + HALO, HALO), ROW_CHUNK), :] = src_ref[rows, :]
            return c
        lax.fori_loop(0, nrow, fill, 0)

        for i in range(nrow):
            start = i * ROW_CHUNK
            y = _silu(_conv3_ref(pad_ref, (), start + HALO, ROW_CHUNK, cw_ref[...]))
            if mode != "v":
                y = y * lax.rsqrt(jnp.sum(y * y, axis=-1, keepdims=True) + EPS)
            if mode == "q":
                y = y * (DN_HEAD_DIM ** -0.5)
            dst_ref[start:start + ROW_CHUNK, :] = y

    conv_silu(q_ref, cq_ref, qn_ref, "q")
    conv_silu(k_ref, ck_ref, kn_ref, "k")
    conv_silu(v_ref, cv_ref, vn_ref, "v")

    lane = lax.broadcasted_iota(jnp.int32, (DN_CHUNK, LANES), 1)
    ii = lax.broadcasted_iota(jnp.int32, (DN_CHUNK, DN_CHUNK), 0)
    jj = lax.broadcasted_iota(jnp.int32, (DN_CHUNK, DN_CHUNK), 1)
    tri2 = jnp.concatenate([jnp.where(ii >= jj, 1.0, 0.0), jnp.where(ii <= jj, 1.0, 0.0)],
                           axis=0).astype(jnp.bfloat16)

    def pick(x, col):
        sel = jnp.sum(jnp.where(lane == col, x, 0.0), axis=-1, keepdims=True)
        return jnp.broadcast_to(sel, x.shape)

    @pl.when(head == 0)
    def _():
        neg_a = -jnp.exp(alog_ref[...])

        def gates(i, c):
            cis = [2 * i, 2 * i + 1]
            rows = [pl.ds(pl.multiple_of(ci * DN_CHUNK, DN_CHUNK), DN_CHUNK) for ci in cis]
            sm = [small_ref[r, :] for r in rows]
            z = [s + dtb_ref[...] for s in sm]
            g = [neg_a * (jnp.maximum(zz, 0.0) + jnp.log1p(jnp.exp(-jnp.abs(zz)))) for zz in z]
            g1 = [_bf(x) for x in g]
            r1 = [x - x1.astype(jnp.float32) for x, x1 in zip(g, g1)]
            g2 = [_bf(x) for x in r1]
            g3 = [_bf(x - x2.astype(jnp.float32)) for x, x2 in zip(r1, g2)]
            r = [_dot(tri2, jnp.concatenate(parts, axis=1)) for parts in zip(g1, g2, g3)]
            cf = [x[:DN_CHUNK, 2 * LANES:] + x[:DN_CHUNK, LANES:2 * LANES] + x[:DN_CHUNK, :LANES] for x in r]
            cb = [x[DN_CHUNK:, 2 * LANES:] + x[DN_CHUNK:, LANES:2 * LANES] + x[DN_CHUNK:, :LANES] for x in r]
            for k in range(2):
                beta_ref[rows[k], :] = 1.0 / (1.0 + jnp.exp(-sm[k]))
                cf_ref[rows[k], :] = cf[k]
                cb_ref[rows[k], :] = cb[k]
                rf_ref[cis[k]] = cf[k].T
                rb_ref[cis[k]] = cb[k].T
            return c
        lax.fori_loop(0, nchunk // 2, gates, 0)

    def picks(ci, c):
        rows = pl.ds(pl.multiple_of(ci * DN_CHUNK, DN_CHUNK), DN_CHUNK)
        beta = beta_ref[rows, :]
        bf_ref[rows, :] = pick(beta, SMALL_BF + head)
        bb_ref[rows, :] = pick(beta, SMALL_BB + head)
        gf_ref[rows, :] = pick(cf_ref[rows, :], SMALL_AF + head)
        gb_ref[rows, :] = pick(cb_ref[rows, :], SMALL_AB + head)
        return c
    lax.fori_loop(0, nchunk, picks, 0)

    xor_ij = ii ^ jj
    eye = jnp.where(ii == jj, 1.0, 0.0)

    def intra(i, c):
        items, where = [], []
        for sub in range(DN_INTRA_UNROLL):
            ci = i * DN_INTRA_UNROLL + sub
            rows = pl.ds(pl.multiple_of(ci * DN_CHUNK, DN_CHUNK), DN_CHUNK)
            for direction in (0, 1):
                if direction == 0:
                    beta, gcol = bf_ref[rows, :], gf_ref[rows, :]
                    grow = rf_ref[ci, pl.ds(SMALL_AF + head, 1), :]
                else:
                    beta, gcol = bb_ref[rows, :], gb_ref[rows, :]
                    grow = rb_ref[ci, pl.ds(SMALL_AB + head, 1), :]
                items.append((qn_ref[rows, :], kn_ref[rows, :], vn_ref[rows, :], beta, gcol, grow, direction == 0))
                where.append((direction, ci, rows))
        outs = (of_ref, ob_ref)
        for (direction, ci, rows), (b, p, o_part, qp, gl) in zip(where, _dn_intra_group(items, ii, jj, xor_ij, eye)):
            b_ref[direction, ci] = b
            p_ref[direction, ci] = p
            outs[direction][rows, :] = o_part
            qp_ref[direction, rows, :] = qp
            gl_ref[direction, ci] = jnp.broadcast_to(gl, (SUBLANES, LANES))
        return c
    lax.fori_loop(0, nchunk // DN_INTRA_UNROLL, intra, 0)

    s_ref[...] = jnp.zeros_like(s_ref)

    def scan(i, c):
        cis = (i, nchunk - 1 - i)
        rows = [pl.ds(pl.multiple_of(ci * DN_CHUNK, DN_CHUNK), DN_CHUNK) for ci in cis]
        s_old = [s_ref[d] for d in (0, 1)]
        s_bf = [_bf(s) for s in s_old]
        s_new = [s_old[d] * gl_ref[d, cis[d], 0:1, :] + b_ref[d, cis[d]] - _dot(p_ref[d, cis[d]], s_bf[d])
                 for d in (0, 1)]
        o_add = [_dot(qp_ref[d, rows[d], :], s_bf[d]) for d in (0, 1)]
        for d, out_ref in ((0, of_ref), (1, ob_ref)):
            s_ref[d] = s_new[d]
            out_ref[rows[d], :] += o_add[d]
        return c
    lax.fori_loop(0, nchunk, scan, 0)

    def fin(i, c):
        rows = pl.ds(pl.multiple_of(i * ROW_CHUNK, ROW_CHUNK), ROW_CHUNK)
        o = of_ref[rows, :] + ob_ref[rows, :]
        o = o * lax.rsqrt(jnp.mean(o * o, axis=-1, keepdims=True) + EPS) * ng_ref[...]
        o_ref[rows, :] = _bf(o * _silu(gate_ref[rows, :]))
        return c
    lax.fori_loop(0, nrow, fin, 0)


def _deltanet(proj, conv_qkv, alog_row, dtb_row, norm_dn_row, layer, batch, seq):
    nchunk = seq // DN_CHUNK
    hd = DN_HEAD_DIM
    qb, gb, sb = OFF_DN_QKV // hd, OFF_DN_GATE // hd, OFF_SMALL // LANES
    seq_buf = pltpu.VMEM((seq, hd), jnp.float32)
    return pl.pallas_call(
        _dn_kernel,
        grid=(batch, DN_HEADS),
        in_specs=[
            pl.BlockSpec((seq, hd), lambda b, h: (b, qb + h)),
            pl.BlockSpec((seq, hd), lambda b, h: (b, qb + DN_HEADS + h)),
            pl.BlockSpec((seq, hd), lambda b, h: (b, qb + 2 * DN_HEADS + h)),
            pl.BlockSpec((seq, hd), lambda b, h: (b, gb + h)),
            pl.BlockSpec((seq, LANES), lambda b, h: (b, sb)),
            pl.BlockSpec((None, 3, hd), lambda b, h: (layer, 0, h)),
            pl.BlockSpec((None, 3, hd), lambda b, h: (layer, 0, DN_HEADS + h)),
            pl.BlockSpec((None, 3, hd), lambda b, h: (layer, 0, 2 * DN_HEADS + h)),
            pl.BlockSpec((None, 1, LANES), lambda b, h: (layer, 0, 0)),
            pl.BlockSpec((None, 1, LANES), lambda b, h: (layer, 0, 0)),
            pl.BlockSpec((None, 1, hd), lambda b, h: (layer, 0, 0)),
        ],
        out_specs=pl.BlockSpec((seq, hd), lambda b, h: (b, h)),
        out_shape=jax.ShapeDtypeStruct((batch * seq, DN_WIDTH), MIX_DTYPE),
        scratch_shapes=[
            pltpu.VMEM((seq + 2 * HALO, hd), jnp.float32),
            seq_buf, seq_buf, seq_buf,
            seq_buf, seq_buf, seq_buf, seq_buf,
            pltpu.VMEM((nchunk, LANES, DN_CHUNK), jnp.float32),
            pltpu.VMEM((nchunk, LANES, DN_CHUNK), jnp.float32),
            seq_buf, seq_buf,
            pltpu.VMEM((2, hd, hd), jnp.float32),
            pltpu.VMEM((2, nchunk, hd, hd), jnp.float32),
            pltpu.VMEM((2, nchunk, hd, hd), jnp.bfloat16),
            pltpu.VMEM((2, seq, hd), jnp.bfloat16),
            pltpu.VMEM((2, nchunk, SUBLANES, LANES), jnp.float32),
            seq_buf, seq_buf, seq_buf,
        ],
        compiler_params=_cparams(("arbitrary", "arbitrary")),
        name="deltanet",
    )(proj, proj, proj, proj, proj, conv_qkv, conv_qkv, conv_qkv, alog_row, dtb_row, norm_dn_row)


def _swa_kernel(q_ref, k_ref, v_ref, pos_ref, invf_ref, rot_ref, ng_ref, o_ref,
                cos_ref, sin_ref, qr_ref, kr_ref, m_ref, a_ref, bias_ref):
    seq = o_ref.shape[0]
    hp = pl.program_id(1)
    step = pl.program_id(2)
    nrow = seq // ROW_CHUNK

    @pl.when((hp == 0) & (step == 0))
    def _():
        def body(i, c):
            rows = pl.ds(pl.multiple_of(i * ROW_CHUNK, ROW_CHUNK), ROW_CHUNK)
            ang = pos_ref[rows, :].astype(jnp.float32) * invf_ref[...]
            cos_ref[rows, :] = jnp.cos(ang)
            sin_ref[rows, :] = jnp.sin(ang)
            return c
        lax.fori_loop(0, nrow, body, 0)

    def rope(i, c):
        rows = pl.ds(pl.multiple_of(i * ROPE_ROWS, ROPE_ROWS), ROPE_ROWS)
        cs, sn = cos_ref[rows, :], sin_ref[rows, :]
        x = jnp.concatenate([q_ref[rows, :], k_ref[rows, :]], axis=1)
        hi = _bf(x)
        lo = _bf(x - hi.astype(jnp.float32))
        xr = _dot(hi, rot_ref[...]) + _dot(lo, rot_ref[...])
        y = x * jnp.concatenate([cs, cs], axis=1) + xr * jnp.concatenate([sn, sn], axis=1)
        qr_ref[rows, :] = y[:, :LANES] * SWA_Q_SCALE
        kr_ref[rows, :] = y[:, LANES:]
        return c
    lax.fori_loop(0, seq // ROPE_ROWS, rope, 0)

    def run_pattern(dil, first):
        sub_len = seq // dil
        nblk = sub_len // SWA_QB
        win = min(sub_len, SWA_QB + 2 * SWA_RADIUS)
        lane_q = lax.broadcasted_iota(jnp.int32, (SWA_QB, LANES), 1)
        head_lo = lane_q < SWA_HEAD_DIM
        lane_w = lax.broadcasted_iota(jnp.int32, (win, LANES), 1)
        own_v = (lane_w < SWA_HEAD_DIM, lane_w >= SWA_HEAD_DIM)
        dij = (lax.broadcasted_iota(jnp.int32, (SWA_QB, win), 0)
               - lax.broadcasted_iota(jnp.int32, (SWA_QB, win), 1))
        for k in range(3 if sub_len > win else 1):
            bias_ref[k, :, 0:win] = jnp.where(jnp.abs(dij + k * SWA_RADIUS) <= SWA_RADIUS, 0.0, NEG_INF)

        def sl(start, size):
            return pl.ds(start, size) if dil == 1 else pl.ds(start, size, stride=dil)

        def body(i, c):
            blocks = []
            for sub in range(SWA_UNROLL):
                it = i * SWA_UNROLL + sub
                res = it // nblk
                qs = (it % nblk) * SWA_QB
                ws = jnp.clip(qs - SWA_RADIUS, 0, sub_len - win)
                rq = sl(res + qs * dil, SWA_QB)
                rk = sl(res + ws * dil, win)
                bias = bias_ref[(qs - ws) // SWA_RADIUS, :, 0:win]
                blocks.append((rq, qr_ref[rq, :], _bf(kr_ref[rk, :]), v_ref[rk, :], bias))
            heads = [(b, hh) for b in range(SWA_UNROLL) for hh in (0, 1)]
            own = (head_lo, jnp.logical_not(head_lo))
            s_all = [_dot_nt(_bf(jnp.where(own[hh], blocks[b][1], 0.0)), blocks[b][2]) + blocks[b][4]
                     for b, hh in heads]
            v_ext = [_bf(jnp.where(own_v[hh], blocks[b][3], 1.0)) for b, hh in heads]
            m_cur = [jnp.broadcast_to(jnp.max(s, axis=-1, keepdims=True), (SWA_QB, LANES)) for s in s_all]
            if first:
                m_new = m_cur
            else:
                m_old = [m_ref[hh, blocks[b][0], :] for b, hh in heads]
                m_new = [jnp.maximum(m_o, m_c) for m_o, m_c in zip(m_old, m_cur)]
            p_all = [jnp.exp2(s - jnp.concatenate([m_n] * (win // LANES), axis=1)) for s, m_n in zip(s_all, m_new)]
            pv_all = [_dot(_bf(p), v) for p, v in zip(p_all, v_ext)]
            for idx, (b, hh) in enumerate(heads):
                rq = blocks[b][0]
                if first:
                    a_ref[hh, rq, :] = pv_all[idx]
                else:
                    a_ref[hh, rq, :] = jnp.exp2(m_old[idx] - m_new[idx]) * a_ref[hh, rq, :] + pv_all[idx]
                m_ref[hh, rq, :] = m_new[idx]
            return c
        lax.fori_loop(0, dil * nblk // SWA_UNROLL, body, 0)

    for s_idx, p_idx in enumerate(SWA_ORDER):
        pl.when(step == s_idx)(functools.partial(run_pattern, SWA_PATTERNS[p_idx][1], s_idx == 0))

    @pl.when(step == SWA_N_PAT - 1)
    def _():
        lo = lax.broadcasted_iota(jnp.int32, (ROW_CHUNK, LANES), 1) < SWA_HEAD_DIM

        def fin(i, c):
            rows = pl.ds(pl.multiple_of(i * ROW_CHUNK, ROW_CHUNK), ROW_CHUNK)
            a0, a1 = a_ref[0, rows, :], a_ref[1, rows, :]
            o = jnp.where(lo, a0 / pltpu.roll(a0, SWA_HEAD_DIM, 1), a1 / pltpu.roll(a1, SWA_HEAD_DIM, 1))
            sq = o * o
            s_lo = jnp.sum(jnp.where(lo, sq, 0.0), axis=-1, keepdims=True)
            s_hi = jnp.sum(jnp.where(lo, 0.0, sq), axis=-1, keepdims=True)
            ms = jnp.where(lo, s_lo, s_hi) * (1.0 / SWA_HEAD_DIM)
            o_ref[rows, :] = _bf(o * lax.rsqrt(ms + EPS) * ng_ref[...])
            return c
        lax.fori_loop(0, nrow, fin, 0)


def _swa(proj, pos_col, invf_row, rot_mat, norm_c_row, layer, batch, seq):
    base = OFF_SWA // LANES
    per_pat = SWA_WIDTH // LANES
    qkv = SWA_QKV_WIDTH // LANES
    pairs = SWA_HEADS // 2
    seq_buf = pltpu.VMEM((seq, LANES), jnp.float32)
    first, stride = SWA_ORDER[0], SWA_ORDER[1] - SWA_ORDER[0]
    assert tuple(first + p * stride for p in range(SWA_N_PAT)) == SWA_ORDER

    def spec(which):
        return pl.BlockSpec((seq, LANES),
                            lambda b, hp, p: (b, base + which * qkv + (first + p * stride) * per_pat + hp))

    return pl.pallas_call(
        _swa_kernel,
        grid=(batch, pairs, SWA_N_PAT),
        in_specs=[
            spec(0), spec(1), spec(2),
            pl.BlockSpec((seq, 1), lambda b, hp, p: (b, 0)),
            pl.BlockSpec((1, LANES), lambda b, hp, p: (0, 0)),
            pl.BlockSpec((2 * LANES, 2 * LANES), lambda b, hp, p: (0, 0)),
            pl.BlockSpec((None, 1, LANES), lambda b, hp, p: (layer, 0, hp)),
        ],
        out_specs=pl.BlockSpec((seq, LANES), lambda b, hp, p: (b, hp)),
        out_shape=jax.ShapeDtypeStruct((batch * seq, SWA_WIDTH), MIX_DTYPE),
        scratch_shapes=([seq_buf] * 4 + [pltpu.VMEM((2, seq, LANES), jnp.float32)] * 2
                        + [pltpu.VMEM((3, SWA_QB, SWA_QB + 2 * SWA_RADIUS), jnp.float32)]),
        compiler_params=_cparams(("arbitrary", "arbitrary", "arbitrary")),
        name="swa",
    )(proj, proj, proj, pos_col, invf_row, rot_mat, norm_c_row)


def _outproj_kernel(x_ref, ya_ref, yb_ref, yc_ref, w_ref, o_ref):
    a0, a1, a2 = CONV_CH, CONV_CH + DN_WIDTH, MIX_WIDTH
    acc = _dot(ya_ref[...], w_ref[0:a0, :])
    acc = acc + _dot(yb_ref[...], w_ref[a0:a1, :])
    acc = acc + _dot(yc_ref[...], w_ref[a1:a2, :])
    o_ref[...] = x_ref[...] + acc


def _out_proj(x2d, y_a, y_b, y_c, w_bf, layer):
    n = x2d.shape[0]
    tm = OUT_TM
    return pl.pallas_call(
        _outproj_kernel,
        grid=(n // tm,),
        in_specs=[
            pl.BlockSpec((tm, D_MODEL), lambda i: (i, 0)),
            pl.BlockSpec((tm, CONV_CH), lambda i: (i, 0)),
            pl.BlockSpec((tm, DN_WIDTH), lambda i: (i, 0)),
            pl.BlockSpec((tm, SWA_WIDTH), lambda i: (i, 0)),
            pl.BlockSpec((None, MIX_WIDTH, D_MODEL), lambda i: (layer, 0, 0)),
        ],
        out_specs=pl.BlockSpec((tm, D_MODEL), lambda i: (i, 0)),
        out_shape=jax.ShapeDtypeStruct((n, D_MODEL), jnp.float32),
        compiler_params=_cparams(("arbitrary",)),
        name="out_proj",
    )(x2d, y_a, y_b, y_c, w_bf)


def _ffn_kernel(xp_ref, x_ref, xn_ref, g_ref, wu_ref, cf_ref, wd_ref, gf_ref, o_ref,
                h_ref, a_ref, *, tiles_per_seq, final_norm):
    tm = x_ref.shape[0]
    t = pl.program_id(0) % tiles_per_seq

    def norm(x):
        return x * lax.rsqrt(jnp.mean(x * x, axis=-1, keepdims=True) + EPS) * g_ref[...]

    halo = HALO_BF16
    hp = norm(xp_ref[...])
    hn = norm(xn_ref[...])
    h_ref[0:halo, :] = _bf(jnp.where(t == 0, 0.0, hp))
    h_ref[halo + tm:halo + tm + halo, :] = _bf(jnp.where(t == tiles_per_seq - 1, 0.0, hn))

    def fill(i, c):
        rows = pl.ds(pl.multiple_of(i * ROW_CHUNK, ROW_CHUNK), ROW_CHUNK)
        h_ref[pl.ds(pl.multiple_of(i * ROW_CHUNK + halo, halo), ROW_CHUNK), :] = _bf(norm(x_ref[rows, :]))
        return c
    lax.fori_loop(0, tm // ROW_CHUNK, fill, 0)

    for j in range(D_FF // FF_CHUNK):
        h = h_ref[...]
        gcols = slice(j * FF_CHUNK, (j + 1) * FF_CHUNK)
        vcols = slice(D_FF + j * FF_CHUNK, D_FF + (j + 1) * FF_CHUNK)
        gate = _conv3_val(_dot(h, wu_ref[:, gcols]), cf_ref[:, gcols], tm, halo)
        val = _conv3_val(_dot(h, wu_ref[:, vcols]), cf_ref[:, vcols], tm, halo)
        a_ref[:, gcols] = _bf(_silu(gate) * val)
    o_ref[...] = _dot(a_ref[...], wd_ref[...])

    def fin(i, c):
        rows = pl.ds(pl.multiple_of(i * ROW_CHUNK, ROW_CHUNK), ROW_CHUNK)
        y = x_ref[rows, :] + o_ref[rows, :]
        if final_norm:
            y = y * lax.rsqrt(jnp.mean(y * y, axis=-1, keepdims=True) + EPS) * gf_ref[...]
        o_ref[rows, :] = y
        return c
    lax.fori_loop(0, tm // ROW_CHUNK, fin, 0)


def _ffn(x2d, g_row, wu, cf, wd, gf_row, layer, seq, final_norm):
    n = x2d.shape[0]
    tm = FFN_TM
    tiles_per_seq = seq // tm
    halo = HALO_BF16
    hb = tm // halo
    nhb = n // halo
    whole = lambda shape: pl.BlockSpec((None,) + shape, lambda i: (layer,) + (0,) * len(shape),
                                       pipeline_mode=pl.Buffered(1))
    kern = functools.partial(_ffn_kernel, tiles_per_seq=tiles_per_seq, final_norm=final_norm)
    return pl.pallas_call(
        kern,
        grid=(n // tm,),
        in_specs=[
            pl.BlockSpec((halo, D_MODEL), lambda i: (jnp.maximum(i * hb - 1, 0), 0)),
            pl.BlockSpec((tm, D_MODEL), lambda i: (i, 0)),
            pl.BlockSpec((halo, D_MODEL), lambda i: (jnp.minimum((i + 1) * hb, nhb - 1), 0)),
            whole((1, D_MODEL)),
            whole((D_MODEL, 2 * D_FF)),
            whole((3, 2 * D_FF)),
            whole((D_FF, D_MODEL)),
            pl.BlockSpec((1, D_MODEL), lambda i: (0, 0)),
        ],
        out_specs=pl.BlockSpec((tm, D_MODEL), lambda i: (i, 0)),
        out_shape=jax.ShapeDtypeStruct((n, D_MODEL), jnp.float32),
        scratch_shapes=[
            pltpu.VMEM((tm + 2 * halo, D_MODEL), jnp.bfloat16),
            pltpu.VMEM((tm, D_FF), jnp.bfloat16),
        ],
        compiler_params=_cparams(("arbitrary",)),
        name="ffn",
    )(x2d, x2d, x2d, g_row, wu, cf, wd, gf_row)


def _reorder_w_in(w):
    n_small = 4 * DN_HEADS
    pad = jnp.zeros(w.shape[:-1] + (IN_COLS - OFF_SMALL - n_small,), w.dtype)
    return _bf(jnp.concatenate(
        [w[..., :OFF_SWA], w[..., OFF_SWA + n_small:], w[..., OFF_SWA:OFF_SWA + n_small], pad], axis=-1))


def _small_rows(f_vals, b_vals):
    row = jnp.concatenate([f_vals, b_vals], axis=-1).astype(jnp.float32)
    return jnp.pad(row, ((0, 0), (SMALL_AF, LANES - SMALL_AF - row.shape[-1])))[:, None, :]


def kernel(x, positions, norm_mix, w_in, conv_a, norm_a, conv_qkv, a_log_f, a_log_b, dt_bias_f, dt_bias_b,
           norm_dn, norm_c, w_o, norm_ffn, w_up, conv_ffn, w_down, norm_final):
    batch, seq, _ = x.shape
    depth = w_in.shape[0]
    x2d = x.reshape(batch * seq, D_MODEL)
    pos_col = positions.reshape(batch * seq, 1)

    half = ROPE_DIM // 2
    inv_freq = ROPE_THETA ** (-jnp.arange(half, dtype=jnp.float32) / half)
    d_idx = np.arange(LANES) % SWA_HEAD_DIM
    invf_row = jnp.where(jnp.asarray(d_idx < ROPE_DIM), inv_freq[d_idx % half], 0.0).reshape(1, LANES)
    rot_np = np.zeros((2 * LANES, 2 * LANES), np.float32)
    for lane_idx in range(2 * LANES):
        if d_idx[lane_idx % LANES] < half:
            rot_np[lane_idx + half, lane_idx] = -1.0
        elif d_idx[lane_idx % LANES] < ROPE_DIM:
            rot_np[lane_idx - half, lane_idx] = 1.0
    rot_mat = jnp.asarray(rot_np, jnp.bfloat16)
    grp = np.arange(CONV_CH) // (CONV_CH // CONV_GROUPS)
    group_mat = _bf(jnp.asarray(grp[:, None] == grp[None, :], jnp.float32))

    rows = lambda a: a[:, None, :]
    w_in_bf, w_o_bf, w_up_bf, w_down_bf = _reorder_w_in(w_in), _bf(w_o), _bf(w_up), _bf(w_down)
    alog_rows, dtb_rows = _small_rows(a_log_f, a_log_b), _small_rows(dt_bias_f, dt_bias_b)
    for l in range(depth):
        proj = _proj_in(x2d, rows(norm_mix), w_in_bf, l, batch, seq)
        y_a = _mix_a(proj, conv_a, rows(norm_a), group_mat, l, batch, seq)
        y_b = _deltanet(proj, conv_qkv, alog_rows, dtb_rows, rows(norm_dn), l, batch, seq)
        y_c = _swa(proj, pos_col, invf_row, rot_mat, rows(norm_c), l, batch, seq)
        x2d = _out_proj(x2d, y_a, y_b, y_c, w_o_bf, l)
        x2d = _ffn(x2d, rows(norm_ffn), w_up_bf, conv_ffn, w_down_bf, norm_final.reshape(1, D_MODEL), l, seq,
                   l == depth - 1)
    return x2d.reshape(batch, seq, D_MODEL)
```

```python
import functools

import numpy as np
import jax
import jax.numpy as jnp
from jax import lax
from jax.experimental import pallas as pl
from jax.experimental.pallas import tpu as pltpu

D_MODEL = 1024
EPS = 1e-6
NEG_INF = -1e30
CONV_CH = 256
CONV_GROUPS = 4
DN_HEADS = 4
DN_HEAD_DIM = 128
DN_WIDTH = DN_HEADS * DN_HEAD_DIM
SWA_HEAD_DIM = 64
SWA_HEADS = 4
SWA_PATTERNS = ((128, 1), (512, 4), (2048, 16))
SWA_N_PAT = 3
SWA_WIDTH = SWA_HEADS * SWA_HEAD_DIM
SWA_QKV_WIDTH = SWA_N_PAT * SWA_WIDTH
ROPE_THETA = 500000.0
ROPE_DIM = SWA_HEAD_DIM // 4
D_FF = 2816
MIX_WIDTH = CONV_CH + DN_WIDTH + SWA_WIDTH

MIX_DTYPE = jnp.bfloat16
LANES = 128
SUBLANES = 8
VMEM_LIMIT = 56 * 1024 * 1024

OFF_A = 0
MIXA_COLS = 3 * CONV_CH
OFF_DN_QKV = OFF_A + MIXA_COLS
OFF_DN_GATE = OFF_DN_QKV + 3 * DN_WIDTH
OFF_SWA = OFF_DN_GATE + DN_WIDTH
OFF_SMALL = OFF_SWA + 3 * SWA_QKV_WIDTH
IN_COLS = 5376
PROJ_CN = 768
SMALL_BF, SMALL_BB, SMALL_AF, SMALL_AB = 0, 4, 8, 12

DN_CHUNK = 256
DN_INTRA_UNROLL = 4
ROW_CHUNK = 256
HALO = SUBLANES
HALO_BF16 = 2 * SUBLANES
FF_CHUNK = 256
FFN_TM = 1024
OUT_TM = 1024
SWA_QB = 128
SWA_RADIUS = 64
SWA_UNROLL = 8
ROPE_ROWS = 512
SWA_Q_SCALE = SWA_HEAD_DIM ** -0.5 * float(np.log2(np.e))
SWA_ORDER = (2, 1, 0)


def _cparams(sem, vmem_limit=VMEM_LIMIT):
    return pltpu.CompilerParams(dimension_semantics=sem, vmem_limit_bytes=vmem_limit)


def _bf(x):
    return x.astype(jnp.bfloat16)


def _dot(a, b):
    return jnp.dot(a, b, preferred_element_type=jnp.float32)


def _dot_nt(a, b):
    return lax.dot_general(a, b, (((1,), (1,)), ((), ())), preferred_element_type=jnp.float32)


def _silu(x):
    return x * (1.0 / (1.0 + jnp.exp(-x)))


def _conv3_val(ext, w, rows, halo):
    n = ext.shape[0]
    prev = pltpu.roll(ext, 1, 0)[halo:halo + rows]
    nxt = pltpu.roll(ext, n - 1, 0)[halo:halo + rows]
    return prev * w[0:1] + ext[halo:halo + rows] * w[1:2] + nxt * w[2:3]


def _conv3_ref(ref, lead, start, rows, w):
    def tap(off):
        return ref[lead + (pl.ds(start + off, rows), slice(None))]
    return tap(-1) * w[0:1] + tap(0) * w[1:2] + tap(1) * w[2:3]


def _proj_kernel(x_ref, g_ref, w_ref, cw_ref, ng_ref, gm_ref, o_ref, ya_ref, h_ref, a_ref, u_ref):
    col = pl.program_id(1)

    @pl.when(col == 0)
    def _():
        def body(i, c):
            rows = pl.ds(pl.multiple_of(i * ROW_CHUNK, ROW_CHUNK), ROW_CHUNK)
            x = x_ref[rows, :]
            y = x * lax.rsqrt(jnp.mean(x * x, axis=-1, keepdims=True) + EPS)
            h_ref[rows, :] = _bf(y * g_ref[...])
            return c
        lax.fori_loop(0, x_ref.shape[0] // ROW_CHUNK, body, 0)
        a_ref[...] = _dot(h_ref[...], w_ref[...])
        _mixa_body(a_ref, cw_ref, ng_ref, gm_ref, ya_ref, u_ref)

    @pl.when(col > 0)
    def _():
        o_ref[...] = _dot(h_ref[...], w_ref[...])


def _proj_in(x2d, g_row, w_bf, conv_a, norm_a_row, group_mat, layer, batch, seq):
    ncol = IN_COLS // PROJ_CN
    assert PROJ_CN == MIXA_COLS
    return pl.pallas_call(
        _proj_kernel,
        grid=(batch, ncol),
        in_specs=[
            pl.BlockSpec((seq, D_MODEL), lambda b, j: (b, 0)),
            pl.BlockSpec((None, 1, D_MODEL), lambda b, j: (layer, 0, 0)),
            pl.BlockSpec((None, D_MODEL, PROJ_CN), lambda b, j: (layer, 0, j)),
            pl.BlockSpec((None, 3, CONV_CH), lambda b, j: (layer, 0, 0)),
            pl.BlockSpec((None, 1, CONV_CH), lambda b, j: (layer, 0, 0)),
            pl.BlockSpec((CONV_CH, CONV_CH), lambda b, j: (0, 0)),
        ],
        out_specs=[pl.BlockSpec((seq, PROJ_CN), lambda b, j: (b, jnp.maximum(j - 1, 0))),
                   pl.BlockSpec((seq, CONV_CH), lambda b, j: (b, 0))],
        out_shape=[jax.ShapeDtypeStruct((batch * seq, IN_COLS - MIXA_COLS), jnp.float32),
                   jax.ShapeDtypeStruct((batch * seq, CONV_CH), MIX_DTYPE)],
        scratch_shapes=[pltpu.VMEM((seq, D_MODEL), jnp.bfloat16),
                        pltpu.VMEM((seq, MIXA_COLS), jnp.float32),
                        pltpu.VMEM((seq + 2 * HALO, CONV_CH), jnp.float32)],
        compiler_params=_cparams(("arbitrary", "arbitrary")),
        name="proj_in",
    )(x2d, g_row, w_bf, conv_a, norm_a_row, group_mat)


def _mixa_body(p_ref, cw_ref, ng_ref, gm_ref, o_ref, u_ref):
    seq = o_ref.shape[0]
    nchunk = seq // ROW_CHUNK
    zeros = jnp.zeros((HALO, CONV_CH), jnp.float32)
    u_ref[0:HALO, :] = zeros
    u_ref[HALO + seq:HALO + seq + HALO, :] = zeros

    def fill(i, c):
        rows = pl.ds(pl.multiple_of(i * ROW_CHUNK, ROW_CHUNK), ROW_CHUNK)
        xa = p_ref[rows, 0:CONV_CH]
        gc = p_ref[rows, 2 * CONV_CH:3 * CONV_CH]
        u_ref[pl.ds(pl.multiple_of(i * ROW_CHUNK + HALO, HALO), ROW_CHUNK), :] = gc * xa
        return c
    lax.fori_loop(0, nchunk, fill, 0)

    for i in range(nchunk):
        start = i * ROW_CHUNK
        gb = p_ref[start:start + ROW_CHUNK, CONV_CH:2 * CONV_CH]
        y = gb * _conv3_ref(u_ref, (), start + HALO, ROW_CHUNK, cw_ref[...])
        sq = y * y
        hi = _bf(sq)
        lo = _bf(sq - hi.astype(jnp.float32))
        ms = (_dot(hi, gm_ref[...]) + _dot(lo, gm_ref[...])) * (1.0 / (CONV_CH // CONV_GROUPS))
        o_ref[start:start + ROW_CHUNK, :] = _bf(y * lax.rsqrt(ms + EPS) * ng_ref[...])


def _tri_inverse_group(a_list, lowers, xor_ij, eye):
    n = a_list[0].shape[0]
    d_list = [eye - jnp.where(xor_ij == 1, a, 0.0) for a in a_list]
    level = 1
    while (1 << level) < n:
        s = 1 << level
        mask = (xor_ij >> level) == 1
        e_list = [_bf(jnp.where(mask, a, 0.0)) for a in a_list]
        d_bf = [_bf(d) for d in d_list]
        if s < SUBLANES:
            de_list = [_dot(db, e) for db, e in zip(d_bf, e_list)]
            d_list = [d - _dot(_bf(de), db) for d, de, db in zip(d_list, de_list, d_bf)]
        else:
            blocks = [[d[b * s:(b + 1) * s] for b in range(n // s)] for d in d_list]
            picks = [range(1, n // s, 2) if lower else range(0, n // s, 2) for lower in lowers]
            sel = [_bf(jnp.concatenate([blk[b] for b in pk], axis=0)) for blk, pk in zip(blocks, picks)]
            de_list = [_dot(x, e) for x, e in zip(sel, e_list)]
            upd = [_dot(_bf(de), db) for de, db in zip(de_list, d_bf)]
            d_list = []
            for blk, pk, u in zip(blocks, picks, upd):
                for k, b in enumerate(pk):
                    blk[b] = blk[b] - u[k * s:(k + 1) * s]
                d_list.append(jnp.concatenate(blk, axis=0))
        level += 1
    return d_list


def _dn_intra_group(items, ii, jj, xor_ij, eye):
    c = items[0][0].shape[0]
    pre = []
    for q, k, v, beta, gcol, grow, lower in items:
        if lower:
            incl, strict, glast = ii >= jj, ii > jj, gcol[c - 1:c, :]
        else:
            incl, strict, glast = ii <= jj, ii < jj, gcol[0:1, :]
        diff = jnp.concatenate([gcol] * (c // LANES), axis=1) - grow
        dec = jnp.where(incl, jnp.exp(jnp.where(incl, diff, 0.0)), 0.0)
        kb = k * beta
        k_bf = _bf(k)
        a = jnp.where(strict, _dot_nt(_bf(kb), k_bf) * dec, 0.0)
        qk = _bf(_dot_nt(_bf(q), k_bf) * dec)
        pre.append((a, qk, kb, glast))
    lhs_list = [jnp.concatenate([_bf((item[1] * jnp.exp(p[3] - item[4])).T), p[1]], axis=0)
                for item, p in zip(items, pre)]
    tinv_list = _tri_inverse_group([p[0] for p in pre], [item[6] for item in items], xor_ij, eye)
    hd = DN_HEAD_DIM
    eg_list = [jnp.exp(item[4]) for item in items]
    sol_list = [_bf(_dot(_bf(tinv), _bf(jnp.concatenate([item[2] * item[3], p[2] * eg], axis=1))))
                for item, p, eg, tinv in zip(items, pre, eg_list, tinv_list)]
    r_list = [_dot(lhs, sol) for lhs, sol in zip(lhs_list, sol_list)]
    return [(r[:hd, :hd], _bf(r[:hd, hd:]), r[hd:, :hd], _bf(item[0] * eg - r[hd:, hd:]), jnp.exp(p[3]))
            for item, p, eg, r in zip(items, pre, eg_list, r_list)]


def _dn_kernel(q_ref, k_ref, v_ref, gate_ref, small_ref, cq_ref, ck_ref, cv_ref, alog_ref, dtb_ref,
               ng_ref, o_ref,
               pad_ref, qn_ref, kn_ref, vn_ref, bf_ref, bb_ref, gf_ref, gb_ref, rf_ref, rb_ref,
               of_ref, ob_ref, s_ref, b_ref, p_ref, qp_ref, gl_ref, beta_ref, cf_ref, cb_ref):
    seq = o_ref.shape[0]
    head = pl.program_id(1)
    nrow = seq // ROW_CHUNK
    nchunk = seq // DN_CHUNK

    zeros = jnp.zeros((HALO, DN_HEAD_DIM), jnp.float32)
    pad_ref[0:HALO, :] = zeros
    pad_ref[HALO + seq:HALO + seq + HALO, :] = zeros

    def conv_silu(src_ref, cw_ref, dst_ref, mode):
        def fill(i, c):
            rows = pl.ds(pl.multiple_of(i * ROW_CHUNK, ROW_CHUNK), ROW_CHUNK)
            pad_ref[pl.ds(pl.multiple_of(i * ROW_CHUNK + HALO, HALO), ROW_CHUNK), :] = src_ref[rows, :]
            return c
        lax.fori_loop(0, nrow, fill, 0)

        for i in range(nrow):
            start = i * ROW_CHUNK
            y = _silu(_conv3_ref(pad_ref, (), start + HALO, ROW_CHUNK, cw_ref[...]))
            if mode != "v":
                y = y * lax.rsqrt(jnp.sum(y * y, axis=-1, keepdims=True) + EPS)
            if mode == "q":
                y = y * (DN_HEAD_DIM ** -0.5)
            dst_ref[start:start + ROW_CHUNK, :] = y

    conv_silu(q_ref, cq_ref, qn_ref, "q")
    conv_silu(k_ref, ck_ref, kn_ref, "k")
    conv_silu(v_ref, cv_ref, vn_ref, "v")

    lane = lax.broadcasted_iota(jnp.int32, (DN_CHUNK, LANES), 1)
    ii = lax.broadcasted_iota(jnp.int32, (DN_CHUNK, DN_CHUNK), 0)
    jj = lax.broadcasted_iota(jnp.int32, (DN_CHUNK, DN_CHUNK), 1)
    tri2 = jnp.concatenate([jnp.where(ii >= jj, 1.0, 0.0), jnp.where(ii <= jj, 1.0, 0.0)],
                           axis=0).astype(jnp.bfloat16)

    def pick(x, col):
        sel = jnp.sum(jnp.where(lane == col, x, 0.0), axis=-1, keepdims=True)
        return jnp.broadcast_to(sel, x.shape)

    @pl.when(head == 0)
    def _():
        neg_a = -jnp.exp(alog_ref[...])

        def gates(i, c):
            cis = [2 * i, 2 * i + 1]
            rows = [pl.ds(pl.multiple_of(ci * DN_CHUNK, DN_CHUNK), DN_CHUNK) for ci in cis]
            sm = [small_ref[r, :] for r in rows]
            z = [s + dtb_ref[...] for s in sm]
            g = [neg_a * (jnp.maximum(zz, 0.0) + jnp.log1p(jnp.exp(-jnp.abs(zz)))) for zz in z]
            g1 = [_bf(x) for x in g]
            r1 = [x - x1.astype(jnp.float32) for x, x1 in zip(g, g1)]
            g2 = [_bf(x) for x in r1]
            g3 = [_bf(x - x2.astype(jnp.float32)) for x, x2 in zip(r1, g2)]
            r = [_dot(tri2, jnp.concatenate(parts, axis=1)) for parts in zip(g1, g2, g3)]
            cf = [x[:DN_CHUNK, 2 * LANES:] + x[:DN_CHUNK, LANES:2 * LANES] + x[:DN_CHUNK, :LANES] for x in r]
            cb = [x[DN_CHUNK:, 2 * LANES:] + x[DN_CHUNK:, LANES:2 * LANES] + x[DN_CHUNK:, :LANES] for x in r]
            for k in range(2):
                beta_ref[rows[k], :] = 1.0 / (1.0 + jnp.exp(-sm[k]))
                cf_ref[rows[k], :] = cf[k]
                cb_ref[rows[k], :] = cb[k]
                rf_ref[cis[k]] = cf[k].T
                rb_ref[cis[k]] = cb[k].T
            return c
        lax.fori_loop(0, nchunk // 2, gates, 0)

    def picks(ci, c):
        rows = pl.ds(pl.multiple_of(ci * DN_CHUNK, DN_CHUNK), DN_CHUNK)
        beta = beta_ref[rows, :]
        bf_ref[rows, :] = pick(beta, SMALL_BF + head)
        bb_ref[rows, :] = pick(beta, SMALL_BB + head)
        gf_ref[rows, :] = pick(cf_ref[rows, :], SMALL_AF + head)
        gb_ref[rows, :] = pick(cb_ref[rows, :], SMALL_AB + head)
        return c
    lax.fori_loop(0, nchunk, picks, 0)

    xor_ij = ii ^ jj
    eye = jnp.where(ii == jj, 1.0, 0.0)

    def intra(i, c):
        items, where = [], []
        for sub in range(DN_INTRA_UNROLL):
            ci = i * DN_INTRA_UNROLL + sub
            rows = pl.ds(pl.multiple_of(ci * DN_CHUNK, DN_CHUNK), DN_CHUNK)
            for direction in (0, 1):
                if direction == 0:
                    beta, gcol = bf_ref[rows, :], gf_ref[rows, :]
                    grow = rf_ref[ci, pl.ds(SMALL_AF + head, 1), :]
                else:
                    beta, gcol = bb_ref[rows, :], gb_ref[rows, :]
                    grow = rb_ref[ci, pl.ds(SMALL_AB + head, 1), :]
                items.append((qn_ref[rows, :], kn_ref[rows, :], vn_ref[rows, :], beta, gcol, grow, direction == 0))
                where.append((direction, ci, rows))
        outs = (of_ref, ob_ref)
        for (direction, ci, rows), (b, p, o_part, qp, gl) in zip(where, _dn_intra_group(items, ii, jj, xor_ij, eye)):
            b_ref[direction, ci] = b
            p_ref[direction, ci] = p
            outs[direction][rows, :] = o_part
            qp_ref[direction, rows, :] = qp
            gl_ref[direction, ci] = jnp.broadcast_to(gl, (SUBLANES, LANES))
        return c
    lax.fori_loop(0, nchunk // DN_INTRA_UNROLL, intra, 0)

    s_ref[...] = jnp.zeros_like(s_ref)

    def scan(i, c):
        cis = (i, nchunk - 1 - i)
        rows = [pl.ds(pl.multiple_of(ci * DN_CHUNK, DN_CHUNK), DN_CHUNK) for ci in cis]
        s_old = [s_ref[d] for d in (0, 1)]
        s_bf = [_bf(s) for s in s_old]
        s_new = [s_old[d] * gl_ref[d, cis[d], 0:1, :] + b_ref[d, cis[d]] - _dot(p_ref[d, cis[d]], s_bf[d])
                 for d in (0, 1)]
        o_add = [_dot(qp_ref[d, rows[d], :], s_bf[d]) for d in (0, 1)]
        for d, out_ref in ((0, of_ref), (1, ob_ref)):
            s_ref[d] = s_new[d]
            out_ref[rows[d], :] += o_add[d]
        return c
    lax.fori_loop(0, nchunk, scan, 0)

    def fin(i, c):
        rows = pl.ds(pl.multiple_of(i * ROW_CHUNK, ROW_CHUNK), ROW_CHUNK)
        o = of_ref[rows, :] + ob_ref[rows, :]
        o = o * lax.rsqrt(jnp.mean(o * o, axis=-1, keepdims=True) + EPS) * ng_ref[...]
        o_ref[rows, :] = _bf(o * _silu(gate_ref[rows, :]))
        return c
    lax.fori_loop(0, nrow, fin, 0)


def _deltanet(proj, conv_qkv, alog_row, dtb_row, norm_dn_row, layer, batch, seq):
    nchunk = seq // DN_CHUNK
    hd = DN_HEAD_DIM
    qb, gb, sb = ((off - MIXA_COLS) // hd for off in (OFF_DN_QKV, OFF_DN_GATE, OFF_SMALL))
    seq_buf = pltpu.VMEM((seq, hd), jnp.float32)
    return pl.pallas_call(
        _dn_kernel,
        grid=(batch, DN_HEADS),
        in_specs=[
            pl.BlockSpec((seq, hd), lambda b, h: (b, qb + h)),
            pl.BlockSpec((seq, hd), lambda b, h: (b, qb + DN_HEADS + h)),
            pl.BlockSpec((seq, hd), lambda b, h: (b, qb + 2 * DN_HEADS + h)),
            pl.BlockSpec((seq, hd), lambda b, h: (b, gb + h)),
            pl.BlockSpec((seq, LANES), lambda b, h: (b, sb)),
            pl.BlockSpec((None, 3, hd), lambda b, h: (layer, 0, h)),
            pl.BlockSpec((None, 3, hd), lambda b, h: (layer, 0, DN_HEADS + h)),
            pl.BlockSpec((None, 3, hd), lambda b, h: (layer, 0, 2 * DN_HEADS + h)),
            pl.BlockSpec((None, 1, LANES), lambda b, h: (layer, 0, 0)),
            pl.BlockSpec((None, 1, LANES), lambda b, h: (layer, 0, 0)),
            pl.BlockSpec((None, 1, hd), lambda b, h: (layer, 0, 0)),
        ],
        out_specs=pl.BlockSpec((seq, hd), lambda b, h: (b, h)),
        out_shape=jax.ShapeDtypeStruct((batch * seq, DN_WIDTH), MIX_DTYPE),
        scratch_shapes=[
            pltpu.VMEM((seq + 2 * HALO, hd), jnp.float32),
            seq_buf, seq_buf, seq_buf,
            seq_buf, seq_buf, seq_buf, seq_buf,
            pltpu.VMEM((nchunk, LANES, DN_CHUNK), jnp.float32),
            pltpu.VMEM((nchunk, LANES, DN_CHUNK), jnp.float32),
            seq_buf, seq_buf,
            pltpu.VMEM((2, hd, hd), jnp.float32),
            pltpu.VMEM((2, nchunk, hd, hd), jnp.float32),
            pltpu.VMEM((2, nchunk, hd, hd), jnp.bfloat16),
            pltpu.VMEM((2, seq, hd), jnp.bfloat16),
            pltpu.VMEM((2, nchunk, SUBLANES, LANES), jnp.float32),
            seq_buf, seq_buf, seq_buf,
        ],
        compiler_params=_cparams(("arbitrary", "arbitrary")),
        name="deltanet",
    )(proj, proj, proj, proj, proj, conv_qkv, conv_qkv, conv_qkv, alog_row, dtb_row, norm_dn_row)


def _swa_kernel(q_ref, k_ref, v_ref, pos_ref, invf_ref, rot_ref, ng_ref, gm_ref, o_ref,
                cos_ref, sin_ref, qr_ref, kr_ref, m_ref, a_ref, bias_ref):
    seq = o_ref.shape[0]
    hp = pl.program_id(1)
    step = pl.program_id(2)
    nrow = seq // ROW_CHUNK

    @pl.when((hp == 0) & (step == 0))
    def _():
        def body(i, c):
            rows = pl.ds(pl.multiple_of(i * ROW_CHUNK, ROW_CHUNK), ROW_CHUNK)
            ang = pos_ref[rows, :].astype(jnp.float32) * invf_ref[...]
            cos_ref[rows, :] = jnp.cos(ang)
            sin_ref[rows, :] = jnp.sin(ang)
            return c
        lax.fori_loop(0, nrow, body, 0)

    def rope(i, c):
        rows = pl.ds(pl.multiple_of(i * ROPE_ROWS, ROPE_ROWS), ROPE_ROWS)
        cs, sn = cos_ref[rows, :], sin_ref[rows, :]
        x = jnp.concatenate([q_ref[rows, :], k_ref[rows, :]], axis=1)
        hi = _bf(x)
        lo = _bf(x - hi.astype(jnp.float32))
        xr = _dot(hi, rot_ref[...]) + _dot(lo, rot_ref[...])
        y = x * jnp.concatenate([cs, cs], axis=1) + xr * jnp.concatenate([sn, sn], axis=1)
        qr_ref[rows, :] = y[:, :LANES] * SWA_Q_SCALE
        kr_ref[rows, :] = y[:, LANES:]
        return c
    lax.fori_loop(0, seq // ROPE_ROWS, rope, 0)

    def run_pattern(dil, first):
        sub_len = seq // dil
        nblk = sub_len // SWA_QB
        win = min(sub_len, SWA_QB + 2 * SWA_RADIUS)
        lane_q = lax.broadcasted_iota(jnp.int32, (SWA_QB, LANES), 1)
        head_lo = lane_q < SWA_HEAD_DIM
        lane_w = lax.broadcasted_iota(jnp.int32, (win, LANES), 1)
        own_v = (lane_w < SWA_HEAD_DIM, lane_w >= SWA_HEAD_DIM)
        dij = (lax.broadcasted_iota(jnp.int32, (SWA_QB, win), 0)
               - lax.broadcasted_iota(jnp.int32, (SWA_QB, win), 1))
        for k in range(3 if sub_len > win else 1):
            bias_ref[k, :, 0:win] = jnp.where(jnp.abs(dij + k * SWA_RADIUS) <= SWA_RADIUS, 0.0, NEG_INF)

        def sl(start, size):
            return pl.ds(start, size) if dil == 1 else pl.ds(start, size, stride=dil)

        def body(i, c):
            blocks = []
            for sub in range(SWA_UNROLL):
                it = i * SWA_UNROLL + sub
                res = it // nblk
                qs = (it % nblk) * SWA_QB
                ws = jnp.clip(qs - SWA_RADIUS, 0, sub_len - win)
                rq = sl(res + qs * dil, SWA_QB)
                rk = sl(res + ws * dil, win)
                bias = bias_ref[(qs - ws) // SWA_RADIUS, :, 0:win]
                blocks.append((rq, qr_ref[rq, :], _bf(kr_ref[rk, :]), v_ref[rk, :], bias))
            heads = [(b, hh) for b in range(SWA_UNROLL) for hh in (0, 1)]
            own = (head_lo, jnp.logical_not(head_lo))
            s_all = [_dot_nt(_bf(jnp.where(own[hh], blocks[b][1], 0.0)), blocks[b][2]) + blocks[b][4]
                     for b, hh in heads]
            v_ext = [_bf(jnp.where(own_v[hh], blocks[b][3], 1.0)) for b, hh in heads]
            m_cur = [jnp.broadcast_to(jnp.max(s, axis=-1, keepdims=True), (SWA_QB, LANES)) for s in s_all]
            if first:
                m_new = m_cur
            else:
                m_old = [m_ref[hh, blocks[b][0], :] for b, hh in heads]
                m_new = [jnp.maximum(m_o, m_c) for m_o, m_c in zip(m_old, m_cur)]
            p_all = [jnp.exp2(s - jnp.concatenate([m_n] * (win // LANES), axis=1)) for s, m_n in zip(s_all, m_new)]
            pv_all = [_dot(_bf(p), v) for p, v in zip(p_all, v_ext)]
            for idx, (b, hh) in enumerate(heads):
                rq = blocks[b][0]
                if first:
                    a_ref[hh, rq, :] = pv_all[idx]
                else:
                    a_ref[hh, rq, :] = jnp.exp2(m_old[idx] - m_new[idx]) * a_ref[hh, rq, :] + pv_all[idx]
                m_ref[hh, rq, :] = m_new[idx]
            return c
        lax.fori_loop(0, dil * nblk // SWA_UNROLL, body, 0)

    for s_idx, p_idx in enumerate(SWA_ORDER):
        pl.when(step == s_idx)(functools.partial(run_pattern, SWA_PATTERNS[p_idx][1], s_idx == 0))

    @pl.when(step == SWA_N_PAT - 1)
    def _():
        lo = lax.broadcasted_iota(jnp.int32, (ROW_CHUNK, LANES), 1) < SWA_HEAD_DIM

        def fin(i, c):
            rows = pl.ds(pl.multiple_of(i * ROW_CHUNK, ROW_CHUNK), ROW_CHUNK)
            a0, a1 = a_ref[0, rows, :], a_ref[1, rows, :]
            o = jnp.where(lo, a0, a1) / pltpu.roll(jnp.where(lo, a1, a0), SWA_HEAD_DIM, 1)
            sq = o * o
            hi = _bf(sq)
            rest = _bf(sq - hi.astype(jnp.float32))
            ms = (_dot(hi, gm_ref[...]) + _dot(rest, gm_ref[...])) * (1.0 / SWA_HEAD_DIM)
            o_ref[rows, :] = _bf(o * lax.rsqrt(ms + EPS) * ng_ref[...])
            return c
        lax.fori_loop(0, nrow, fin, 0)


def _swa(proj, pos_col, invf_row, rot_mat, norm_c_row, group_mat, layer, batch, seq):
    base = (OFF_SWA - MIXA_COLS) // LANES
    per_pat = SWA_WIDTH // LANES
    qkv = SWA_QKV_WIDTH // LANES
    pairs = SWA_HEADS // 2
    seq_buf = pltpu.VMEM((seq, LANES), jnp.float32)
    first, stride = SWA_ORDER[0], SWA_ORDER[1] - SWA_ORDER[0]
    assert tuple(first + p * stride for p in range(SWA_N_PAT)) == SWA_ORDER
    assert CONV_CH // CONV_GROUPS == SWA_HEAD_DIM

    def spec(which):
        return pl.BlockSpec((seq, LANES),
                            lambda b, hp, p: (b, base + which * qkv + (first + p * stride) * per_pat + hp))

    return pl.pallas_call(
        _swa_kernel,
        grid=(batch, pairs, SWA_N_PAT),
        in_specs=[
            spec(0), spec(1), spec(2),
            pl.BlockSpec((seq, 1), lambda b, hp, p: (b, 0)),
            pl.BlockSpec((1, LANES), lambda b, hp, p: (0, 0)),
            pl.BlockSpec((2 * LANES, 2 * LANES), lambda b, hp, p: (0, 0)),
            pl.BlockSpec((None, 1, LANES), lambda b, hp, p: (layer, 0, hp)),
            pl.BlockSpec((LANES, LANES), lambda b, hp, p: (0, 0)),
        ],
        out_specs=pl.BlockSpec((seq, LANES), lambda b, hp, p: (b, hp)),
        out_shape=jax.ShapeDtypeStruct((batch * seq, SWA_WIDTH), MIX_DTYPE),
        scratch_shapes=([seq_buf] * 4 + [pltpu.VMEM((2, seq, LANES), jnp.float32)] * 2
                        + [pltpu.VMEM((3, SWA_QB, SWA_QB + 2 * SWA_RADIUS), jnp.float32)]),
        compiler_params=_cparams(("arbitrary", "arbitrary", "arbitrary")),
        name="swa",
    )(proj, proj, proj, pos_col, invf_row, rot_mat, norm_c_row, group_mat)


def _outproj_kernel(x_ref, ya_ref, yb_ref, yc_ref, w_ref, o_ref):
    a0, a1, a2 = CONV_CH, CONV_CH + DN_WIDTH, MIX_WIDTH
    acc = _dot(ya_ref[...], w_ref[0:a0, :])
    acc = acc + _dot(yb_ref[...], w_ref[a0:a1, :])
    acc = acc + _dot(yc_ref[...], w_ref[a1:a2, :])
    o_ref[...] = x_ref[...] + acc


def _out_proj(x2d, y_a, y_b, y_c, w_bf, layer):
    n = x2d.shape[0]
    tm = OUT_TM
    return pl.pallas_call(
        _outproj_kernel,
        grid=(n // tm,),
        in_specs=[
            pl.BlockSpec((tm, D_MODEL), lambda i: (i, 0)),
            pl.BlockSpec((tm, CONV_CH), lambda i: (i, 0)),
            pl.BlockSpec((tm, DN_WIDTH), lambda i: (i, 0)),
            pl.BlockSpec((tm, SWA_WIDTH), lambda i: (i, 0)),
            pl.BlockSpec((None, MIX_WIDTH, D_MODEL), lambda i: (layer, 0, 0)),
        ],
        out_specs=pl.BlockSpec((tm, D_MODEL), lambda i: (i, 0)),
        out_shape=jax.ShapeDtypeStruct((n, D_MODEL), jnp.float32),
        compiler_params=_cparams(("arbitrary",)),
        name="out_proj",
    )(x2d, y_a, y_b, y_c, w_bf)


def _ffn_kernel(xp_ref, x_ref, xn_ref, g_ref, wu_ref, cf_ref, wd_ref, gf_ref, o_ref,
                h_ref, a_ref, *, tiles_per_seq, final_norm):
    tm = x_ref.shape[0]
    t = pl.program_id(0) % tiles_per_seq

    def norm(x):
        return x * lax.rsqrt(jnp.mean(x * x, axis=-1, keepdims=True) + EPS) * g_ref[...]

    halo = HALO_BF16
    hp = norm(xp_ref[...])
    hn = norm(xn_ref[...])
    h_ref[0:halo, :] = _bf(jnp.where(t == 0, 0.0, hp))
    h_ref[halo + tm:halo + tm + halo, :] = _bf(jnp.where(t == tiles_per_seq - 1, 0.0, hn))

    def fill(i, c):
        rows = pl.ds(pl.multiple_of(i * ROW_CHUNK, ROW_CHUNK), ROW_CHUNK)
        h_ref[pl.ds(pl.multiple_of(i * ROW_CHUNK + halo, halo), ROW_CHUNK), :] = _bf(norm(x_ref[rows, :]))
        return c
    lax.fori_loop(0, tm // ROW_CHUNK, fill, 0)

    for j in range(D_FF // FF_CHUNK):
        h = h_ref[...]
        gcols = slice(j * FF_CHUNK, (j + 1) * FF_CHUNK)
        vcols = slice(D_FF + j * FF_CHUNK, D_FF + (j + 1) * FF_CHUNK)
        gate = _conv3_val(_dot(h, wu_ref[:, gcols]), cf_ref[:, gcols], tm, halo)
        val = _conv3_val(_dot(h, wu_ref[:, vcols]), cf_ref[:, vcols], tm, halo)
        a_ref[:, gcols] = _bf(_silu(gate) * val)
    o_ref[...] = _dot(a_ref[...], wd_ref[...])

    def fin(i, c):
        rows = pl.ds(pl.multiple_of(i * ROW_CHUNK, ROW_CHUNK), ROW_CHUNK)
        y = x_ref[rows, :] + o_ref[rows, :]
        if final_norm:
            y = y * lax.rsqrt(jnp.mean(y * y, axis=-1, keepdims=True) + EPS) * gf_ref[...]
        o_ref[rows, :] = y
        return c
    lax.fori_loop(0, tm // ROW_CHUNK, fin, 0)


def _ffn(x2d, g_row, wu, cf, wd, gf_row, layer, seq, final_norm):
    n = x2d.shape[0]
    tm = FFN_TM
    tiles_per_seq = seq // tm
    halo = HALO_BF16
    hb = tm // halo
    nhb = n // halo
    whole = lambda shape: pl.BlockSpec((None,) + shape, lambda i: (layer,) + (0,) * len(shape),
                                       pipeline_mode=pl.Buffered(1))
    kern = functools.partial(_ffn_kernel, tiles_per_seq=tiles_per_seq, final_norm=final_norm)
    return pl.pallas_call(
        kern,
        grid=(n // tm,),
        in_specs=[
            pl.BlockSpec((halo, D_MODEL), lambda i: (jnp.maximum(i * hb - 1, 0), 0)),
            pl.BlockSpec((tm, D_MODEL), lambda i: (i, 0)),
            pl.BlockSpec((halo, D_MODEL), lambda i: (jnp.minimum((i + 1) * hb, nhb - 1), 0)),
            whole((1, D_MODEL)),
            whole((D_MODEL, 2 * D_FF)),
            whole((3, 2 * D_FF)),
            whole((D_FF, D_MODEL)),
            pl.BlockSpec((1, D_MODEL), lambda i: (0, 0)),
        ],
        out_specs=pl.BlockSpec((tm, D_MODEL), lambda i: (i, 0)),
        out_shape=jax.ShapeDtypeStruct((n, D_MODEL), jnp.float32),
        scratch_shapes=[
            pltpu.VMEM((tm + 2 * halo, D_MODEL), jnp.bfloat16),
            pltpu.VMEM((tm, D_FF), jnp.bfloat16),
        ],
        compiler_params=_cparams(("arbitrary",)),
        name="ffn",
    )(x2d, x2d, x2d, g_row, wu, cf, wd, gf_row)


def _win_prep_kernel(w_ref, o_ref):
    n_small = 4 * DN_HEADS
    n_swa = 3 * SWA_QKV_WIDTH
    o_ref[:, 0:OFF_SWA] = _bf(w_ref[:, 0:OFF_SWA])
    o_ref[:, OFF_SWA:OFF_SMALL] = _bf(w_ref[:, OFF_SWA + n_small:OFF_SWA + n_small + n_swa])
    o_ref[:, OFF_SMALL:OFF_SMALL + LANES] = _bf(jnp.concatenate(
        [w_ref[:, OFF_SWA:OFF_SWA + n_small], jnp.zeros((w_ref.shape[0], LANES - n_small), jnp.float32)], axis=1))
    o_ref[:, OFF_SMALL + LANES:] = jnp.zeros((w_ref.shape[0], IN_COLS - OFF_SMALL - LANES), jnp.bfloat16)


def _reorder_w_in(w):
    depth, rows, cols = w.shape
    return pl.pallas_call(
        _win_prep_kernel,
        grid=(depth, rows // ROW_CHUNK),
        in_specs=[pl.BlockSpec((None, ROW_CHUNK, cols), lambda l, i: (l, i, 0))],
        out_specs=pl.BlockSpec((None, ROW_CHUNK, IN_COLS), lambda l, i: (l, i, 0)),
        out_shape=jax.ShapeDtypeStruct((depth, rows, IN_COLS), jnp.bfloat16),
        compiler_params=_cparams(("arbitrary", "arbitrary")),
        name="w_in_prep",
    )(w)


def _small_rows(f_vals, b_vals):
    row = jnp.concatenate([f_vals, b_vals], axis=-1).astype(jnp.float32)
    return jnp.pad(row, ((0, 0), (SMALL_AF, LANES - SMALL_AF - row.shape[-1])))[:, None, :]


def kernel(x, positions, norm_mix, w_in, conv_a, norm_a, conv_qkv, a_log_f, a_log_b, dt_bias_f, dt_bias_b,
           norm_dn, norm_c, w_o, norm_ffn, w_up, conv_ffn, w_down, norm_final):
    batch, seq, _ = x.shape
    depth = w_in.shape[0]
    x2d = x.reshape(batch * seq, D_MODEL)
    pos_col = positions.reshape(batch * seq, 1)

    half = ROPE_DIM // 2
    inv_freq = ROPE_THETA ** (-jnp.arange(half, dtype=jnp.float32) / half)
    d_idx = np.arange(LANES) % SWA_HEAD_DIM
    invf_row = jnp.where(jnp.asarray(d_idx < ROPE_DIM), inv_freq[d_idx % half], 0.0).reshape(1, LANES)
    rot_np = np.zeros((2 * LANES, 2 * LANES), np.float32)
    for lane_idx in range(2 * LANES):
        if d_idx[lane_idx % LANES] < half:
            rot_np[lane_idx + half, lane_idx] = -1.0
        elif d_idx[lane_idx % LANES] < ROPE_DIM:
            rot_np[lane_idx - half, lane_idx] = 1.0
    rot_mat = jnp.asarray(rot_np, jnp.bfloat16)
    grp = np.arange(CONV_CH) // (CONV_CH // CONV_GROUPS)
    group_mat = _bf(jnp.asarray(grp[:, None] == grp[None, :], jnp.float32))

    rows = lambda a: a[:, None, :]
    w_in_bf, w_o_bf, w_up_bf, w_down_bf = _reorder_w_in(w_in), _bf(w_o), _bf(w_up), _bf(w_down)
    alog_rows, dtb_rows = _small_rows(a_log_f, a_log_b), _small_rows(dt_bias_f, dt_bias_b)
    for l in range(depth):
        proj, y_a = _proj_in(x2d, rows(norm_mix), w_in_bf, conv_a, rows(norm_a), group_mat, l, batch, seq)
        y_b = _deltanet(proj, conv_qkv, alog_rows, dtb_rows, rows(norm_dn), l, batch, seq)
        y_c = _swa(proj, pos_col, invf_row, rot_mat, rows(norm_c), group_mat, l, batch, seq)
        x2d = _out_proj(x2d, y_a, y_b, y_c, w_o_bf, l)
        x2d = _ffn(x2d, rows(norm_ffn), w_up_bf, conv_ffn, w_down_bf, norm_final.reshape(1, D_MODEL), l, seq,
                   l == depth - 1)
    return x2d.reshape(batch, seq, D_MODEL)
```

```python
import functools

import numpy as np
import jax
import jax.numpy as jnp
from jax import lax
from jax.experimental import pallas as pl
from jax.experimental.pallas import tpu as pltpu

D_MODEL = 1024
EPS = 1e-6
NEG_INF = -1e30
CONV_CH = 256
CONV_GROUPS = 4
DN_HEADS = 4
DN_HEAD_DIM = 128
DN_WIDTH = DN_HEADS * DN_HEAD_DIM
SWA_HEAD_DIM = 64
SWA_HEADS = 4
SWA_PATTERNS = ((128, 1), (512, 4), (2048, 16))
SWA_N_PAT = 3
SWA_WIDTH = SWA_HEADS * SWA_HEAD_DIM
SWA_QKV_WIDTH = SWA_N_PAT * SWA_WIDTH
ROPE_THETA = 500000.0
ROPE_DIM = SWA_HEAD_DIM // 4
D_FF = 2816
MIX_WIDTH = CONV_CH + DN_WIDTH + SWA_WIDTH

MIX_DTYPE = jnp.bfloat16
LANES = 128
SUBLANES = 8
VMEM_LIMIT = 56 * 1024 * 1024

OFF_A = 0
MIXA_COLS = 3 * CONV_CH
OFF_DN_QKV = OFF_A + MIXA_COLS
OFF_DN_GATE = OFF_DN_QKV + 3 * DN_WIDTH
OFF_SWA = OFF_DN_GATE + DN_WIDTH
OFF_SMALL = OFF_SWA + 3 * SWA_QKV_WIDTH
IN_COLS = 5376
PROJ_CN = 768
SMALL_BF, SMALL_BB, SMALL_AF, SMALL_AB = 0, 4, 8, 12

DN_CHUNK = 256
DN_INTRA_UNROLL = 4
ROW_CHUNK = 256
HALO = SUBLANES
HALO_BF16 = 2 * SUBLANES
FF_CHUNK = 256
FFN_TM = 1024
OUT_TM = 2048
SWA_QB = 128
SWA_RADIUS = 64
SWA_UNROLL = 8
ROPE_ROWS = 512
SWA_Q_SCALE = SWA_HEAD_DIM ** -0.5 * float(np.log2(np.e))
SWA_ORDER = (2, 1, 0)


def _cparams(sem, vmem_limit=VMEM_LIMIT):
    return pltpu.CompilerParams(dimension_semantics=sem, vmem_limit_bytes=vmem_limit)


def _bf(x):
    return x.astype(jnp.bfloat16)


def _dot(a, b):
    return jnp.dot(a, b, preferred_element_type=jnp.float32)


def _dot_nt(a, b):
    return lax.dot_general(a, b, (((1,), (1,)), ((), ())), preferred_element_type=jnp.float32)


def _silu(x):
    return x * (1.0 / (1.0 + jnp.exp(-x)))


def _conv3_val(ext, w, rows, halo):
    n = ext.shape[0]
    prev = pltpu.roll(ext, 1, 0)[halo:halo + rows]
    nxt = pltpu.roll(ext, n - 1, 0)[halo:halo + rows]
    return prev * w[0:1] + ext[halo:halo + rows] * w[1:2] + nxt * w[2:3]


def _conv3_ref(ref, lead, start, rows, w):
    def tap(off):
        return ref[lead + (pl.ds(start + off, rows), slice(None))]
    return tap(-1) * w[0:1] + tap(0) * w[1:2] + tap(1) * w[2:3]


def _proj_kernel(x_ref, g_ref, w_ref, cw_ref, ng_ref, gm_ref, o_ref, ya_ref, h_ref, a_ref, u_ref):
    col = pl.program_id(1)

    @pl.when(col == 0)
    def _():
        def body(i, c):
            rows = pl.ds(pl.multiple_of(i * ROW_CHUNK, ROW_CHUNK), ROW_CHUNK)
            x = x_ref[rows, :]
            y = x * lax.rsqrt(jnp.mean(x * x, axis=-1, keepdims=True) + EPS)
            h_ref[rows, :] = _bf(y * g_ref[...])
            return c
        lax.fori_loop(0, x_ref.shape[0] // ROW_CHUNK, body, 0)
        a_ref[...] = _dot(h_ref[...], w_ref[...])
        _mixa_body(a_ref, cw_ref, ng_ref, gm_ref, ya_ref, u_ref)

    @pl.when(col > 0)
    def _():
        o_ref[...] = _dot(h_ref[...], w_ref[...])


def _proj_in(x2d, g_row, w_bf, conv_a, norm_a_row, group_mat, layer, batch, seq):
    ncol = IN_COLS // PROJ_CN
    assert PROJ_CN == MIXA_COLS
    return pl.pallas_call(
        _proj_kernel,
        grid=(batch, ncol),
        in_specs=[
            pl.BlockSpec((seq, D_MODEL), lambda b, j: (b, 0)),
            pl.BlockSpec((None, 1, D_MODEL), lambda b, j: (layer, 0, 0)),
            pl.BlockSpec((None, D_MODEL, PROJ_CN), lambda b, j: (layer, 0, j)),
            pl.BlockSpec((None, 3, CONV_CH), lambda b, j: (layer, 0, 0)),
            pl.BlockSpec((None, 1, CONV_CH), lambda b, j: (layer, 0, 0)),
            pl.BlockSpec((CONV_CH, CONV_CH), lambda b, j: (0, 0)),
        ],
        out_specs=[pl.BlockSpec((seq, PROJ_CN), lambda b, j: (b, jnp.maximum(j - 1, 0))),
                   pl.BlockSpec((seq, CONV_CH), lambda b, j: (b, 0))],
        out_shape=[jax.ShapeDtypeStruct((batch * seq, IN_COLS - MIXA_COLS), jnp.float32),
                   jax.ShapeDtypeStruct((batch * seq, CONV_CH), MIX_DTYPE)],
        scratch_shapes=[pltpu.VMEM((seq, D_MODEL), jnp.bfloat16),
                        pltpu.VMEM((seq, MIXA_COLS), jnp.float32),
                        pltpu.VMEM((seq + 2 * HALO, CONV_CH), jnp.float32)],
        compiler_params=_cparams(("arbitrary", "arbitrary")),
        name="proj_in",
    )(x2d, g_row, w_bf, conv_a, norm_a_row, group_mat)


def _mixa_body(p_ref, cw_ref, ng_ref, gm_ref, o_ref, u_ref):
    seq = o_ref.shape[0]
    nchunk = seq // ROW_CHUNK
    zeros = jnp.zeros((HALO, CONV_CH), jnp.float32)
    u_ref[0:HALO, :] = zeros
    u_ref[HALO + seq:HALO + seq + HALO, :] = zeros

    def fill(i, c):
        rows = pl.ds(pl.multiple_of(i * ROW_CHUNK, ROW_CHUNK), ROW_CHUNK)
        xa = p_ref[rows, 0:CONV_CH]
        gc = p_ref[rows, 2 * CONV_CH:3 * CONV_CH]
        u_ref[pl.ds(pl.multiple_of(i * ROW_CHUNK + HALO, HALO), ROW_CHUNK), :] = gc * xa
        return c
    lax.fori_loop(0, nchunk, fill, 0)

    for i in range(nchunk):
        start = i * ROW_CHUNK
        gb = p_ref[start:start + ROW_CHUNK, CONV_CH:2 * CONV_CH]
        y = gb * _conv3_ref(u_ref, (), start + HALO, ROW_CHUNK, cw_ref[...])
        sq = y * y
        hi = _bf(sq)
        lo = _bf(sq - hi.astype(jnp.float32))
        ms = (_dot(hi, gm_ref[...]) + _dot(lo, gm_ref[...])) * (1.0 / (CONV_CH // CONV_GROUPS))
        o_ref[start:start + ROW_CHUNK, :] = _bf(y * lax.rsqrt(ms + EPS) * ng_ref[...])


def _tri_inverse_group(a_list, lowers, xor_ij, eye):
    n = a_list[0].shape[0]
    d_list = [eye - jnp.where(xor_ij == 1, a, 0.0) for a in a_list]
    level = 1
    while (1 << level) < n:
        s = 1 << level
        mask = (xor_ij >> level) == 1
        e_list = [_bf(jnp.where(mask, a, 0.0)) for a in a_list]
        d_bf = [_bf(d) for d in d_list]
        if s < SUBLANES:
            de_list = [_dot(db, e) for db, e in zip(d_bf, e_list)]
            d_list = [d - _dot(_bf(de), db) for d, de, db in zip(d_list, de_list, d_bf)]
        else:
            blocks = [[d[b * s:(b + 1) * s] for b in range(n // s)] for d in d_list]
            picks = [range(1, n // s, 2) if lower else range(0, n // s, 2) for lower in lowers]
            sel = [_bf(jnp.concatenate([blk[b] for b in pk], axis=0)) for blk, pk in zip(blocks, picks)]
            de_list = [_dot(x, e) for x, e in zip(sel, e_list)]
            upd = [_dot(_bf(de), db) for de, db in zip(de_list, d_bf)]
            d_list = []
            for blk, pk, u in zip(blocks, picks, upd):
                for k, b in enumerate(pk):
                    blk[b] = blk[b] - u[k * s:(k + 1) * s]
                d_list.append(jnp.concatenate(blk, axis=0))
        level += 1
    return d_list


def _dn_intra_group(items, ii, jj, xor_ij, eye):
    c = items[0][0].shape[0]
    wide = lambda col: jnp.concatenate([col] * (c // LANES), axis=1)
    grams = []
    for q, k in [(item[0], item[1]) for item in items[::2]]:
        k_bf = _bf(k)
        grams.append((_dot_nt(k_bf, k_bf), _dot_nt(_bf(q), k_bf)))
    pre = []
    for idx, (q, k, v, beta, gcol, grow, lower) in enumerate(items):
        if lower:
            incl, strict, glast = ii >= jj, ii > jj, gcol[c - 1:c, :]
        else:
            incl, strict, glast = ii <= jj, ii < jj, gcol[0:1, :]
        kk, qk_raw = grams[idx // 2]
        dec = jnp.where(incl, jnp.exp(jnp.where(incl, wide(gcol) - grow, 0.0)), 0.0)
        a = jnp.where(strict, kk * wide(beta) * dec, 0.0)
        pre.append((a, _bf(qk_raw * dec), k * beta, glast))
    lhs_list = [jnp.concatenate([_bf((item[1] * jnp.exp(p[3] - item[4])).T), p[1]], axis=0)
                for item, p in zip(items, pre)]
    tinv_list = _tri_inverse_group([p[0] for p in pre], [item[6] for item in items], xor_ij, eye)
    hd = DN_HEAD_DIM
    eg_list = [jnp.exp(item[4]) for item in items]
    sol_list = [_bf(_dot(_bf(tinv), _bf(jnp.concatenate([item[2] * item[3], p[2] * eg], axis=1))))
                for item, p, eg, tinv in zip(items, pre, eg_list, tinv_list)]
    r_list = [_dot(lhs, sol) for lhs, sol in zip(lhs_list, sol_list)]
    return [(r[:hd, :hd], _bf(r[:hd, hd:]), r[hd:, :hd], _bf(item[0] * eg - r[hd:, hd:]), jnp.exp(p[3]))
            for item, p, eg, r in zip(items, pre, eg_list, r_list)]


def _dn_kernel(q_ref, k_ref, v_ref, gate_ref, small_ref, cq_ref, ck_ref, cv_ref, alog_ref, dtb_ref,
               ng_ref, o_ref,
               pad_ref, qn_ref, kn_ref, vn_ref, bf_ref, bb_ref, gf_ref, gb_ref, rf_ref, rb_ref,
               of_ref, ob_ref, s_ref, b_ref, p_ref, qp_ref, gl_ref, beta_ref, cf_ref, cb_ref):
    seq = o_ref.shape[0]
    head = pl.program_id(1)
    nrow = seq // ROW_CHUNK
    nchunk = seq // DN_CHUNK

    zeros = jnp.zeros((HALO, DN_HEAD_DIM), jnp.float32)
    pad_ref[0:HALO, :] = zeros
    pad_ref[HALO + seq:HALO + seq + HALO, :] = zeros

    def conv_silu(src_ref, cw_ref, dst_ref, mode):
        def fill(i, c):
            rows = pl.ds(pl.multiple_of(i * ROW_CHUNK, ROW_CHUNK), ROW_CHUNK)
            pad_ref[pl.ds(pl.multiple_of(i * ROW_CHUNK + HALO, HALO), ROW_CHUNK), :] = src_ref[rows, :]
            return c
        lax.fori_loop(0, nrow, fill, 0)

        for i in range(nrow):
            start = i * ROW_CHUNK
            y = _silu(_conv3_ref(pad_ref, (), start + HALO, ROW_CHUNK, cw_ref[...]))
            if mode != "v":
                y = y * lax.rsqrt(jnp.sum(y * y, axis=-1, keepdims=True) + EPS)
            if mode == "q":
                y = y * (DN_HEAD_DIM ** -0.5)
            dst_ref[start:start + ROW_CHUNK, :] = y

    conv_silu(q_ref, cq_ref, qn_ref, "q")
    conv_silu(k_ref, ck_ref, kn_ref, "k")
    conv_silu(v_ref, cv_ref, vn_ref, "v")

    lane = lax.broadcasted_iota(jnp.int32, (DN_CHUNK, LANES), 1)
    ii = lax.broadcasted_iota(jnp.int32, (DN_CHUNK, DN_CHUNK), 0)
    jj = lax.broadcasted_iota(jnp.int32, (DN_CHUNK, DN_CHUNK), 1)
    tri2 = jnp.concatenate([jnp.where(ii >= jj, 1.0, 0.0), jnp.where(ii <= jj, 1.0, 0.0)],
                           axis=0).astype(jnp.bfloat16)

    def pick(x, col):
        sel = jnp.sum(jnp.where(lane == col, x, 0.0), axis=-1, keepdims=True)
        return jnp.broadcast_to(sel, x.shape)

    @pl.when(head == 0)
    def _():
        neg_a = -jnp.exp(alog_ref[...])

        def gates(i, c):
            cis = [2 * i, 2 * i + 1]
            rows = [pl.ds(pl.multiple_of(ci * DN_CHUNK, DN_CHUNK), DN_CHUNK) for ci in cis]
            sm = [small_ref[r, :] for r in rows]
            z = [s + dtb_ref[...] for s in sm]
            g = [neg_a * (jnp.maximum(zz, 0.0) + jnp.log1p(jnp.exp(-jnp.abs(zz)))) for zz in z]
            g1 = [_bf(x) for x in g]
            r1 = [x - x1.astype(jnp.float32) for x, x1 in zip(g, g1)]
            g2 = [_bf(x) for x in r1]
            g3 = [_bf(x - x2.astype(jnp.float32)) for x, x2 in zip(r1, g2)]
            r = [_dot(tri2, jnp.concatenate(parts, axis=1)) for parts in zip(g1, g2, g3)]
            cf = [x[:DN_CHUNK, 2 * LANES:] + x[:DN_CHUNK, LANES:2 * LANES] + x[:DN_CHUNK, :LANES] for x in r]
            cb = [x[DN_CHUNK:, 2 * LANES:] + x[DN_CHUNK:, LANES:2 * LANES] + x[DN_CHUNK:, :LANES] for x in r]
            for k in range(2):
                beta_ref[rows[k], :] = 1.0 / (1.0 + jnp.exp(-sm[k]))
                cf_ref[rows[k], :] = cf[k]
                cb_ref[rows[k], :] = cb[k]
                rf_ref[cis[k]] = cf[k].T
                rb_ref[cis[k]] = cb[k].T
            return c
        lax.fori_loop(0, nchunk // 2, gates, 0)

    def picks(ci, c):
        rows = pl.ds(pl.multiple_of(ci * DN_CHUNK, DN_CHUNK), DN_CHUNK)
        beta = beta_ref[rows, :]
        bf_ref[rows, :] = pick(beta, SMALL_BF + head)
        bb_ref[rows, :] = pick(beta, SMALL_BB + head)
        gf_ref[rows, :] = pick(cf_ref[rows, :], SMALL_AF + head)
        gb_ref[rows, :] = pick(cb_ref[rows, :], SMALL_AB + head)
        return c
    lax.fori_loop(0, nchunk, picks, 0)

    xor_ij = ii ^ jj
    eye = jnp.where(ii == jj, 1.0, 0.0)

    def intra(i, c):
        items, where = [], []
        for sub in range(DN_INTRA_UNROLL):
            ci = i * DN_INTRA_UNROLL + sub
            rows = pl.ds(pl.multiple_of(ci * DN_CHUNK, DN_CHUNK), DN_CHUNK)
            for direction in (0, 1):
                if direction == 0:
                    beta, gcol = bf_ref[rows, :], gf_ref[rows, :]
                    grow = rf_ref[ci, pl.ds(SMALL_AF + head, 1), :]
                else:
                    beta, gcol = bb_ref[rows, :], gb_ref[rows, :]
                    grow = rb_ref[ci, pl.ds(SMALL_AB + head, 1), :]
                items.append((qn_ref[rows, :], kn_ref[rows, :], vn_ref[rows, :], beta, gcol, grow, direction == 0))
                where.append((direction, ci, rows))
        outs = (of_ref, ob_ref)
        for (direction, ci, rows), (b, p, o_part, qp, gl) in zip(where, _dn_intra_group(items, ii, jj, xor_ij, eye)):
            b_ref[direction, ci] = b
            p_ref[direction, ci] = p
            outs[direction][rows, :] = o_part
            qp_ref[direction, rows, :] = qp
            gl_ref[direction, ci] = jnp.broadcast_to(gl, (SUBLANES, LANES))
        return c
    lax.fori_loop(0, nchunk // DN_INTRA_UNROLL, intra, 0)

    s_ref[...] = jnp.zeros_like(s_ref)

    def scan(i, c):
        cis = (i, nchunk - 1 - i)
        rows = [pl.ds(pl.multiple_of(ci * DN_CHUNK, DN_CHUNK), DN_CHUNK) for ci in cis]
        s_old = [s_ref[d] for d in (0, 1)]
        s_bf = [_bf(s) for s in s_old]
        s_new = [s_old[d] * gl_ref[d, cis[d], 0:1, :] + b_ref[d, cis[d]] - _dot(p_ref[d, cis[d]], s_bf[d])
                 for d in (0, 1)]
        o_add = [_dot(qp_ref[d, rows[d], :], s_bf[d]) for d in (0, 1)]
        for d, out_ref in ((0, of_ref), (1, ob_ref)):
            s_ref[d] = s_new[d]
            out_ref[rows[d], :] += o_add[d]
        return c
    lax.fori_loop(0, nchunk, scan, 0)

    def fin(i, c):
        rows = pl.ds(pl.multiple_of(i * ROW_CHUNK, ROW_CHUNK), ROW_CHUNK)
        o = of_ref[rows, :] + ob_ref[rows, :]
        o = o * lax.rsqrt(jnp.mean(o * o, axis=-1, keepdims=True) + EPS) * ng_ref[...]
        o_ref[rows, :] = _bf(o * _silu(gate_ref[rows, :]))
        return c
    lax.fori_loop(0, nrow, fin, 0)


def _deltanet(proj, conv_qkv, alog_row, dtb_row, norm_dn_row, layer, batch, seq):
    nchunk = seq // DN_CHUNK
    hd = DN_HEAD_DIM
    qb, gb, sb = ((off - MIXA_COLS) // hd for off in (OFF_DN_QKV, OFF_DN_GATE, OFF_SMALL))
    seq_buf = pltpu.VMEM((seq, hd), jnp.float32)
    return pl.pallas_call(
        _dn_kernel,
        grid=(batch, DN_HEADS),
        in_specs=[
            pl.BlockSpec((seq, hd), lambda b, h: (b, qb + h)),
            pl.BlockSpec((seq, hd), lambda b, h: (b, qb + DN_HEADS + h)),
            pl.BlockSpec((seq, hd), lambda b, h: (b, qb + 2 * DN_HEADS + h)),
            pl.BlockSpec((seq, hd), lambda b, h: (b, gb + h)),
            pl.BlockSpec((seq, LANES), lambda b, h: (b, sb)),
            pl.BlockSpec((None, 3, hd), lambda b, h: (layer, 0, h)),
            pl.BlockSpec((None, 3, hd), lambda b, h: (layer, 0, DN_HEADS + h)),
            pl.BlockSpec((None, 3, hd), lambda b, h: (layer, 0, 2 * DN_HEADS + h)),
            pl.BlockSpec((None, 1, LANES), lambda b, h: (layer, 0, 0)),
            pl.BlockSpec((None, 1, LANES), lambda b, h: (layer, 0, 0)),
            pl.BlockSpec((None, 1, hd), lambda b, h: (layer, 0, 0)),
        ],
        out_specs=pl.BlockSpec((seq, hd), lambda b, h: (b, h)),
        out_shape=jax.ShapeDtypeStruct((batch * seq, DN_WIDTH), MIX_DTYPE),
        scratch_shapes=[
            pltpu.VMEM((seq + 2 * HALO, hd), jnp.float32),
            seq_buf, seq_buf, seq_buf,
            seq_buf, seq_buf, seq_buf, seq_buf,
            pltpu.VMEM((nchunk, LANES, DN_CHUNK), jnp.float32),
            pltpu.VMEM((nchunk, LANES, DN_CHUNK), jnp.float32),
            seq_buf, seq_buf,
            pltpu.VMEM((2, hd, hd), jnp.float32),
            pltpu.VMEM((2, nchunk, hd, hd), jnp.float32),
            pltpu.VMEM((2, nchunk, hd, hd), jnp.bfloat16),
            pltpu.VMEM((2, seq, hd), jnp.bfloat16),
            pltpu.VMEM((2, nchunk, SUBLANES, LANES), jnp.float32),
            seq_buf, seq_buf, seq_buf,
        ],
        compiler_params=_cparams(("arbitrary", "arbitrary")),
        name="deltanet",
    )(proj, proj, proj, proj, proj, conv_qkv, conv_qkv, conv_qkv, alog_row, dtb_row, norm_dn_row)


def _rope_table_kernel(pos_ref, invf_ref, cos_ref, sin_ref):
    def body(i, c):
        rows = pl.ds(pl.multiple_of(i * ROW_CHUNK, ROW_CHUNK), ROW_CHUNK)
        ang = pos_ref[rows, :].astype(jnp.float32) * invf_ref[...]
        cos_ref[rows, :] = jnp.cos(ang)
        sin_ref[rows, :] = jnp.sin(ang)
        return c
    lax.fori_loop(0, cos_ref.shape[0] // ROW_CHUNK, body, 0)


def _rope_tables(pos_col, invf_row, batch, seq):
    table = jax.ShapeDtypeStruct((batch * seq, LANES), jnp.float32)
    return pl.pallas_call(
        _rope_table_kernel,
        grid=(batch,),
        in_specs=[pl.BlockSpec((seq, 1), lambda b: (b, 0)), pl.BlockSpec((1, LANES), lambda b: (0, 0))],
        out_specs=[pl.BlockSpec((seq, LANES), lambda b: (b, 0))] * 2,
        out_shape=[table, table],
        compiler_params=_cparams(("arbitrary",)),
        name="rope_tables",
    )(pos_col, invf_row)


def _swa_kernel(q_ref, k_ref, v_ref, cos_ref, sin_ref, rot_ref, ng_ref, gm_ref, o_ref,
                qr_ref, kr_ref, m_ref, a_ref, bias_ref):
    seq = o_ref.shape[0]
    step = pl.program_id(2)
    nrow = seq // ROW_CHUNK

    def rope(i, c):
        rows = pl.ds(pl.multiple_of(i * ROPE_ROWS, ROPE_ROWS), ROPE_ROWS)
        cs, sn = cos_ref[rows, :], sin_ref[rows, :]
        x = jnp.concatenate([q_ref[rows, :], k_ref[rows, :]], axis=1)
        xr = _dot(_bf(x), rot_ref[...])
        y = x * jnp.concatenate([cs, cs], axis=1) + xr * jnp.concatenate([sn, sn], axis=1)
        qr_ref[rows, :] = y[:, :LANES] * SWA_Q_SCALE
        kr_ref[rows, :] = y[:, LANES:]
        return c
    lax.fori_loop(0, seq // ROPE_ROWS, rope, 0)

    def run_pattern(dil, first):
        sub_len = seq // dil
        nblk = sub_len // SWA_QB
        win = min(sub_len, SWA_QB + 2 * SWA_RADIUS)
        lane_q = lax.broadcasted_iota(jnp.int32, (SWA_QB, LANES), 1)
        head_lo = lane_q < SWA_HEAD_DIM
        lane_w = lax.broadcasted_iota(jnp.int32, (win, LANES), 1)
        own_v = (lane_w < SWA_HEAD_DIM, lane_w >= SWA_HEAD_DIM)
        dij = (lax.broadcasted_iota(jnp.int32, (SWA_QB, win), 0)
               - lax.broadcasted_iota(jnp.int32, (SWA_QB, win), 1))
        for k in range(3 if sub_len > win else 1):
            bias_ref[k, :, 0:win] = jnp.where(jnp.abs(dij + k * SWA_RADIUS) <= SWA_RADIUS, 0.0, NEG_INF)

        def sl(start, size):
            return pl.ds(start, size) if dil == 1 else pl.ds(start, size, stride=dil)

        def body(i, c):
            blocks = []
            for sub in range(SWA_UNROLL):
                it = i * SWA_UNROLL + sub
                res = it // nblk
                qs = (it % nblk) * SWA_QB
                ws = jnp.clip(qs - SWA_RADIUS, 0, sub_len - win)
                rq = sl(res + qs * dil, SWA_QB)
                rk = sl(res + ws * dil, win)
                bias = bias_ref[(qs - ws) // SWA_RADIUS, :, 0:win]
                blocks.append((rq, qr_ref[rq, :], _bf(kr_ref[rk, :]), v_ref[rk, :], bias))
            heads = [(b, hh) for b in range(SWA_UNROLL) for hh in (0, 1)]
            own = (head_lo, jnp.logical_not(head_lo))
            s_all = [_dot_nt(_bf(jnp.where(own[hh], blocks[b][1], 0.0)), blocks[b][2]) + blocks[b][4]
                     for b, hh in heads]
            v_ext = [_bf(jnp.where(own_v[hh], blocks[b][3], 1.0)) for b, hh in heads]
            m_cur = [jnp.broadcast_to(jnp.max(s, axis=-1, keepdims=True), (SWA_QB, LANES)) for s in s_all]
            if first:
                m_new = m_cur
            else:
                m_old = [m_ref[hh, blocks[b][0], :] for b, hh in heads]
                m_new = [jnp.maximum(m_o, m_c) for m_o, m_c in zip(m_old, m_cur)]
            p_all = [jnp.exp2(s - jnp.concatenate([m_n] * (win // LANES), axis=1)) for s, m_n in zip(s_all, m_new)]
            pv_all = [_dot(_bf(p), v) for p, v in zip(p_all, v_ext)]
            for idx, (b, hh) in enumerate(heads):
                rq = blocks[b][0]
                if first:
                    a_ref[hh, rq, :] = pv_all[idx]
                else:
                    a_ref[hh, rq, :] = jnp.exp2(m_old[idx] - m_new[idx]) * a_ref[hh, rq, :] + pv_all[idx]
                m_ref[hh, rq, :] = m_new[idx]
            return c
        lax.fori_loop(0, dil * nblk // SWA_UNROLL, body, 0)

    for s_idx, p_idx in enumerate(SWA_ORDER):
        pl.when(step == s_idx)(functools.partial(run_pattern, SWA_PATTERNS[p_idx][1], s_idx == 0))

    @pl.when(step == SWA_N_PAT - 1)
    def _():
        lo = lax.broadcasted_iota(jnp.int32, (ROW_CHUNK, LANES), 1) < SWA_HEAD_DIM

        def fin(i, c):
            rows = pl.ds(pl.multiple_of(i * ROW_CHUNK, ROW_CHUNK), ROW_CHUNK)
            a0, a1 = a_ref[0, rows, :], a_ref[1, rows, :]
            o = jnp.where(lo, a0, a1) / pltpu.roll(jnp.where(lo, a1, a0), SWA_HEAD_DIM, 1)
            sq = o * o
            hi = _bf(sq)
            rest = _bf(sq - hi.astype(jnp.float32))
            ms = (_dot(hi, gm_ref[...]) + _dot(rest, gm_ref[...])) * (1.0 / SWA_HEAD_DIM)
            o_ref[rows, :] = _bf(o * lax.rsqrt(ms + EPS) * ng_ref[...])
            return c
        lax.fori_loop(0, nrow, fin, 0)


def _swa(proj, cos_tab, sin_tab, rot_mat, norm_c_row, group_mat, layer, batch, seq):
    base = (OFF_SWA - MIXA_COLS) // LANES
    per_pat = SWA_WIDTH // LANES
    qkv = SWA_QKV_WIDTH // LANES
    pairs = SWA_HEADS // 2
    seq_buf = pltpu.VMEM((seq, LANES), jnp.float32)
    first, stride = SWA_ORDER[0], SWA_ORDER[1] - SWA_ORDER[0]
    assert tuple(first + p * stride for p in range(SWA_N_PAT)) == SWA_ORDER
    assert CONV_CH // CONV_GROUPS == SWA_HEAD_DIM

    def spec(which):
        return pl.BlockSpec((seq, LANES),
                            lambda b, hp, p: (b, base + which * qkv + (first + p * stride) * per_pat + hp))

    return pl.pallas_call(
        _swa_kernel,
        grid=(batch, pairs, SWA_N_PAT),
        in_specs=[
            spec(0), spec(1), spec(2),
            pl.BlockSpec((seq, LANES), lambda b, hp, p: (b, 0)),
            pl.BlockSpec((seq, LANES), lambda b, hp, p: (b, 0)),
            pl.BlockSpec((2 * LANES, 2 * LANES), lambda b, hp, p: (0, 0)),
            pl.BlockSpec((None, 1, LANES), lambda b, hp, p: (layer, 0, hp)),
            pl.BlockSpec((LANES, LANES), lambda b, hp, p: (0, 0)),
        ],
        out_specs=pl.BlockSpec((seq, LANES), lambda b, hp, p: (b, hp)),
        out_shape=jax.ShapeDtypeStruct((batch * seq, SWA_WIDTH), MIX_DTYPE),
        scratch_shapes=([seq_buf] * 2 + [pltpu.VMEM((2, seq, LANES), jnp.float32)] * 2
                        + [pltpu.VMEM((3, SWA_QB, SWA_QB + 2 * SWA_RADIUS), jnp.float32)]),
        compiler_params=_cparams(("arbitrary", "arbitrary", "arbitrary")),
        name="swa",
    )(proj, proj, proj, cos_tab, sin_tab, rot_mat, norm_c_row, group_mat)


def _outproj_kernel(x_ref, ya_ref, yb_ref, yc_ref, w_ref, o_ref):
    a0, a1, a2 = CONV_CH, CONV_CH + DN_WIDTH, MIX_WIDTH
    acc = _dot(ya_ref[...], w_ref[0:a0, :])
    acc = acc + _dot(yb_ref[...], w_ref[a0:a1, :])
    acc = acc + _dot(yc_ref[...], w_ref[a1:a2, :])
    o_ref[...] = x_ref[...] + acc


def _out_proj(x2d, y_a, y_b, y_c, w_bf, layer):
    n = x2d.shape[0]
    tm = OUT_TM
    return pl.pallas_call(
        _outproj_kernel,
        grid=(n // tm,),
        in_specs=[
            pl.BlockSpec((tm, D_MODEL), lambda i: (i, 0)),
            pl.BlockSpec((tm, CONV_CH), lambda i: (i, 0)),
            pl.BlockSpec((tm, DN_WIDTH), lambda i: (i, 0)),
            pl.BlockSpec((tm, SWA_WIDTH), lambda i: (i, 0)),
            pl.BlockSpec((None, MIX_WIDTH, D_MODEL), lambda i: (layer, 0, 0)),
        ],
        out_specs=pl.BlockSpec((tm, D_MODEL), lambda i: (i, 0)),
        out_shape=jax.ShapeDtypeStruct((n, D_MODEL), jnp.float32),
        compiler_params=_cparams(("arbitrary",)),
        name="out_proj",
    )(x2d, y_a, y_b, y_c, w_bf)


def _ffn_kernel(xp_ref, x_ref, xn_ref, g_ref, wu_ref, cf_ref, wd_ref, gf_ref, o_ref,
                h_ref, a_ref, *, tiles_per_seq, final_norm):
    tm = x_ref.shape[0]
    t = pl.program_id(0) % tiles_per_seq

    def norm(x):
        return x * lax.rsqrt(jnp.mean(x * x, axis=-1, keepdims=True) + EPS) * g_ref[...]

    halo = HALO_BF16
    hp = norm(xp_ref[...])
    hn = norm(xn_ref[...])
    h_ref[0:halo, :] = _bf(jnp.where(t == 0, 0.0, hp))
    h_ref[halo + tm:halo + tm + halo, :] = _bf(jnp.where(t == tiles_per_seq - 1, 0.0, hn))

    def fill(i, c):
        rows = pl.ds(pl.multiple_of(i * ROW_CHUNK, ROW_CHUNK), ROW_CHUNK)
        h_ref[pl.ds(pl.multiple_of(i * ROW_CHUNK + halo, halo), ROW_CHUNK), :] = _bf(norm(x_ref[rows, :]))
        return c
    lax.fori_loop(0, tm // ROW_CHUNK, fill, 0)

    for j in range(D_FF // FF_CHUNK):
        h = h_ref[...]
        gcols = slice(j * FF_CHUNK, (j + 1) * FF_CHUNK)
        vcols = slice(D_FF + j * FF_CHUNK, D_FF + (j + 1) * FF_CHUNK)
        gate = _conv3_val(_dot(h, wu_ref[:, gcols]), cf_ref[:, gcols], tm, halo)
        val = _conv3_val(_dot(h, wu_ref[:, vcols]), cf_ref[:, vcols], tm, halo)
        a_ref[:, gcols] = _bf(_silu(gate) * val)
    o_ref[...] = _dot(a_ref[...], wd_ref[...])

    def fin(i, c):
        rows = pl.ds(pl.multiple_of(i * ROW_CHUNK, ROW_CHUNK), ROW_CHUNK)
        y = x_ref[rows, :] + o_ref[rows, :]
        if final_norm:
            y = y * lax.rsqrt(jnp.mean(y * y, axis=-1, keepdims=True) + EPS) * gf_ref[...]
        o_ref[rows, :] = y
        return c
    lax.fori_loop(0, tm // ROW_CHUNK, fin, 0)


def _ffn(x2d, g_row, wu, cf, wd, gf_row, layer, seq, final_norm):
    n = x2d.shape[0]
    tm = FFN_TM
    tiles_per_seq = seq // tm
    halo = HALO_BF16
    hb = tm // halo
    nhb = n // halo
    whole = lambda shape: pl.BlockSpec((None,) + shape, lambda i: (layer,) + (0,) * len(shape),
                                       pipeline_mode=pl.Buffered(1))
    kern = functools.partial(_ffn_kernel, tiles_per_seq=tiles_per_seq, final_norm=final_norm)
    return pl.pallas_call(
        kern,
        grid=(n // tm,),
        in_specs=[
            pl.BlockSpec((halo, D_MODEL), lambda i: (jnp.maximum(i * hb - 1, 0), 0)),
            pl.BlockSpec((tm, D_MODEL), lambda i: (i, 0)),
            pl.BlockSpec((halo, D_MODEL), lambda i: (jnp.minimum((i + 1) * hb, nhb - 1), 0)),
            whole((1, D_MODEL)),
            whole((D_MODEL, 2 * D_FF)),
            whole((3, 2 * D_FF)),
            whole((D_FF, D_MODEL)),
            pl.BlockSpec((1, D_MODEL), lambda i: (0, 0)),
        ],
        out_specs=pl.BlockSpec((tm, D_MODEL), lambda i: (i, 0)),
        out_shape=jax.ShapeDtypeStruct((n, D_MODEL), jnp.float32),
        scratch_shapes=[
            pltpu.VMEM((tm + 2 * halo, D_MODEL), jnp.bfloat16),
            pltpu.VMEM((tm, D_FF), jnp.bfloat16),
        ],
        compiler_params=_cparams(("arbitrary",)),
        name="ffn",
    )(x2d, x2d, x2d, g_row, wu, cf, wd, gf_row)


def _win_prep_kernel(w_ref, o_ref):
    n_small = 4 * DN_HEADS
    n_swa = 3 * SWA_QKV_WIDTH
    o_ref[:, 0:OFF_SWA] = _bf(w_ref[:, 0:OFF_SWA])
    o_ref[:, OFF_SWA:OFF_SMALL] = _bf(w_ref[:, OFF_SWA + n_small:OFF_SWA + n_small + n_swa])
    o_ref[:, OFF_SMALL:OFF_SMALL + LANES] = _bf(jnp.concatenate(
        [w_ref[:, OFF_SWA:OFF_SWA + n_small], jnp.zeros((w_ref.shape[0], LANES - n_small), jnp.float32)], axis=1))
    o_ref[:, OFF_SMALL + LANES:] = jnp.zeros((w_ref.shape[0], IN_COLS - OFF_SMALL - LANES), jnp.bfloat16)


def _reorder_w_in(w):
    depth, rows, cols = w.shape
    return pl.pallas_call(
        _win_prep_kernel,
        grid=(depth, rows // ROW_CHUNK),
        in_specs=[pl.BlockSpec((None, ROW_CHUNK, cols), lambda l, i: (l, i, 0))],
        out_specs=pl.BlockSpec((None, ROW_CHUNK, IN_COLS), lambda l, i: (l, i, 0)),
        out_shape=jax.ShapeDtypeStruct((depth, rows, IN_COLS), jnp.bfloat16),
        compiler_params=_cparams(("arbitrary", "arbitrary")),
        name="w_in_prep",
    )(w)


def _small_rows(f_vals, b_vals):
    row = jnp.concatenate([f_vals, b_vals], axis=-1).astype(jnp.float32)
    return jnp.pad(row, ((0, 0), (SMALL_AF, LANES - SMALL_AF - row.shape[-1])))[:, None, :]


def kernel(x, positions, norm_mix, w_in, conv_a, norm_a, conv_qkv, a_log_f, a_log_b, dt_bias_f, dt_bias_b,
           norm_dn, norm_c, w_o, norm_ffn, w_up, conv_ffn, w_down, norm_final):
    batch, seq, _ = x.shape
    depth = w_in.shape[0]
    x2d = x.reshape(batch * seq, D_MODEL)
    pos_col = positions.reshape(batch * seq, 1)

    half = ROPE_DIM // 2
    inv_freq = ROPE_THETA ** (-jnp.arange(half, dtype=jnp.float32) / half)
    d_idx = np.arange(LANES) % SWA_HEAD_DIM
    invf_row = jnp.where(jnp.asarray(d_idx < ROPE_DIM), inv_freq[d_idx % half], 0.0).reshape(1, LANES)
    rot_np = np.zeros((2 * LANES, 2 * LANES), np.float32)
    for lane_idx in range(2 * LANES):
        if d_idx[lane_idx % LANES] < half:
            rot_np[lane_idx + half, lane_idx] = -1.0
        elif d_idx[lane_idx % LANES] < ROPE_DIM:
            rot_np[lane_idx - half, lane_idx] = 1.0
    rot_mat = jnp.asarray(rot_np, jnp.bfloat16)
    grp = np.arange(CONV_CH) // (CONV_CH // CONV_GROUPS)
    group_mat = _bf(jnp.asarray(grp[:, None] == grp[None, :], jnp.float32))

    rows = lambda a: a[:, None, :]
    w_in_bf, w_o_bf, w_up_bf, w_down_bf = _reorder_w_in(w_in), _bf(w_o), _bf(w_up), _bf(w_down)
    alog_rows, dtb_rows = _small_rows(a_log_f, a_log_b), _small_rows(dt_bias_f, dt_bias_b)
    cos_tab, sin_tab = _rope_tables(pos_col, invf_row, batch, seq)
    for l in range(depth):
        proj, y_a = _proj_in(x2d, rows(norm_mix), w_in_bf, conv_a, rows(norm_a), group_mat, l, batch, seq)
        y_b = _deltanet(proj, conv_qkv, alog_rows, dtb_rows, rows(norm_dn), l, batch, seq)
        y_c = _swa(proj, cos_tab, sin_tab, rot_mat, rows(norm_c), group_mat, l, batch, seq)
        x2d = _out_proj(x2d, y_a, y_b, y_c, w_o_bf, l)
        x2d = _ffn(x2d, rows(norm_ffn), w_up_bf, conv_ffn, w_down_bf, norm_final.reshape(1, D_MODEL), l, seq,
                   l == depth - 1)
    return x2d.reshape(batch, seq, D_MODEL)
```

```python
import functools

import numpy as np
import jax
import jax.numpy as jnp
from jax import lax
from jax.experimental import pallas as pl
from jax.experimental.pallas import tpu as pltpu

D_MODEL = 1024
EPS = 1e-6
NEG_INF = -1e30
CONV_CH = 256
CONV_GROUPS = 4
DN_HEADS = 4
DN_HEAD_DIM = 128
DN_WIDTH = DN_HEADS * DN_HEAD_DIM
SWA_HEAD_DIM = 64
SWA_HEADS = 4
SWA_PATTERNS = ((128, 1), (512, 4), (2048, 16))
SWA_N_PAT = 3
SWA_WIDTH = SWA_HEADS * SWA_HEAD_DIM
SWA_QKV_WIDTH = SWA_N_PAT * SWA_WIDTH
ROPE_THETA = 500000.0
ROPE_DIM = SWA_HEAD_DIM // 4
D_FF = 2816
MIX_WIDTH = CONV_CH + DN_WIDTH + SWA_WIDTH

MIX_DTYPE = jnp.bfloat16
LANES = 128
SUBLANES = 8
VMEM_LIMIT = 56 * 1024 * 1024

OFF_A = 0
MIXA_COLS = 3 * CONV_CH
OFF_DN_QKV = OFF_A + MIXA_COLS
OFF_DN_GATE = OFF_DN_QKV + 3 * DN_WIDTH
OFF_SWA = OFF_DN_GATE + DN_WIDTH
OFF_SMALL = OFF_SWA + 3 * SWA_QKV_WIDTH
IN_COLS = 5376
PROJ_CN = 768
SMALL_BF, SMALL_BB, SMALL_AF, SMALL_AB = 0, 4, 8, 12

DN_CHUNK = 256
DN_INTRA_UNROLL = 4
ROW_CHUNK = 256
HALO = SUBLANES
HALO_BF16 = 2 * SUBLANES
FF_CHUNK = 256
FFN_TM = 1024
OUT_TM = 2048
SWA_QB = 128
SWA_RADIUS = 64
SWA_UNROLL = 8
ROPE_ROWS = 512
SWA_Q_SCALE = SWA_HEAD_DIM ** -0.5 * float(np.log2(np.e))
SWA_ORDER = (2, 1, 0)


def _cparams(sem, vmem_limit=VMEM_LIMIT):
    return pltpu.CompilerParams(dimension_semantics=sem, vmem_limit_bytes=vmem_limit)


def _bf(x):
    return x.astype(jnp.bfloat16)


def _dot(a, b):
    return jnp.dot(a, b, preferred_element_type=jnp.float32)


def _dot_nt(a, b):
    return lax.dot_general(a, b, (((1,), (1,)), ((), ())), preferred_element_type=jnp.float32)


def _silu(x):
    return x * (1.0 / (1.0 + jnp.exp(-x)))


def _conv3_val(ext, w, rows, halo):
    n = ext.shape[0]
    prev = pltpu.roll(ext, 1, 0)[halo:halo + rows]
    nxt = pltpu.roll(ext, n - 1, 0)[halo:halo + rows]
    return prev * w[0:1] + ext[halo:halo + rows] * w[1:2] + nxt * w[2:3]


def _conv3_ref(ref, lead, start, rows, w):
    def tap(off):
        return ref[lead + (pl.ds(start + off, rows), slice(None))]
    return tap(-1) * w[0:1] + tap(0) * w[1:2] + tap(1) * w[2:3]


def _proj_kernel(x_ref, g_ref, w_ref, cw_ref, ng_ref, gm_ref, o_ref, ya_ref, h_ref, a_ref, u_ref):
    col = pl.program_id(1)

    @pl.when(col == 0)
    def _():
        def body(i, c):
            rows = pl.ds(pl.multiple_of(i * ROW_CHUNK, ROW_CHUNK), ROW_CHUNK)
            x = x_ref[rows, :]
            y = x * lax.rsqrt(jnp.mean(x * x, axis=-1, keepdims=True) + EPS)
            h_ref[rows, :] = _bf(y * g_ref[...])
            return c
        lax.fori_loop(0, x_ref.shape[0] // ROW_CHUNK, body, 0)
        a_ref[...] = _dot(h_ref[...], w_ref[...])
        _mixa_body(a_ref, cw_ref, ng_ref, gm_ref, ya_ref, u_ref)

    @pl.when(col > 0)
    def _():
        o_ref[...] = _dot(h_ref[...], w_ref[...])


def _proj_in(x2d, g_row, w_bf, conv_a, norm_a_row, group_mat, layer, batch, seq):
    ncol = IN_COLS // PROJ_CN
    assert PROJ_CN == MIXA_COLS
    return pl.pallas_call(
        _proj_kernel,
        grid=(batch, ncol),
        in_specs=[
            pl.BlockSpec((seq, D_MODEL), lambda b, j: (b, 0)),
            pl.BlockSpec((None, 1, D_MODEL), lambda b, j: (layer, 0, 0)),
            pl.BlockSpec((None, D_MODEL, PROJ_CN), lambda b, j: (layer, 0, j)),
            pl.BlockSpec((None, 3, CONV_CH), lambda b, j: (layer, 0, 0)),
            pl.BlockSpec((None, 1, CONV_CH), lambda b, j: (layer, 0, 0)),
            pl.BlockSpec((CONV_CH, CONV_CH), lambda b, j: (0, 0)),
        ],
        out_specs=[pl.BlockSpec((seq, PROJ_CN), lambda b, j: (b, jnp.maximum(j - 1, 0))),
                   pl.BlockSpec((seq, CONV_CH), lambda b, j: (b, 0))],
        out_shape=[jax.ShapeDtypeStruct((batch * seq, IN_COLS - MIXA_COLS), jnp.float32),
                   jax.ShapeDtypeStruct((batch * seq, CONV_CH), MIX_DTYPE)],
        scratch_shapes=[pltpu.VMEM((seq, D_MODEL), jnp.bfloat16),
                        pltpu.VMEM((seq, MIXA_COLS), jnp.float32),
                        pltpu.VMEM((seq + 2 * HALO, CONV_CH), jnp.float32)],
        compiler_params=_cparams(("arbitrary", "arbitrary")),
        name="proj_in",
    )(x2d, g_row, w_bf, conv_a, norm_a_row, group_mat)


def _mixa_body(p_ref, cw_ref, ng_ref, gm_ref, o_ref, u_ref):
    seq = o_ref.shape[0]
    nchunk = seq // ROW_CHUNK
    zeros = jnp.zeros((HALO, CONV_CH), jnp.float32)
    u_ref[0:HALO, :] = zeros
    u_ref[HALO + seq:HALO + seq + HALO, :] = zeros

    def fill(i, c):
        rows = pl.ds(pl.multiple_of(i * ROW_CHUNK, ROW_CHUNK), ROW_CHUNK)
        xa = p_ref[rows, 0:CONV_CH]
        gc = p_ref[rows, 2 * CONV_CH:3 * CONV_CH]
        u_ref[pl.ds(pl.multiple_of(i * ROW_CHUNK + HALO, HALO), ROW_CHUNK), :] = gc * xa
        return c
    lax.fori_loop(0, nchunk, fill, 0)

    for i in range(nchunk):
        start = i * ROW_CHUNK
        gb = p_ref[start:start + ROW_CHUNK, CONV_CH:2 * CONV_CH]
        y = gb * _conv3_ref(u_ref, (), start + HALO, ROW_CHUNK, cw_ref[...])
        sq = y * y
        hi = _bf(sq)
        lo = _bf(sq - hi.astype(jnp.float32))
        ms = (_dot(hi, gm_ref[...]) + _dot(lo, gm_ref[...])) * (1.0 / (CONV_CH // CONV_GROUPS))
        o_ref[start:start + ROW_CHUNK, :] = _bf(y * lax.rsqrt(ms + EPS) * ng_ref[...])


def _tri_inverse_group(a_list, lowers, xor_ij, eye):
    n = a_list[0].shape[0]
    d_list = [eye - jnp.where(xor_ij == 1, a, 0.0) for a in a_list]
    level = 1
    while (1 << level) < n:
        s = 1 << level
        mask = (xor_ij >> level) == 1
        e_list = [_bf(jnp.where(mask, a, 0.0)) for a in a_list]
        d_bf = [_bf(d) for d in d_list]
        if s < SUBLANES:
            de_list = [_dot(db, e) for db, e in zip(d_bf, e_list)]
            d_list = [d - _dot(_bf(de), db) for d, de, db in zip(d_list, de_list, d_bf)]
        else:
            blocks = [[d[b * s:(b + 1) * s] for b in range(n // s)] for d in d_list]
            picks = [range(1, n // s, 2) if lower else range(0, n // s, 2) for lower in lowers]
            sel = [_bf(jnp.concatenate([blk[b] for b in pk], axis=0)) for blk, pk in zip(blocks, picks)]
            de_list = [_dot(x, e) for x, e in zip(sel, e_list)]
            upd = [_dot(_bf(de), db) for de, db in zip(de_list, d_bf)]
            d_list = []
            for blk, pk, u in zip(blocks, picks, upd):
                for k, b in enumerate(pk):
                    blk[b] = blk[b] - u[k * s:(k + 1) * s]
                d_list.append(jnp.concatenate(blk, axis=0))
        level += 1
    return d_list


def _dn_intra_group(items, ii, jj, xor_ij, eye):
    c = items[0][0].shape[0]
    wide = lambda col: jnp.concatenate([col] * (c // LANES), axis=1)
    grams = []
    for q, k in [(item[0], item[1]) for item in items[::2]]:
        k_bf = _bf(k)
        grams.append((_dot_nt(k_bf, k_bf), _dot_nt(_bf(q), k_bf)))
    pre = []
    for idx, (q, k, v, beta, gcol, grow, lower) in enumerate(items):
        if lower:
            incl, strict, glast = ii >= jj, ii > jj, gcol[c - 1:c, :]
        else:
            incl, strict, glast = ii <= jj, ii < jj, gcol[0:1, :]
        kk, qk_raw = grams[idx // 2]
        dec = jnp.where(incl, jnp.exp(jnp.where(incl, wide(gcol) - grow, 0.0)), 0.0)
        a = jnp.where(strict, kk * wide(beta) * dec, 0.0)
        pre.append((a, _bf(qk_raw * dec), k * beta, glast))
    lhs_list = [jnp.concatenate([_bf((item[1] * jnp.exp(p[3] - item[4])).T), p[1]], axis=0)
                for item, p in zip(items, pre)]
    tinv_list = _tri_inverse_group([p[0] for p in pre], [item[6] for item in items], xor_ij, eye)
    hd = DN_HEAD_DIM
    eg_list = [jnp.exp(item[4]) for item in items]
    sol_list = [_bf(_dot(_bf(tinv), _bf(jnp.concatenate([item[2] * item[3], p[2] * eg], axis=1))))
                for item, p, eg, tinv in zip(items, pre, eg_list, tinv_list)]
    r_list = [_dot(lhs, sol) for lhs, sol in zip(lhs_list, sol_list)]
    return [(r[:hd, :hd], _bf(r[:hd, hd:]), r[hd:, :hd], _bf(item[0] * eg - r[hd:, hd:]), jnp.exp(p[3]))
            for item, p, eg, r in zip(items, pre, eg_list, r_list)]


def _dn_kernel(q_ref, k_ref, v_ref, gate_ref, small_ref, cq_ref, ck_ref, cv_ref, alog_ref, dtb_ref,
               ng_ref, o_ref,
               pad_ref, qn_ref, kn_ref, vn_ref, bf_ref, bb_ref, gf_ref, gb_ref, rf_ref, rb_ref,
               of_ref, ob_ref, b_ref, p_ref, qp_ref, gl_ref, beta_ref, cf_ref, cb_ref):
    seq = o_ref.shape[0]
    head = pl.program_id(1)
    nrow = seq // ROW_CHUNK
    nchunk = seq // DN_CHUNK

    lane = lax.broadcasted_iota(jnp.int32, (DN_CHUNK, LANES), 1)
    ii = lax.broadcasted_iota(jnp.int32, (DN_CHUNK, DN_CHUNK), 0)
    jj = lax.broadcasted_iota(jnp.int32, (DN_CHUNK, DN_CHUNK), 1)
    tri2 = jnp.concatenate([jnp.where(ii >= jj, 1.0, 0.0), jnp.where(ii <= jj, 1.0, 0.0)],
                           axis=0).astype(jnp.bfloat16)

    def pick(x, col):
        sel = jnp.sum(jnp.where(lane == col, x, 0.0), axis=-1, keepdims=True)
        return jnp.broadcast_to(sel, x.shape)

    @pl.when(head == 0)
    def _():
        neg_a = -jnp.exp(alog_ref[...])

        def gates(i, c):
            cis = [2 * i, 2 * i + 1]
            rows = [pl.ds(pl.multiple_of(ci * DN_CHUNK, DN_CHUNK), DN_CHUNK) for ci in cis]
            sm = [small_ref[r, :] for r in rows]
            z = [s + dtb_ref[...] for s in sm]
            g = [neg_a * (jnp.maximum(zz, 0.0) + jnp.log1p(jnp.exp(-jnp.abs(zz)))) for zz in z]
            g1 = [_bf(x) for x in g]
            r1 = [x - x1.astype(jnp.float32) for x, x1 in zip(g, g1)]
            g2 = [_bf(x) for x in r1]
            g3 = [_bf(x - x2.astype(jnp.float32)) for x, x2 in zip(r1, g2)]
            r = [_dot(tri2, jnp.concatenate(parts, axis=1)) for parts in zip(g1, g2, g3)]
            cf = [x[:DN_CHUNK, 2 * LANES:] + x[:DN_CHUNK, LANES:2 * LANES] + x[:DN_CHUNK, :LANES] for x in r]
            cb = [x[DN_CHUNK:, 2 * LANES:] + x[DN_CHUNK:, LANES:2 * LANES] + x[DN_CHUNK:, :LANES] for x in r]
            for k in range(2):
                beta_ref[rows[k], :] = 1.0 / (1.0 + jnp.exp(-sm[k]))
                cf_ref[rows[k], :] = cf[k]
                cb_ref[rows[k], :] = cb[k]
                rf_ref[cis[k]] = cf[k].T
                rb_ref[cis[k]] = cb[k].T
            return c
        lax.fori_loop(0, nchunk // 2, gates, 0)

    assert DN_CHUNK == ROW_CHUNK
    zeros = jnp.zeros((HALO, DN_HEAD_DIM), jnp.float32)
    pad_ref[0:HALO, :] = zeros
    pad_ref[HALO + seq:HALO + seq + HALO, :] = zeros

    def conv_silu(src_ref, cw_ref, dst_ref, mode, picks):
        def fill(i, c):
            rows = pl.ds(pl.multiple_of(i * ROW_CHUNK, ROW_CHUNK), ROW_CHUNK)
            pad_ref[pl.ds(pl.multiple_of(i * ROW_CHUNK + HALO, HALO), ROW_CHUNK), :] = src_ref[rows, :]
            return c
        lax.fori_loop(0, nrow, fill, 0)

        for i in range(nrow):
            start = i * ROW_CHUNK
            rows = slice(start, start + ROW_CHUNK)
            y = _silu(_conv3_ref(pad_ref, (), start + HALO, ROW_CHUNK, cw_ref[...]))
            if mode != "v":
                y = y * lax.rsqrt(jnp.sum(y * y, axis=-1, keepdims=True) + EPS)
            if mode == "q":
                y = y * (DN_HEAD_DIM ** -0.5)
            dst_ref[rows, :] = y
            for src, col, dst in picks:
                dst[rows, :] = pick(src[rows, :], col + head)

    conv_silu(q_ref, cq_ref, qn_ref, "q", [(beta_ref, SMALL_BF, bf_ref), (beta_ref, SMALL_BB, bb_ref)])
    conv_silu(k_ref, ck_ref, kn_ref, "k", [(cf_ref, SMALL_AF, gf_ref)])
    conv_silu(v_ref, cv_ref, vn_ref, "v", [(cb_ref, SMALL_AB, gb_ref)])

    xor_ij = ii ^ jj
    eye = jnp.where(ii == jj, 1.0, 0.0)

    def intra(i, c):
        items, where = [], []
        for sub in range(DN_INTRA_UNROLL):
            ci = i * DN_INTRA_UNROLL + sub
            rows = pl.ds(pl.multiple_of(ci * DN_CHUNK, DN_CHUNK), DN_CHUNK)
            for direction in (0, 1):
                if direction == 0:
                    beta, gcol = bf_ref[rows, :], gf_ref[rows, :]
                    grow = rf_ref[ci, pl.ds(SMALL_AF + head, 1), :]
                else:
                    beta, gcol = bb_ref[rows, :], gb_ref[rows, :]
                    grow = rb_ref[ci, pl.ds(SMALL_AB + head, 1), :]
                items.append((qn_ref[rows, :], kn_ref[rows, :], vn_ref[rows, :], beta, gcol, grow, direction == 0))
                where.append((direction, ci, rows))
        outs = (of_ref, ob_ref)
        for (direction, ci, rows), (b, p, o_part, qp, gl) in zip(where, _dn_intra_group(items, ii, jj, xor_ij, eye)):
            b_ref[direction, ci] = b
            p_ref[direction, ci] = p
            outs[direction][rows, :] = o_part
            qp_ref[direction, rows, :] = qp
            gl_ref[direction, ci] = jnp.broadcast_to(gl, (SUBLANES, LANES))
        return c
    lax.fori_loop(0, nchunk // DN_INTRA_UNROLL, intra, 0)

    def fin(ci):
        rows = slice(ci * DN_CHUNK, (ci + 1) * DN_CHUNK)
        o = of_ref[rows, :] + ob_ref[rows, :]
        o = o * lax.rsqrt(jnp.mean(o * o, axis=-1, keepdims=True) + EPS) * ng_ref[...]
        o_ref[rows, :] = _bf(o * _silu(gate_ref[rows, :]))

    state = [jnp.zeros((DN_HEAD_DIM, DN_HEAD_DIM), jnp.float32)] * 2
    for step in range(nchunk):
        cis = (step, nchunk - 1 - step)
        rows = [slice(ci * DN_CHUNK, (ci + 1) * DN_CHUNK) for ci in cis]
        s_bf = [_bf(s) for s in state]
        s_new = [state[d] * gl_ref[d, cis[d], 0:1, :] + b_ref[d, cis[d]] - _dot(p_ref[d, cis[d]], s_bf[d])
                 for d in (0, 1)]
        o_add = [_dot(qp_ref[d, rows[d], :], s_bf[d]) for d in (0, 1)]
        for d, out_ref in ((0, of_ref), (1, ob_ref)):
            out_ref[rows[d], :] += o_add[d]
        state = s_new
        for ci in range(nchunk):
            if max(ci, nchunk - 1 - ci) == step:
                fin(ci)


def _deltanet(proj, conv_qkv, alog_row, dtb_row, norm_dn_row, layer, batch, seq):
    nchunk = seq // DN_CHUNK
    hd = DN_HEAD_DIM
    qb, gb, sb = ((off - MIXA_COLS) // hd for off in (OFF_DN_QKV, OFF_DN_GATE, OFF_SMALL))
    seq_buf = pltpu.VMEM((seq, hd), jnp.float32)
    return pl.pallas_call(
        _dn_kernel,
        grid=(batch, DN_HEADS),
        in_specs=[
            pl.BlockSpec((seq, hd), lambda b, h: (b, qb + h)),
            pl.BlockSpec((seq, hd), lambda b, h: (b, qb + DN_HEADS + h)),
            pl.BlockSpec((seq, hd), lambda b, h: (b, qb + 2 * DN_HEADS + h)),
            pl.BlockSpec((seq, hd), lambda b, h: (b, gb + h)),
            pl.BlockSpec((seq, LANES), lambda b, h: (b, sb)),
            pl.BlockSpec((None, 3, hd), lambda b, h: (layer, 0, h)),
            pl.BlockSpec((None, 3, hd), lambda b, h: (layer, 0, DN_HEADS + h)),
            pl.BlockSpec((None, 3, hd), lambda b, h: (layer, 0, 2 * DN_HEADS + h)),
            pl.BlockSpec((None, 1, LANES), lambda b, h: (layer, 0, 0)),
            pl.BlockSpec((None, 1, LANES), lambda b, h: (layer, 0, 0)),
            pl.BlockSpec((None, 1, hd), lambda b, h: (layer, 0, 0)),
        ],
        out_specs=pl.BlockSpec((seq, hd), lambda b, h: (b, h)),
        out_shape=jax.ShapeDtypeStruct((batch * seq, DN_WIDTH), MIX_DTYPE),
        scratch_shapes=[
            pltpu.VMEM((seq + 2 * HALO, hd), jnp.float32),
            seq_buf, seq_buf, seq_buf,
            seq_buf, seq_buf, seq_buf, seq_buf,
            pltpu.VMEM((nchunk, LANES, DN_CHUNK), jnp.float32),
            pltpu.VMEM((nchunk, LANES, DN_CHUNK), jnp.float32),
            seq_buf, seq_buf,
            pltpu.VMEM((2, nchunk, hd, hd), jnp.float32),
            pltpu.VMEM((2, nchunk, hd, hd), jnp.bfloat16),
            pltpu.VMEM((2, seq, hd), jnp.bfloat16),
            pltpu.VMEM((2, nchunk, SUBLANES, LANES), jnp.float32),
            seq_buf, seq_buf, seq_buf,
        ],
        compiler_params=_cparams(("arbitrary", "arbitrary")),
        name="deltanet",
    )(proj, proj, proj, proj, proj, conv_qkv, conv_qkv, conv_qkv, alog_row, dtb_row, norm_dn_row)


def _rope_table_kernel(pos_ref, invf_ref, cos_ref, sin_ref):
    def body(i, c):
        rows = pl.ds(pl.multiple_of(i * ROW_CHUNK, ROW_CHUNK), ROW_CHUNK)
        ang = pos_ref[rows, :].astype(jnp.float32) * invf_ref[...]
        cos_ref[rows, :] = jnp.cos(ang)
        sin_ref[rows, :] = jnp.sin(ang)
        return c
    lax.fori_loop(0, cos_ref.shape[0] // ROW_CHUNK, body, 0)


def _rope_tables(pos_col, invf_row, batch, seq):
    table = jax.ShapeDtypeStruct((batch * seq, LANES), jnp.float32)
    return pl.pallas_call(
        _rope_table_kernel,
        grid=(batch,),
        in_specs=[pl.BlockSpec((seq, 1), lambda b: (b, 0)), pl.BlockSpec((1, LANES), lambda b: (0, 0))],
        out_specs=[pl.BlockSpec((seq, LANES), lambda b: (b, 0))] * 2,
        out_shape=[table, table],
        compiler_params=_cparams(("arbitrary",)),
        name="rope_tables",
    )(pos_col, invf_row)


def _swa_kernel(q_ref, k_ref, v_ref, cos_ref, sin_ref, rot_ref, ng_ref, gm_ref, o_ref,
                qr_ref, kr_ref, m_ref, a_ref, bias_ref):
    seq = o_ref.shape[0]
    step = pl.program_id(2)
    nrow = seq // ROW_CHUNK

    def rope(i, c):
        rows = pl.ds(pl.multiple_of(i * ROPE_ROWS, ROPE_ROWS), ROPE_ROWS)
        cs, sn = cos_ref[rows, :], sin_ref[rows, :]
        x = jnp.concatenate([q_ref[rows, :], k_ref[rows, :]], axis=1)
        xr = _dot(_bf(x), rot_ref[...])
        y = x * jnp.concatenate([cs, cs], axis=1) + xr * jnp.concatenate([sn, sn], axis=1)
        qr_ref[rows, :] = y[:, :LANES] * SWA_Q_SCALE
        kr_ref[rows, :] = y[:, LANES:]
        return c
    lax.fori_loop(0, seq // ROPE_ROWS, rope, 0)

    def run_pattern(dil, first):
        sub_len = seq // dil
        nblk = sub_len // SWA_QB
        win = min(sub_len, SWA_QB + 2 * SWA_RADIUS)
        lane_q = lax.broadcasted_iota(jnp.int32, (SWA_QB, LANES), 1)
        head_lo = lane_q < SWA_HEAD_DIM
        lane_w = lax.broadcasted_iota(jnp.int32, (win, LANES), 1)
        own_v = (lane_w < SWA_HEAD_DIM, lane_w >= SWA_HEAD_DIM)
        dij = (lax.broadcasted_iota(jnp.int32, (SWA_QB, win), 0)
               - lax.broadcasted_iota(jnp.int32, (SWA_QB, win), 1))
        for k in range(3 if sub_len > win else 1):
            bias_ref[k, :, 0:win] = jnp.where(jnp.abs(dij + k * SWA_RADIUS) <= SWA_RADIUS, 0.0, NEG_INF)

        def sl(start, size):
            return pl.ds(start, size) if dil == 1 else pl.ds(start, size, stride=dil)

        def body(i, c):
            blocks = []
            for sub in range(SWA_UNROLL):
                it = i * SWA_UNROLL + sub
                res = it // nblk
                qs = (it % nblk) * SWA_QB
                ws = jnp.clip(qs - SWA_RADIUS, 0, sub_len - win)
                rq = sl(res + qs * dil, SWA_QB)
                rk = sl(res + ws * dil, win)
                bias = bias_ref[(qs - ws) // SWA_RADIUS, :, 0:win]
                blocks.append((rq, qr_ref[rq, :], _bf(kr_ref[rk, :]), v_ref[rk, :], bias))
            heads = [(b, hh) for b in range(SWA_UNROLL) for hh in (0, 1)]
            own = (head_lo, jnp.logical_not(head_lo))
            s_all = [_dot_nt(_bf(jnp.where(own[hh], blocks[b][1], 0.0)), blocks[b][2]) + blocks[b][4]
                     for b, hh in heads]
            v_ext = [_bf(jnp.where(own_v[hh], blocks[b][3], 1.0)) for b, hh in heads]
            m_cur = [jnp.broadcast_to(jnp.max(s, axis=-1, keepdims=True), (SWA_QB, LANES)) for s in s_all]
            if first:
                m_new = m_cur
            else:
                m_old = [m_ref[hh, blocks[b][0], :] for b, hh in heads]
                m_new = [jnp.maximum(m_o, m_c) for m_o, m_c in zip(m_old, m_cur)]
            p_all = [jnp.exp2(s - jnp.concatenate([m_n] * (win // LANES), axis=1)) for s, m_n in zip(s_all, m_new)]
            pv_all = [_dot(_bf(p), v) for p, v in zip(p_all, v_ext)]
            for idx, (b, hh) in enumerate(heads):
                rq = blocks[b][0]
                if first:
                    a_ref[hh, rq, :] = pv_all[idx]
                else:
                    a_ref[hh, rq, :] = jnp.exp2(m_old[idx] - m_new[idx]) * a_ref[hh, rq, :] + pv_all[idx]
                m_ref[hh, rq, :] = m_new[idx]
            return c
        lax.fori_loop(0, dil * nblk // SWA_UNROLL, body, 0)

    for s_idx, p_idx in enumerate(SWA_ORDER):
        pl.when(step == s_idx)(functools.partial(run_pattern, SWA_PATTERNS[p_idx][1], s_idx == 0))

    @pl.when(step == SWA_N_PAT - 1)
    def _():
        lo = lax.broadcasted_iota(jnp.int32, (ROW_CHUNK, LANES), 1) < SWA_HEAD_DIM

        def fin(i, c):
            rows = pl.ds(pl.multiple_of(i * ROW_CHUNK, ROW_CHUNK), ROW_CHUNK)
            a0, a1 = a_ref[0, rows, :], a_ref[1, rows, :]
            o = jnp.where(lo, a0, a1) / pltpu.roll(jnp.where(lo, a1, a0), SWA_HEAD_DIM, 1)
            sq = o * o
            hi = _bf(sq)
            rest = _bf(sq - hi.astype(jnp.float32))
            ms = (_dot(hi, gm_ref[...]) + _dot(rest, gm_ref[...])) * (1.0 / SWA_HEAD_DIM)
            o_ref[rows, :] = _bf(o * lax.rsqrt(ms + EPS) * ng_ref[...])
            return c
        lax.fori_loop(0, nrow, fin, 0)


def _swa(proj, cos_tab, sin_tab, rot_mat, norm_c_row, group_mat, layer, batch, seq):
    base = (OFF_SWA - MIXA_COLS) // LANES
    per_pat = SWA_WIDTH // LANES
    qkv = SWA_QKV_WIDTH // LANES
    pairs = SWA_HEADS // 2
    seq_buf = pltpu.VMEM((seq, LANES), jnp.float32)
    first, stride = SWA_ORDER[0], SWA_ORDER[1] - SWA_ORDER[0]
    assert tuple(first + p * stride for p in range(SWA_N_PAT)) == SWA_ORDER
    assert CONV_CH // CONV_GROUPS == SWA_HEAD_DIM

    def spec(which):
        return pl.BlockSpec((seq, LANES),
                            lambda b, hp, p: (b, base + which * qkv + (first + p * stride) * per_pat + hp))

    return pl.pallas_call(
        _swa_kernel,
        grid=(batch, pairs, SWA_N_PAT),
        in_specs=[
            spec(0), spec(1), spec(2),
            pl.BlockSpec((seq, LANES), lambda b, hp, p: (b, 0)),
            pl.BlockSpec((seq, LANES), lambda b, hp, p: (b, 0)),
            pl.BlockSpec((2 * LANES, 2 * LANES), lambda b, hp, p: (0, 0)),
            pl.BlockSpec((None, 1, LANES), lambda b, hp, p: (layer, 0, hp)),
            pl.BlockSpec((LANES, LANES), lambda b, hp, p: (0, 0)),
        ],
        out_specs=pl.BlockSpec((seq, LANES), lambda b, hp, p: (b, hp)),
        out_shape=jax.ShapeDtypeStruct((batch * seq, SWA_WIDTH), MIX_DTYPE),
        scratch_shapes=([seq_buf] * 2 + [pltpu.VMEM((2, seq, LANES), jnp.float32)] * 2
                        + [pltpu.VMEM((3, SWA_QB, SWA_QB + 2 * SWA_RADIUS), jnp.float32)]),
        compiler_params=_cparams(("arbitrary", "arbitrary", "arbitrary")),
        name="swa",
    )(proj, proj, proj, cos_tab, sin_tab, rot_mat, norm_c_row, group_mat)


def _outproj_kernel(x_ref, ya_ref, yb_ref, yc_ref, w_ref, o_ref):
    a0, a1, a2 = CONV_CH, CONV_CH + DN_WIDTH, MIX_WIDTH
    acc = _dot(ya_ref[...], w_ref[0:a0, :])
    acc = acc + _dot(yb_ref[...], w_ref[a0:a1, :])
    acc = acc + _dot(yc_ref[...], w_ref[a1:a2, :])
    o_ref[...] = x_ref[...] + acc


def _out_proj(x2d, y_a, y_b, y_c, w_bf, layer):
    n = x2d.shape[0]
    tm = OUT_TM
    return pl.pallas_call(
        _outproj_kernel,
        grid=(n // tm,),
        in_specs=[
            pl.BlockSpec((tm, D_MODEL), lambda i: (i, 0)),
            pl.BlockSpec((tm, CONV_CH), lambda i: (i, 0)),
            pl.BlockSpec((tm, DN_WIDTH), lambda i: (i, 0)),
            pl.BlockSpec((tm, SWA_WIDTH), lambda i: (i, 0)),
            pl.BlockSpec((None, MIX_WIDTH, D_MODEL), lambda i: (layer, 0, 0)),
        ],
        out_specs=pl.BlockSpec((tm, D_MODEL), lambda i: (i, 0)),
        out_shape=jax.ShapeDtypeStruct((n, D_MODEL), jnp.float32),
        compiler_params=_cparams(("arbitrary",)),
        name="out_proj",
    )(x2d, y_a, y_b, y_c, w_bf)


def _ffn_kernel(xp_ref, x_ref, xn_ref, g_ref, wu_ref, cf_ref, wd_ref, gf_ref, o_ref,
                h_ref, a_ref, *, tiles_per_seq, final_norm):
    tm = x_ref.shape[0]
    t = pl.program_id(0) % tiles_per_seq

    def norm(x):
        return x * lax.rsqrt(jnp.mean(x * x, axis=-1, keepdims=True) + EPS) * g_ref[...]

    halo = HALO_BF16
    hp = norm(xp_ref[...])
    hn = norm(xn_ref[...])
    h_ref[0:halo, :] = _bf(jnp.where(t == 0, 0.0, hp))
    h_ref[halo + tm:halo + tm + halo, :] = _bf(jnp.where(t == tiles_per_seq - 1, 0.0, hn))

    def fill(i, c):
        rows = pl.ds(pl.multiple_of(i * ROW_CHUNK, ROW_CHUNK), ROW_CHUNK)
        h_ref[pl.ds(pl.multiple_of(i * ROW_CHUNK + halo, halo), ROW_CHUNK), :] = _bf(norm(x_ref[rows, :]))
        return c
    lax.fori_loop(0, tm // ROW_CHUNK, fill, 0)

    for j in range(D_FF // FF_CHUNK):
        h = h_ref[...]
        gcols = slice(j * FF_CHUNK, (j + 1) * FF_CHUNK)
        vcols = slice(D_FF + j * FF_CHUNK, D_FF + (j + 1) * FF_CHUNK)
        gate = _conv3_val(_dot(h, wu_ref[:, gcols]), cf_ref[:, gcols], tm, halo)
        val = _conv3_val(_dot(h, wu_ref[:, vcols]), cf_ref[:, vcols], tm, halo)
        a_ref[:, gcols] = _bf(_silu(gate) * val)
    o_ref[...] = _dot(a_ref[...], wd_ref[...])

    def fin(i, c):
        rows = pl.ds(pl.multiple_of(i * ROW_CHUNK, ROW_CHUNK), ROW_CHUNK)
        y = x_ref[rows, :] + o_ref[rows, :]
        if final_norm:
            y = y * lax.rsqrt(jnp.mean(y * y, axis=-1, keepdims=True) + EPS) * gf_ref[...]
        o_ref[rows, :] = y
        return c
    lax.fori_loop(0, tm // ROW_CHUNK, fin, 0)


def _ffn(x2d, g_row, wu, cf, wd, gf_row, layer, seq, final_norm):
    n = x2d.shape[0]
    tm = FFN_TM
    tiles_per_seq = seq // tm
    halo = HALO_BF16
    hb = tm // halo
    nhb = n // halo
    whole = lambda shape: pl.BlockSpec((None,) + shape, lambda i: (layer,) + (0,) * len(shape),
                                       pipeline_mode=pl.Buffered(1))
    kern = functools.partial(_ffn_kernel, tiles_per_seq=tiles_per_seq, final_norm=final_norm)
    return pl.pallas_call(
        kern,
        grid=(n // tm,),
        in_specs=[
            pl.BlockSpec((halo, D_MODEL), lambda i: (jnp.maximum(i * hb - 1, 0), 0)),
            pl.BlockSpec((tm, D_MODEL), lambda i: (i, 0)),
            pl.BlockSpec((halo, D_MODEL), lambda i: (jnp.minimum((i + 1) * hb, nhb - 1), 0)),
            whole((1, D_MODEL)),
            whole((D_MODEL, 2 * D_FF)),
            whole((3, 2 * D_FF)),
            whole((D_FF, D_MODEL)),
            pl.BlockSpec((1, D_MODEL), lambda i: (0, 0)),
        ],
        out_specs=pl.BlockSpec((tm, D_MODEL), lambda i: (i, 0)),
        out_shape=jax.ShapeDtypeStruct((n, D_MODEL), jnp.float32),
        scratch_shapes=[
            pltpu.VMEM((tm + 2 * halo, D_MODEL), jnp.bfloat16),
            pltpu.VMEM((tm, D_FF), jnp.bfloat16),
        ],
        compiler_params=_cparams(("arbitrary",)),
        name="ffn",
    )(x2d, x2d, x2d, g_row, wu, cf, wd, gf_row)


def _win_prep_kernel(w_ref, o_ref):
    n_small = 4 * DN_HEADS
    n_swa = 3 * SWA_QKV_WIDTH
    o_ref[:, 0:OFF_SWA] = _bf(w_ref[:, 0:OFF_SWA])
    o_ref[:, OFF_SWA:OFF_SMALL] = _bf(w_ref[:, OFF_SWA + n_small:OFF_SWA + n_small + n_swa])
    o_ref[:, OFF_SMALL:OFF_SMALL + LANES] = _bf(jnp.concatenate(
        [w_ref[:, OFF_SWA:OFF_SWA + n_small], jnp.zeros((w_ref.shape[0], LANES - n_small), jnp.float32)], axis=1))
    o_ref[:, OFF_SMALL + LANES:] = jnp.zeros((w_ref.shape[0], IN_COLS - OFF_SMALL - LANES), jnp.bfloat16)


def _reorder_w_in(w):
    depth, rows, cols = w.shape
    return pl.pallas_call(
        _win_prep_kernel,
        grid=(depth, rows // ROW_CHUNK),
        in_specs=[pl.BlockSpec((None, ROW_CHUNK, cols), lambda l, i: (l, i, 0))],
        out_specs=pl.BlockSpec((None, ROW_CHUNK, IN_COLS), lambda l, i: (l, i, 0)),
        out_shape=jax.ShapeDtypeStruct((depth, rows, IN_COLS), jnp.bfloat16),
        compiler_params=_cparams(("arbitrary", "arbitrary")),
        name="w_in_prep",
    )(w)


def _small_rows(f_vals, b_vals):
    row = jnp.concatenate([f_vals, b_vals], axis=-1).astype(jnp.float32)
    return jnp.pad(row, ((0, 0), (SMALL_AF, LANES - SMALL_AF - row.shape[-1])))[:, None, :]


def kernel(x, positions, norm_mix, w_in, conv_a, norm_a, conv_qkv, a_log_f, a_log_b, dt_bias_f, dt_bias_b,
           norm_dn, norm_c, w_o, norm_ffn, w_up, conv_ffn, w_down, norm_final):
    batch, seq, _ = x.shape
    depth = w_in.shape[0]
    x2d = x.reshape(batch * seq, D_MODEL)
    pos_col = positions.reshape(batch * seq, 1)

    half = ROPE_DIM // 2
    inv_freq = ROPE_THETA ** (-jnp.arange(half, dtype=jnp.float32) / half)
    d_idx = np.arange(LANES) % SWA_HEAD_DIM
    invf_row = jnp.where(jnp.asarray(d_idx < ROPE_DIM), inv_freq[d_idx % half], 0.0).reshape(1, LANES)
    rot_np = np.zeros((2 * LANES, 2 * LANES), np.float32)
    for lane_idx in range(2 * LANES):
        if d_idx[lane_idx % LANES] < half:
            rot_np[lane_idx + half, lane_idx] = -1.0
        elif d_idx[lane_idx % LANES] < ROPE_DIM:
            rot_np[lane_idx - half, lane_idx] = 1.0
    rot_mat = jnp.asarray(rot_np, jnp.bfloat16)
    grp = np.arange(CONV_CH) // (CONV_CH // CONV_GROUPS)
    group_mat = _bf(jnp.asarray(grp[:, None] == grp[None, :], jnp.float32))

    rows = lambda a: a[:, None, :]
    w_in_bf, w_o_bf, w_up_bf, w_down_bf = _reorder_w_in(w_in), _bf(w_o), _bf(w_up), _bf(w_down)
    alog_rows, dtb_rows = _small_rows(a_log_f, a_log_b), _small_rows(dt_bias_f, dt_bias_b)
    cos_tab, sin_tab = _rope_tables(pos_col, invf_row, batch, seq)
    for l in range(depth):
        proj, y_a = _proj_in(x2d, rows(norm_mix), w_in_bf, conv_a, rows(norm_a), group_mat, l, batch, seq)
        y_b = _deltanet(proj, conv_qkv, alog_rows, dtb_rows, rows(norm_dn), l, batch, seq)
        y_c = _swa(proj, cos_tab, sin_tab, rot_mat, rows(norm_c), group_mat, l, batch, seq)
        x2d = _out_proj(x2d, y_a, y_b, y_c, w_o_bf, l)
        x2d = _ffn(x2d, rows(norm_ffn), w_up_bf, conv_ffn, w_down_bf, norm_final.reshape(1, D_MODEL), l, seq,
                   l == depth - 1)
    return x2d.reshape(batch, seq, D_MODEL)
```

```python
import functools

import numpy as np
import jax
import jax.numpy as jnp
from jax import lax
from jax.experimental import pallas as pl
from jax.experimental.pallas import tpu as pltpu

D_MODEL = 1024
EPS = 1e-6
NEG_INF = -1e30
CONV_CH = 256
CONV_GROUPS = 4
DN_HEADS = 4
DN_HEAD_DIM = 128
DN_WIDTH = DN_HEADS * DN_HEAD_DIM
SWA_HEAD_DIM = 64
SWA_HEADS = 4
SWA_PATTERNS = ((128, 1), (512, 4), (2048, 16))
SWA_N_PAT = 3
SWA_WIDTH = SWA_HEADS * SWA_HEAD_DIM
SWA_QKV_WIDTH = SWA_N_PAT * SWA_WIDTH
ROPE_THETA = 500000.0
ROPE_DIM = SWA_HEAD_DIM // 4
D_FF = 2816
MIX_WIDTH = CONV_CH + DN_WIDTH + SWA_WIDTH

MIX_DTYPE = jnp.bfloat16
LANES = 128
SUBLANES = 8
VMEM_LIMIT = 56 * 1024 * 1024

OFF_A = 0
MIXA_COLS = 3 * CONV_CH
OFF_DN_QKV = OFF_A + MIXA_COLS
OFF_DN_GATE = OFF_DN_QKV + 3 * DN_WIDTH
OFF_SWA = OFF_DN_GATE + DN_WIDTH
OFF_SMALL = OFF_SWA + 3 * SWA_QKV_WIDTH
IN_COLS = 5376
PROJ_CN = 768
SMALL_BF, SMALL_BB, SMALL_AF, SMALL_AB = 0, 4, 8, 12

DN_CHUNK = 256
DN_INTRA_UNROLL = 4
DN_GATES_UNROLL = 4
ROW_CHUNK = 256
HALO = SUBLANES
HALO_BF16 = 2 * SUBLANES
FF_CHUNK = 512
FFN_TM = 1024
OUT_TM = 2048
SWA_QB = 128
SWA_RADIUS = 64
SWA_UNROLL = 8
ROPE_ROWS = 512
SWA_FIN_UNROLL = 4
SWA_Q_SCALE = SWA_HEAD_DIM ** -0.5 * float(np.log2(np.e))
SWA_ORDER = (2, 1, 0)


def _cparams(sem, vmem_limit=VMEM_LIMIT):
    return pltpu.CompilerParams(dimension_semantics=sem, vmem_limit_bytes=vmem_limit)


def _bf(x):
    return x.astype(jnp.bfloat16)


def _dot(a, b):
    return jnp.dot(a, b, preferred_element_type=jnp.float32)


def _dot_nt(a, b):
    return lax.dot_general(a, b, (((1,), (1,)), ((), ())), preferred_element_type=jnp.float32)


def _silu(x):
    return x * (1.0 / (1.0 + jnp.exp(-x)))


def _conv3_val(ext, w, rows, halo):
    n = ext.shape[0]
    prev = pltpu.roll(ext, 1, 0)[halo:halo + rows]
    nxt = pltpu.roll(ext, n - 1, 0)[halo:halo + rows]
    return prev * w[0:1] + ext[halo:halo + rows] * w[1:2] + nxt * w[2:3]


def _conv3_ref(ref, lead, start, rows, w):
    def tap(off):
        return ref[lead + (pl.ds(start + off, rows), slice(None))]
    return tap(-1) * w[0:1] + tap(0) * w[1:2] + tap(1) * w[2:3]


def _proj_kernel(x_ref, g_ref, w_ref, cw_ref, ng_ref, gm_ref, o_ref, ya_ref, h_ref, a_ref, u_ref):
    col = pl.program_id(1)

    @pl.when(col == 0)
    def _():
        def body(i, c):
            rows = pl.ds(pl.multiple_of(i * ROW_CHUNK, ROW_CHUNK), ROW_CHUNK)
            x = x_ref[rows, :]
            y = x * lax.rsqrt(jnp.mean(x * x, axis=-1, keepdims=True) + EPS)
            h_ref[rows, :] = _bf(y * g_ref[...])
            return c
        lax.fori_loop(0, x_ref.shape[0] // ROW_CHUNK, body, 0)
        a_ref[...] = _dot(h_ref[...], w_ref[...])
        _mixa_body(a_ref, cw_ref, ng_ref, gm_ref, ya_ref, u_ref)

    @pl.when(col > 0)
    def _():
        o_ref[...] = _dot(h_ref[...], w_ref[...])


def _proj_in(x2d, g_row, w_bf, conv_a, norm_a_row, group_mat, layer, batch, seq):
    ncol = IN_COLS // PROJ_CN
    assert PROJ_CN == MIXA_COLS
    return pl.pallas_call(
        _proj_kernel,
        grid=(batch, ncol),
        in_specs=[
            pl.BlockSpec((seq, D_MODEL), lambda b, j: (b, 0)),
            pl.BlockSpec((None, 1, D_MODEL), lambda b, j: (layer, 0, 0)),
            pl.BlockSpec((None, D_MODEL, PROJ_CN), lambda b, j: (layer, 0, j)),
            pl.BlockSpec((None, 3, CONV_CH), lambda b, j: (layer, 0, 0)),
            pl.BlockSpec((None, 1, CONV_CH), lambda b, j: (layer, 0, 0)),
            pl.BlockSpec((CONV_CH, CONV_CH), lambda b, j: (0, 0)),
        ],
        out_specs=[pl.BlockSpec((seq, PROJ_CN), lambda b, j: (b, jnp.maximum(j - 1, 0))),
                   pl.BlockSpec((seq, CONV_CH), lambda b, j: (b, 0))],
        out_shape=[jax.ShapeDtypeStruct((batch * seq, IN_COLS - MIXA_COLS), jnp.float32),
                   jax.ShapeDtypeStruct((batch * seq, CONV_CH), MIX_DTYPE)],
        scratch_shapes=[pltpu.VMEM((seq, D_MODEL), jnp.bfloat16),
                        pltpu.VMEM((seq, MIXA_COLS), jnp.float32),
                        pltpu.VMEM((seq + 2 * HALO, CONV_CH), jnp.float32)],
        compiler_params=_cparams(("arbitrary", "arbitrary")),
        name="proj_in",
    )(x2d, g_row, w_bf, conv_a, norm_a_row, group_mat)


def _mixa_body(p_ref, cw_ref, ng_ref, gm_ref, o_ref, u_ref):
    seq = o_ref.shape[0]
    nchunk = seq // ROW_CHUNK
    zeros = jnp.zeros((HALO, CONV_CH), jnp.float32)
    u_ref[0:HALO, :] = zeros
    u_ref[HALO + seq:HALO + seq + HALO, :] = zeros

    def fill(i, c):
        rows = pl.ds(pl.multiple_of(i * ROW_CHUNK, ROW_CHUNK), ROW_CHUNK)
        xa = p_ref[rows, 0:CONV_CH]
        gc = p_ref[rows, 2 * CONV_CH:3 * CONV_CH]
        u_ref[pl.ds(pl.multiple_of(i * ROW_CHUNK + HALO, HALO), ROW_CHUNK), :] = gc * xa
        return c
    lax.fori_loop(0, nchunk, fill, 0)

    for i in range(nchunk):
        start = i * ROW_CHUNK
        gb = p_ref[start:start + ROW_CHUNK, CONV_CH:2 * CONV_CH]
        y = gb * _conv3_ref(u_ref, (), start + HALO, ROW_CHUNK, cw_ref[...])
        sq = y * y
        hi = _bf(sq)
        lo = _bf(sq - hi.astype(jnp.float32))
        ms = (_dot(hi, gm_ref[...]) + _dot(lo, gm_ref[...])) * (1.0 / (CONV_CH // CONV_GROUPS))
        o_ref[start:start + ROW_CHUNK, :] = _bf(y * lax.rsqrt(ms + EPS) * ng_ref[...])


def _tri_inverse_group(a_list, lowers, xor_ij, eye):
    n = a_list[0].shape[0]
    d_list = [eye - jnp.where(xor_ij == 1, a, 0.0) for a in a_list]
    level = 1
    while (1 << level) < n:
        s = 1 << level
        mask = (xor_ij >> level) == 1
        e_list = [_bf(jnp.where(mask, a, 0.0)) for a in a_list]
        d_bf = [_bf(d) for d in d_list]
        if s < SUBLANES:
            de_list = [_dot(db, e) for db, e in zip(d_bf, e_list)]
            d_list = [d - _dot(_bf(de), db) for d, de, db in zip(d_list, de_list, d_bf)]
        else:
            blocks = [[d[b * s:(b + 1) * s] for b in range(n // s)] for d in d_list]
            picks = [range(1, n // s, 2) if lower else range(0, n // s, 2) for lower in lowers]
            sel = [_bf(jnp.concatenate([blk[b] for b in pk], axis=0)) for blk, pk in zip(blocks, picks)]
            de_list = [_dot(x, e) for x, e in zip(sel, e_list)]
            upd = [_dot(_bf(de), db) for de, db in zip(de_list, d_bf)]
            d_list = []
            for blk, pk, u in zip(blocks, picks, upd):
                for k, b in enumerate(pk):
                    blk[b] = blk[b] - u[k * s:(k + 1) * s]
                d_list.append(jnp.concatenate(blk, axis=0))
        level += 1
    return d_list


def _dn_intra_group(items, ii, jj, xor_ij, eye):
    c = items[0][0].shape[0]
    wide = lambda col: jnp.concatenate([col] * (c // LANES), axis=1)
    grams = []
    for q, k in [(item[0], item[1]) for item in items[::2]]:
        k_bf = _bf(k)
        grams.append((_dot_nt(k_bf, k_bf), _dot_nt(_bf(q), k_bf)))
    pre = []
    for idx, (q, k, v, beta, gcol, grow, lower) in enumerate(items):
        if lower:
            incl, strict, glast = ii >= jj, ii > jj, gcol[c - 1:c, :]
        else:
            incl, strict, glast = ii <= jj, ii < jj, gcol[0:1, :]
        kk, qk_raw = grams[idx // 2]
        dec = jnp.where(incl, jnp.exp(jnp.where(incl, wide(gcol) - grow, 0.0)), 0.0)
        a = jnp.where(strict, kk * wide(beta) * dec, 0.0)
        pre.append((a, _bf(qk_raw * dec), k * beta, glast))
    lhs_list = [jnp.concatenate([_bf((item[1] * jnp.exp(p[3] - item[4])).T), p[1]], axis=0)
                for item, p in zip(items, pre)]
    tinv_list = _tri_inverse_group([p[0] for p in pre], [item[6] for item in items], xor_ij, eye)
    hd = DN_HEAD_DIM
    eg_list = [jnp.exp(item[4]) for item in items]
    sol_list = [_bf(_dot(_bf(tinv), _bf(jnp.concatenate([item[2] * item[3], p[2] * eg], axis=1))))
                for item, p, eg, tinv in zip(items, pre, eg_list, tinv_list)]
    r_list = [_dot(lhs, sol) for lhs, sol in zip(lhs_list, sol_list)]
    return [(r[:hd, :hd], _bf(r[:hd, hd:]), r[hd:, :hd], _bf(item[0] * eg - r[hd:, hd:]), jnp.exp(p[3]))
            for item, p, eg, r in zip(items, pre, eg_list, r_list)]


def _dn_kernel(q_ref, k_ref, v_ref, gate_ref, small_ref, cq_ref, ck_ref, cv_ref, alog_ref, dtb_ref,
               ng_ref, o_ref,
               pad_ref, qn_ref, kn_ref, vn_ref, bf_ref, bb_ref, gf_ref, gb_ref, rf_ref, rb_ref,
               of_ref, ob_ref, b_ref, p_ref, qp_ref, gl_ref, beta_ref, cf_ref, cb_ref):
    seq = o_ref.shape[0]
    head = pl.program_id(1)
    nrow = seq // ROW_CHUNK
    nchunk = seq // DN_CHUNK

    lane = lax.broadcasted_iota(jnp.int32, (DN_CHUNK, LANES), 1)
    ii = lax.broadcasted_iota(jnp.int32, (DN_CHUNK, DN_CHUNK), 0)
    jj = lax.broadcasted_iota(jnp.int32, (DN_CHUNK, DN_CHUNK), 1)
    tri2 = jnp.concatenate([jnp.where(ii >= jj, 1.0, 0.0), jnp.where(ii <= jj, 1.0, 0.0)],
                           axis=0).astype(jnp.bfloat16)

    def pick(x, col):
        sel = jnp.sum(jnp.where(lane == col, x, 0.0), axis=-1, keepdims=True)
        return jnp.broadcast_to(sel, x.shape)

    @pl.when(head == 0)
    def _():
        neg_a = -jnp.exp(alog_ref[...])

        def gates(i, c):
            cis = [DN_GATES_UNROLL * i + k for k in range(DN_GATES_UNROLL)]
            rows = [pl.ds(pl.multiple_of(ci * DN_CHUNK, DN_CHUNK), DN_CHUNK) for ci in cis]
            sm = [small_ref[r, :] for r in rows]
            z = [s + dtb_ref[...] for s in sm]
            g = [neg_a * (jnp.maximum(zz, 0.0) + jnp.log1p(jnp.exp(-jnp.abs(zz)))) for zz in z]
            g1 = [_bf(x) for x in g]
            r1 = [x - x1.astype(jnp.float32) for x, x1 in zip(g, g1)]
            g2 = [_bf(x) for x in r1]
            g3 = [_bf(x - x2.astype(jnp.float32)) for x, x2 in zip(r1, g2)]
            r = [_dot(tri2, jnp.concatenate(parts, axis=1)) for parts in zip(g1, g2, g3)]
            cf = [x[:DN_CHUNK, 2 * LANES:] + x[:DN_CHUNK, LANES:2 * LANES] + x[:DN_CHUNK, :LANES] for x in r]
            cb = [x[DN_CHUNK:, 2 * LANES:] + x[DN_CHUNK:, LANES:2 * LANES] + x[DN_CHUNK:, :LANES] for x in r]
            for k in range(DN_GATES_UNROLL):
                beta_ref[rows[k], :] = 1.0 / (1.0 + jnp.exp(-sm[k]))
                cf_ref[rows[k], :] = cf[k]
                cb_ref[rows[k], :] = cb[k]
                rf_ref[cis[k]] = cf[k].T
                rb_ref[cis[k]] = cb[k].T
            return c
        lax.fori_loop(0, nchunk // DN_GATES_UNROLL, gates, 0)

    assert DN_CHUNK == ROW_CHUNK
    zeros = jnp.zeros((HALO, DN_HEAD_DIM), jnp.float32)
    pad_ref[0:HALO, :] = zeros
    pad_ref[HALO + seq:HALO + seq + HALO, :] = zeros

    def conv_silu(src_ref, cw_ref, dst_ref, mode, picks):
        def fill(i, c):
            rows = pl.ds(pl.multiple_of(i * ROW_CHUNK, ROW_CHUNK), ROW_CHUNK)
            pad_ref[pl.ds(pl.multiple_of(i * ROW_CHUNK + HALO, HALO), ROW_CHUNK), :] = src_ref[rows, :]
            return c
        lax.fori_loop(0, nrow, fill, 0)

        for i in range(nrow):
            start = i * ROW_CHUNK
            rows = slice(start, start + ROW_CHUNK)
            y = _silu(_conv3_ref(pad_ref, (), start + HALO, ROW_CHUNK, cw_ref[...]))
            if mode != "v":
                y = y * lax.rsqrt(jnp.sum(y * y, axis=-1, keepdims=True) + EPS)
            if mode == "q":
                y = y * (DN_HEAD_DIM ** -0.5)
            dst_ref[rows, :] = y
            for src, col, dst in picks:
                dst[rows, :] = pick(src[rows, :], col + head)

    conv_silu(q_ref, cq_ref, qn_ref, "q", [(beta_ref, SMALL_BF, bf_ref), (beta_ref, SMALL_BB, bb_ref)])
    conv_silu(k_ref, ck_ref, kn_ref, "k", [(cf_ref, SMALL_AF, gf_ref)])
    conv_silu(v_ref, cv_ref, vn_ref, "v", [(cb_ref, SMALL_AB, gb_ref)])

    xor_ij = ii ^ jj
    eye = jnp.where(ii == jj, 1.0, 0.0)

    def intra(i, c):
        items, where = [], []
        for sub in range(DN_INTRA_UNROLL):
            ci = i * DN_INTRA_UNROLL + sub
            rows = pl.ds(pl.multiple_of(ci * DN_CHUNK, DN_CHUNK), DN_CHUNK)
            for direction in (0, 1):
                if direction == 0:
                    beta, gcol = bf_ref[rows, :], gf_ref[rows, :]
                    grow = rf_ref[ci, pl.ds(SMALL_AF + head, 1), :]
                else:
                    beta, gcol = bb_ref[rows, :], gb_ref[rows, :]
                    grow = rb_ref[ci, pl.ds(SMALL_AB + head, 1), :]
                items.append((qn_ref[rows, :], kn_ref[rows, :], vn_ref[rows, :], beta, gcol, grow, direction == 0))
                where.append((direction, ci, rows))
        outs = (of_ref, ob_ref)
        for (direction, ci, rows), (b, p, o_part, qp, gl) in zip(where, _dn_intra_group(items, ii, jj, xor_ij, eye)):
            b_ref[direction, ci] = b
            p_ref[direction, ci] = p
            outs[direction][rows, :] = o_part
            qp_ref[direction, rows, :] = qp
            gl_ref[direction, ci] = jnp.broadcast_to(gl, (SUBLANES, LANES))
        return c
    lax.fori_loop(0, nchunk // DN_INTRA_UNROLL, intra, 0)

    def fin(ci):
        rows = slice(ci * DN_CHUNK, (ci + 1) * DN_CHUNK)
        o = of_ref[rows, :] + ob_ref[rows, :]
        o = o * lax.rsqrt(jnp.mean(o * o, axis=-1, keepdims=True) + EPS) * ng_ref[...]
        o_ref[rows, :] = _bf(o * _silu(gate_ref[rows, :]))

    state = [jnp.zeros((DN_HEAD_DIM, DN_HEAD_DIM), jnp.float32)] * 2
    for step in range(nchunk):
        cis = (step, nchunk - 1 - step)
        rows = [slice(ci * DN_CHUNK, (ci + 1) * DN_CHUNK) for ci in cis]
        s_bf = [_bf(s) for s in state]
        s_new = [state[d] * gl_ref[d, cis[d], 0:1, :] + b_ref[d, cis[d]] - _dot(p_ref[d, cis[d]], s_bf[d])
                 for d in (0, 1)]
        o_add = [_dot(qp_ref[d, rows[d], :], s_bf[d]) for d in (0, 1)]
        for d, out_ref in ((0, of_ref), (1, ob_ref)):
            out_ref[rows[d], :] += o_add[d]
        state = s_new
        for ci in range(nchunk):
            if max(ci, nchunk - 1 - ci) == step:
                fin(ci)


def _deltanet(proj, conv_qkv, alog_row, dtb_row, norm_dn_row, layer, batch, seq):
    nchunk = seq // DN_CHUNK
    hd = DN_HEAD_DIM
    qb, gb, sb = ((off - MIXA_COLS) // hd for off in (OFF_DN_QKV, OFF_DN_GATE, OFF_SMALL))
    seq_buf = pltpu.VMEM((seq, hd), jnp.float32)
    return pl.pallas_call(
        _dn_kernel,
        grid=(batch, DN_HEADS),
        in_specs=[
            pl.BlockSpec((seq, hd), lambda b, h: (b, qb + h)),
            pl.BlockSpec((seq, hd), lambda b, h: (b, qb + DN_HEADS + h)),
            pl.BlockSpec((seq, hd), lambda b, h: (b, qb + 2 * DN_HEADS + h)),
            pl.BlockSpec((seq, hd), lambda b, h: (b, gb + h)),
            pl.BlockSpec((seq, LANES), lambda b, h: (b, sb)),
            pl.BlockSpec((None, 3, hd), lambda b, h: (layer, 0, h)),
            pl.BlockSpec((None, 3, hd), lambda b, h: (layer, 0, DN_HEADS + h)),
            pl.BlockSpec((None, 3, hd), lambda b, h: (layer, 0, 2 * DN_HEADS + h)),
            pl.BlockSpec((None, 1, LANES), lambda b, h: (layer, 0, 0)),
            pl.BlockSpec((None, 1, LANES), lambda b, h: (layer, 0, 0)),
            pl.BlockSpec((None, 1, hd), lambda b, h: (layer, 0, 0)),
        ],
        out_specs=pl.BlockSpec((seq, hd), lambda b, h: (b, h)),
        out_shape=jax.ShapeDtypeStruct((batch * seq, DN_WIDTH), MIX_DTYPE),
        scratch_shapes=[
            pltpu.VMEM((seq + 2 * HALO, hd), jnp.float32),
            seq_buf, seq_buf, seq_buf,
            seq_buf, seq_buf, seq_buf, seq_buf,
            pltpu.VMEM((nchunk, LANES, DN_CHUNK), jnp.float32),
            pltpu.VMEM((nchunk, LANES, DN_CHUNK), jnp.float32),
            seq_buf, seq_buf,
            pltpu.VMEM((2, nchunk, hd, hd), jnp.float32),
            pltpu.VMEM((2, nchunk, hd, hd), jnp.bfloat16),
            pltpu.VMEM((2, seq, hd), jnp.bfloat16),
            pltpu.VMEM((2, nchunk, SUBLANES, LANES), jnp.float32),
            seq_buf, seq_buf, seq_buf,
        ],
        compiler_params=_cparams(("arbitrary", "arbitrary")),
        name="deltanet",
    )(proj, proj, proj, proj, proj, conv_qkv, conv_qkv, conv_qkv, alog_row, dtb_row, norm_dn_row)


def _rope_table_kernel(pos_ref, invf_ref, cos_ref, sin_ref):
    def body(i, c):
        rows = pl.ds(pl.multiple_of(i * ROW_CHUNK, ROW_CHUNK), ROW_CHUNK)
        ang = pos_ref[rows, :].astype(jnp.float32) * invf_ref[...]
        cos_ref[rows, :] = jnp.cos(ang)
        sin_ref[rows, :] = jnp.sin(ang)
        return c
    lax.fori_loop(0, cos_ref.shape[0] // ROW_CHUNK, body, 0)


def _rope_tables(pos_col, invf_row, batch, seq):
    table = jax.ShapeDtypeStruct((batch * seq, LANES), jnp.float32)
    return pl.pallas_call(
        _rope_table_kernel,
        grid=(batch,),
        in_specs=[pl.BlockSpec((seq, 1), lambda b: (b, 0)), pl.BlockSpec((1, LANES), lambda b: (0, 0))],
        out_specs=[pl.BlockSpec((seq, LANES), lambda b: (b, 0))] * 2,
        out_shape=[table, table],
        compiler_params=_cparams(("arbitrary",)),
        name="rope_tables",
    )(pos_col, invf_row)


def _swa_kernel(q_ref, k_ref, v_ref, cos_ref, sin_ref, rot_ref, ng_ref, gm_ref, o_ref,
                qr_ref, kr_ref, m_ref, a_ref, bias_ref):
    seq = o_ref.shape[0]
    step = pl.program_id(2)
    nrow = seq // ROW_CHUNK

    def rope(i, c):
        rows = pl.ds(pl.multiple_of(i * ROPE_ROWS, ROPE_ROWS), ROPE_ROWS)
        cs, sn = cos_ref[rows, :], sin_ref[rows, :]
        x = jnp.concatenate([q_ref[rows, :], k_ref[rows, :]], axis=1)
        xr = _dot(_bf(x), rot_ref[...])
        y = x * jnp.concatenate([cs, cs], axis=1) + xr * jnp.concatenate([sn, sn], axis=1)
        qr_ref[rows, :] = y[:, :LANES] * SWA_Q_SCALE
        kr_ref[rows, :] = y[:, LANES:]
        return c
    lax.fori_loop(0, seq // ROPE_ROWS, rope, 0)

    def run_pattern(dil, first):
        sub_len = seq // dil
        nblk = sub_len // SWA_QB
        win = min(sub_len, SWA_QB + 2 * SWA_RADIUS)
        lane_q = lax.broadcasted_iota(jnp.int32, (SWA_QB, LANES), 1)
        head_lo = lane_q < SWA_HEAD_DIM
        lane_w = lax.broadcasted_iota(jnp.int32, (win, LANES), 1)
        own_v = (lane_w < SWA_HEAD_DIM, lane_w >= SWA_HEAD_DIM)
        dij = (lax.broadcasted_iota(jnp.int32, (SWA_QB, win), 0)
               - lax.broadcasted_iota(jnp.int32, (SWA_QB, win), 1))
        for k in range(3 if sub_len > win else 1):
            bias_ref[k, :, 0:win] = jnp.where(jnp.abs(dij + k * SWA_RADIUS) <= SWA_RADIUS, 0.0, NEG_INF)

        def sl(start, size):
            return pl.ds(start, size) if dil == 1 else pl.ds(start, size, stride=dil)

        def body(i, c):
            blocks = []
            for sub in range(SWA_UNROLL):
                it = i * SWA_UNROLL + sub
                res = it // nblk
                qs = (it % nblk) * SWA_QB
                ws = jnp.clip(qs - SWA_RADIUS, 0, sub_len - win)
                rq = sl(res + qs * dil, SWA_QB)
                rk = sl(res + ws * dil, win)
                bias = bias_ref[(qs - ws) // SWA_RADIUS, :, 0:win]
                blocks.append((rq, qr_ref[rq, :], _bf(kr_ref[rk, :]), v_ref[rk, :], bias))
            heads = [(b, hh) for b in range(SWA_UNROLL) for hh in (0, 1)]
            own = (head_lo, jnp.logical_not(head_lo))
            s_all = [_dot_nt(_bf(jnp.where(own[hh], blocks[b][1], 0.0)), blocks[b][2]) + blocks[b][4]
                     for b, hh in heads]
            v_ext = [_bf(jnp.where(own_v[hh], blocks[b][3], 1.0)) for b, hh in heads]
            m_cur = [jnp.broadcast_to(jnp.max(s, axis=-1, keepdims=True), (SWA_QB, LANES)) for s in s_all]
            if first:
                m_new = m_cur
            else:
                m_old = [m_ref[hh, blocks[b][0], :] for b, hh in heads]
                m_new = [jnp.maximum(m_o, m_c) for m_o, m_c in zip(m_old, m_cur)]
            p_all = [jnp.exp2(s - jnp.concatenate([m_n] * (win // LANES), axis=1)) for s, m_n in zip(s_all, m_new)]
            pv_all = [_dot(_bf(p), v) for p, v in zip(p_all, v_ext)]
            for idx, (b, hh) in enumerate(heads):
                rq = blocks[b][0]
                if first:
                    a_ref[hh, rq, :] = pv_all[idx]
                else:
                    a_ref[hh, rq, :] = jnp.exp2(m_old[idx] - m_new[idx]) * a_ref[hh, rq, :] + pv_all[idx]
                m_ref[hh, rq, :] = m_new[idx]
            return c
        lax.fori_loop(0, dil * nblk // SWA_UNROLL, body, 0)

    for s_idx, p_idx in enumerate(SWA_ORDER):
        pl.when(step == s_idx)(functools.partial(run_pattern, SWA_PATTERNS[p_idx][1], s_idx == 0))

    @pl.when(step == SWA_N_PAT - 1)
    def _():
        lo = lax.broadcasted_iota(jnp.int32, (ROW_CHUNK, LANES), 1) < SWA_HEAD_DIM

        def fin(i, c):
            rows = [pl.ds(pl.multiple_of((i * SWA_FIN_UNROLL + k) * ROW_CHUNK, ROW_CHUNK), ROW_CHUNK)
                    for k in range(SWA_FIN_UNROLL)]
            o = [jnp.where(lo, a_ref[0, r, :], a_ref[1, r, :])
                 / pltpu.roll(jnp.where(lo, a_ref[1, r, :], a_ref[0, r, :]), SWA_HEAD_DIM, 1) for r in rows]
            sq = [x * x for x in o]
            hi = [_bf(x) for x in sq]
            rest = [_bf(x - h.astype(jnp.float32)) for x, h in zip(sq, hi)]
            ms = [(_dot(h, gm_ref[...]) + _dot(rs, gm_ref[...])) * (1.0 / SWA_HEAD_DIM)
                  for h, rs in zip(hi, rest)]
            for r, x, m in zip(rows, o, ms):
                o_ref[r, :] = _bf(x * lax.rsqrt(m + EPS) * ng_ref[...])
            return c
        lax.fori_loop(0, nrow // SWA_FIN_UNROLL, fin, 0)


def _swa(proj, cos_tab, sin_tab, rot_mat, norm_c_row, group_mat, layer, batch, seq):
    base = (OFF_SWA - MIXA_COLS) // LANES
    per_pat = SWA_WIDTH // LANES
    qkv = SWA_QKV_WIDTH // LANES
    pairs = SWA_HEADS // 2
    seq_buf = pltpu.VMEM((seq, LANES), jnp.float32)
    first, stride = SWA_ORDER[0], SWA_ORDER[1] - SWA_ORDER[0]
    assert tuple(first + p * stride for p in range(SWA_N_PAT)) == SWA_ORDER
    assert CONV_CH // CONV_GROUPS == SWA_HEAD_DIM

    def spec(which):
        return pl.BlockSpec((seq, LANES),
                            lambda b, hp, p: (b, base + which * qkv + (first + p * stride) * per_pat + hp))

    return pl.pallas_call(
        _swa_kernel,
        grid=(batch, pairs, SWA_N_PAT),
        in_specs=[
            spec(0), spec(1), spec(2),
            pl.BlockSpec((seq, LANES), lambda b, hp, p: (b, 0)),
            pl.BlockSpec((seq, LANES), lambda b, hp, p: (b, 0)),
            pl.BlockSpec((2 * LANES, 2 * LANES), lambda b, hp, p: (0, 0)),
            pl.BlockSpec((None, 1, LANES), lambda b, hp, p: (layer, 0, hp)),
            pl.BlockSpec((LANES, LANES), lambda b, hp, p: (0, 0)),
        ],
        out_specs=pl.BlockSpec((seq, LANES), lambda b, hp, p: (b, hp)),
        out_shape=jax.ShapeDtypeStruct((batch * seq, SWA_WIDTH), MIX_DTYPE),
        scratch_shapes=([seq_buf] * 2 + [pltpu.VMEM((2, seq, LANES), jnp.float32)] * 2
                        + [pltpu.VMEM((3, SWA_QB, SWA_QB + 2 * SWA_RADIUS), jnp.float32)]),
        compiler_params=_cparams(("arbitrary", "arbitrary", "arbitrary")),
        name="swa",
    )(proj, proj, proj, cos_tab, sin_tab, rot_mat, norm_c_row, group_mat)


def _outproj_kernel(x_ref, ya_ref, yb_ref, yc_ref, w_ref, o_ref):
    a0, a1, a2 = CONV_CH, CONV_CH + DN_WIDTH, MIX_WIDTH
    acc = _dot(ya_ref[...], w_ref[0:a0, :])
    acc = acc + _dot(yb_ref[...], w_ref[a0:a1, :])
    acc = acc + _dot(yc_ref[...], w_ref[a1:a2, :])
    o_ref[...] = x_ref[...] + acc


def _out_proj(x2d, y_a, y_b, y_c, w_bf, layer):
    n = x2d.shape[0]
    tm = OUT_TM
    return pl.pallas_call(
        _outproj_kernel,
        grid=(n // tm,),
        in_specs=[
            pl.BlockSpec((tm, D_MODEL), lambda i: (i, 0)),
            pl.BlockSpec((tm, CONV_CH), lambda i: (i, 0)),
            pl.BlockSpec((tm, DN_WIDTH), lambda i: (i, 0)),
            pl.BlockSpec((tm, SWA_WIDTH), lambda i: (i, 0)),
            pl.BlockSpec((None, MIX_WIDTH, D_MODEL), lambda i: (layer, 0, 0)),
        ],
        out_specs=pl.BlockSpec((tm, D_MODEL), lambda i: (i, 0)),
        out_shape=jax.ShapeDtypeStruct((n, D_MODEL), jnp.float32),
        compiler_params=_cparams(("arbitrary",)),
        name="out_proj",
    )(x2d, y_a, y_b, y_c, w_bf)


def _ffn_kernel(xp_ref, x_ref, xn_ref, g_ref, wu_ref, cf_ref, wd_ref, gf_ref, o_ref,
                h_ref, a_ref, *, tiles_per_seq, final_norm):
    tm = x_ref.shape[0]
    t = pl.program_id(0) % tiles_per_seq

    def norm(x):
        return x * lax.rsqrt(jnp.mean(x * x, axis=-1, keepdims=True) + EPS) * g_ref[...]

    halo = HALO_BF16
    hp = norm(xp_ref[...])
    hn = norm(xn_ref[...])
    h_ref[0:halo, :] = _bf(jnp.where(t == 0, 0.0, hp))
    h_ref[halo + tm:halo + tm + halo, :] = _bf(jnp.where(t == tiles_per_seq - 1, 0.0, hn))

    def fill(i, c):
        rows = pl.ds(pl.multiple_of(i * ROW_CHUNK, ROW_CHUNK), ROW_CHUNK)
        h_ref[pl.ds(pl.multiple_of(i * ROW_CHUNK + halo, halo), ROW_CHUNK), :] = _bf(norm(x_ref[rows, :]))
        return c
    lax.fori_loop(0, tm // ROW_CHUNK, fill, 0)

    for start in range(0, D_FF, FF_CHUNK):
        h = h_ref[...]
        stop = min(start + FF_CHUNK, D_FF)
        gcols = slice(start, stop)
        vcols = slice(D_FF + start, D_FF + stop)
        gate = _conv3_val(_dot(h, wu_ref[:, gcols]), cf_ref[:, gcols], tm, halo)
        val = _conv3_val(_dot(h, wu_ref[:, vcols]), cf_ref[:, vcols], tm, halo)
        a_ref[:, gcols] = _bf(_silu(gate) * val)
    o_ref[...] = _dot(a_ref[...], wd_ref[...])

    def fin(i, c):
        rows = pl.ds(pl.multiple_of(i * ROW_CHUNK, ROW_CHUNK), ROW_CHUNK)
        y = x_ref[rows, :] + o_ref[rows, :]
        if final_norm:
            y = y * lax.rsqrt(jnp.mean(y * y, axis=-1, keepdims=True) + EPS) * gf_ref[...]
        o_ref[rows, :] = y
        return c
    lax.fori_loop(0, tm // ROW_CHUNK, fin, 0)


def _ffn(x2d, g_row, wu, cf, wd, gf_row, layer, seq, final_norm):
    n = x2d.shape[0]
    tm = FFN_TM
    tiles_per_seq = seq // tm
    halo = HALO_BF16
    hb = tm // halo
    nhb = n // halo
    whole = lambda shape: pl.BlockSpec((None,) + shape, lambda i: (layer,) + (0,) * len(shape),
                                       pipeline_mode=pl.Buffered(1))
    kern = functools.partial(_ffn_kernel, tiles_per_seq=tiles_per_seq, final_norm=final_norm)
    return pl.pallas_call(
        kern,
        grid=(n // tm,),
        in_specs=[
            pl.BlockSpec((halo, D_MODEL), lambda i: (jnp.maximum(i * hb - 1, 0), 0)),
            pl.BlockSpec((tm, D_MODEL), lambda i: (i, 0)),
            pl.BlockSpec((halo, D_MODEL), lambda i: (jnp.minimum((i + 1) * hb, nhb - 1), 0)),
            whole((1, D_MODEL)),
            whole((D_MODEL, 2 * D_FF)),
            whole((3, 2 * D_FF)),
            whole((D_FF, D_MODEL)),
            pl.BlockSpec((1, D_MODEL), lambda i: (0, 0)),
        ],
        out_specs=pl.BlockSpec((tm, D_MODEL), lambda i: (i, 0)),
        out_shape=jax.ShapeDtypeStruct((n, D_MODEL), jnp.float32),
        scratch_shapes=[
            pltpu.VMEM((tm + 2 * halo, D_MODEL), jnp.bfloat16),
            pltpu.VMEM((tm, D_FF), jnp.bfloat16),
        ],
        compiler_params=_cparams(("arbitrary",)),
        name="ffn",
    )(x2d, x2d, x2d, g_row, wu, cf, wd, gf_row)


def _win_prep_kernel(w_ref, o_ref):
    n_small = 4 * DN_HEADS
    n_swa = 3 * SWA_QKV_WIDTH
    o_ref[:, 0:OFF_SWA] = _bf(w_ref[:, 0:OFF_SWA])
    o_ref[:, OFF_SWA:OFF_SMALL] = _bf(w_ref[:, OFF_SWA + n_small:OFF_SWA + n_small + n_swa])
    o_ref[:, OFF_SMALL:OFF_SMALL + LANES] = _bf(jnp.concatenate(
        [w_ref[:, OFF_SWA:OFF_SWA + n_small], jnp.zeros((w_ref.shape[0], LANES - n_small), jnp.float32)], axis=1))
    o_ref[:, OFF_SMALL + LANES:] = jnp.zeros((w_ref.shape[0], IN_COLS - OFF_SMALL - LANES), jnp.bfloat16)


def _reorder_w_in(w):
    depth, rows, cols = w.shape
    return pl.pallas_call(
        _win_prep_kernel,
        grid=(depth, rows // ROW_CHUNK),
        in_specs=[pl.BlockSpec((None, ROW_CHUNK, cols), lambda l, i: (l, i, 0))],
        out_specs=pl.BlockSpec((None, ROW_CHUNK, IN_COLS), lambda l, i: (l, i, 0)),
        out_shape=jax.ShapeDtypeStruct((depth, rows, IN_COLS), jnp.bfloat16),
        compiler_params=_cparams(("arbitrary", "arbitrary")),
        name="w_in_prep",
    )(w)


def _small_rows(f_vals, b_vals):
    row = jnp.concatenate([f_vals, b_vals], axis=-1).astype(jnp.float32)
    return jnp.pad(row, ((0, 0), (SMALL_AF, LANES - SMALL_AF - row.shape[-1])))[:, None, :]


def kernel(x, positions, norm_mix, w_in, conv_a, norm_a, conv_qkv, a_log_f, a_log_b, dt_bias_f, dt_bias_b,
           norm_dn, norm_c, w_o, norm_ffn, w_up, conv_ffn, w_down, norm_final):
    batch, seq, _ = x.shape
    depth = w_in.shape[0]
    x2d = x.reshape(batch * seq, D_MODEL)
    pos_col = positions.reshape(batch * seq, 1)

    half = ROPE_DIM // 2
    inv_freq = ROPE_THETA ** (-jnp.arange(half, dtype=jnp.float32) / half)
    d_idx = np.arange(LANES) % SWA_HEAD_DIM
    invf_row = jnp.where(jnp.asarray(d_idx < ROPE_DIM), inv_freq[d_idx % half], 0.0).reshape(1, LANES)
    rot_np = np.zeros((2 * LANES, 2 * LANES), np.float32)
    for lane_idx in range(2 * LANES):
        if d_idx[lane_idx % LANES] < half:
            rot_np[lane_idx + half, lane_idx] = -1.0
        elif d_idx[lane_idx % LANES] < ROPE_DIM:
            rot_np[lane_idx - half, lane_idx] = 1.0
    rot_mat = jnp.asarray(rot_np, jnp.bfloat16)
    grp = np.arange(CONV_CH) // (CONV_CH // CONV_GROUPS)
    group_mat = _bf(jnp.asarray(grp[:, None] == grp[None, :], jnp.float32))

    rows = lambda a: a[:, None, :]
    w_in_bf, w_o_bf, w_up_bf, w_down_bf = _reorder_w_in(w_in), _bf(w_o), _bf(w_up), _bf(w_down)
    alog_rows, dtb_rows = _small_rows(a_log_f, a_log_b), _small_rows(dt_bias_f, dt_bias_b)
    cos_tab, sin_tab = _rope_tables(pos_col, invf_row, batch, seq)
    for l in range(depth):
        proj, y_a = _proj_in(x2d, rows(norm_mix), w_in_bf, conv_a, rows(norm_a), group_mat, l, batch, seq)
        y_b = _deltanet(proj, conv_qkv, alog_rows, dtb_rows, rows(norm_dn), l, batch, seq)
        y_c = _swa(proj, cos_tab, sin_tab, rot_mat, rows(norm_c), group_mat, l, batch, seq)
        x2d = _out_proj(x2d, y_a, y_b, y_c, w_o_bf, l)
        x2d = _ffn(x2d, rows(norm_ffn), w_up_bf, conv_ffn, w_down_bf, norm_final.reshape(1, D_MODEL), l, seq,
                   l == depth - 1)
    return x2d.reshape(batch, seq, D_MODEL)
```

```python
import functools

import numpy as np
import jax
import jax.numpy as jnp
from jax import lax
from jax.experimental import pallas as pl
from jax.experimental.pallas import tpu as pltpu

D_MODEL = 1024
EPS = 1e-6
NEG_INF = -1e30
CONV_CH = 256
CONV_GROUPS = 4
DN_HEADS = 4
DN_HEAD_DIM = 128
DN_WIDTH = DN_HEADS * DN_HEAD_DIM
SWA_HEAD_DIM = 64
SWA_HEADS = 4
SWA_PATTERNS = ((128, 1), (512, 4), (2048, 16))
SWA_N_PAT = 3
SWA_WIDTH = SWA_HEADS * SWA_HEAD_DIM
SWA_QKV_WIDTH = SWA_N_PAT * SWA_WIDTH
ROPE_THETA = 500000.0
ROPE_DIM = SWA_HEAD_DIM // 4
D_FF = 2816
MIX_WIDTH = CONV_CH + DN_WIDTH + SWA_WIDTH

MIX_DTYPE = jnp.bfloat16
LANES = 128
SUBLANES = 8
VMEM_LIMIT = 56 * 1024 * 1024

OFF_A = 0
MIXA_COLS = 3 * CONV_CH
OFF_DN_QKV = OFF_A + MIXA_COLS
OFF_DN_GATE = OFF_DN_QKV + 3 * DN_WIDTH
OFF_SWA = OFF_DN_GATE + DN_WIDTH
OFF_SMALL = OFF_SWA + 3 * SWA_QKV_WIDTH
IN_COLS = 5376
PROJ_CN = 768
SMALL_BF, SMALL_BB, SMALL_AF, SMALL_AB = 0, 4, 8, 12

DN_CHUNK = 256
DN_INTRA_UNROLL = 4
DN_GATES_UNROLL = 4
ROW_CHUNK = 256
HALO = SUBLANES
HALO_BF16 = 2 * SUBLANES
FF_CHUNK = 512
FFN_TM = 1024
OUT_TM = 2048
SWA_QB = 128
SWA_RADIUS = 64
SWA_UNROLL = 8
ROPE_ROWS = 512
SWA_FIN_UNROLL = 4
SWA_Q_SCALE = SWA_HEAD_DIM ** -0.5 * float(np.log2(np.e))
SWA_ORDER = (2, 1, 0)


def _cparams(sem, vmem_limit=VMEM_LIMIT):
    return pltpu.CompilerParams(dimension_semantics=sem, vmem_limit_bytes=vmem_limit)


def _bf(x):
    return x.astype(jnp.bfloat16)


def _dot(a, b):
    return jnp.dot(a, b, preferred_element_type=jnp.float32)


def _dot_nt(a, b):
    return lax.dot_general(a, b, (((1,), (1,)), ((), ())), preferred_element_type=jnp.float32)


def _silu(x):
    return x * (1.0 / (1.0 + jnp.exp(-x)))


def _conv3_val(ext, w, rows, halo):
    n = ext.shape[0]
    prev = pltpu.roll(ext, 1, 0)[halo:halo + rows]
    nxt = pltpu.roll(ext, n - 1, 0)[halo:halo + rows]
    return prev * w[0:1] + ext[halo:halo + rows] * w[1:2] + nxt * w[2:3]


def _conv3_ref(ref, lead, start, rows, w):
    def tap(off):
        return ref[lead + (pl.ds(start + off, rows), slice(None))]
    return tap(-1) * w[0:1] + tap(0) * w[1:2] + tap(1) * w[2:3]


def _proj_kernel(x_ref, g_ref, w_ref, cw_ref, ng_ref, gm_ref, o_ref, ya_ref, h_ref, a_ref, u_ref):
    col = pl.program_id(1)

    @pl.when(col == 0)
    def _():
        def body(i, c):
            rows = pl.ds(pl.multiple_of(i * ROW_CHUNK, ROW_CHUNK), ROW_CHUNK)
            x = x_ref[rows, :]
            y = x * lax.rsqrt(jnp.mean(x * x, axis=-1, keepdims=True) + EPS)
            h_ref[rows, :] = _bf(y * g_ref[...])
            return c
        lax.fori_loop(0, x_ref.shape[0] // ROW_CHUNK, body, 0)
        a_ref[...] = _dot(h_ref[...], w_ref[...])
        _mixa_body(a_ref, cw_ref, ng_ref, gm_ref, ya_ref, u_ref)

    @pl.when(col > 0)
    def _():
        o_ref[...] = _dot(h_ref[...], w_ref[...])


def _proj_in(x2d, g_row, w_bf, conv_a, norm_a_row, group_mat, layer, batch, seq):
    ncol = IN_COLS // PROJ_CN
    assert PROJ_CN == MIXA_COLS
    return pl.pallas_call(
        _proj_kernel,
        grid=(batch, ncol),
        in_specs=[
            pl.BlockSpec((seq, D_MODEL), lambda b, j: (b, 0)),
            pl.BlockSpec((None, 1, D_MODEL), lambda b, j: (layer, 0, 0)),
            pl.BlockSpec((None, D_MODEL, PROJ_CN), lambda b, j: (layer, 0, j)),
            pl.BlockSpec((None, 3, CONV_CH), lambda b, j: (layer, 0, 0)),
            pl.BlockSpec((None, 1, CONV_CH), lambda b, j: (layer, 0, 0)),
            pl.BlockSpec((CONV_CH, CONV_CH), lambda b, j: (0, 0)),
        ],
        out_specs=[pl.BlockSpec((seq, PROJ_CN), lambda b, j: (b, jnp.maximum(j - 1, 0))),
                   pl.BlockSpec((seq, CONV_CH), lambda b, j: (b, 0))],
        out_shape=[jax.ShapeDtypeStruct((batch * seq, IN_COLS - MIXA_COLS), jnp.float32),
                   jax.ShapeDtypeStruct((batch * seq, CONV_CH), MIX_DTYPE)],
        scratch_shapes=[pltpu.VMEM((seq, D_MODEL), jnp.bfloat16),
                        pltpu.VMEM((seq, MIXA_COLS), jnp.float32),
                        pltpu.VMEM((seq + 2 * HALO, CONV_CH), jnp.float32)],
        compiler_params=_cparams(("arbitrary", "arbitrary")),
        name="proj_in",
    )(x2d, g_row, w_bf, conv_a, norm_a_row, group_mat)


def _mixa_body(p_ref, cw_ref, ng_ref, gm_ref, o_ref, u_ref):
    seq = o_ref.shape[0]
    nchunk = seq // ROW_CHUNK
    zeros = jnp.zeros((HALO, CONV_CH), jnp.float32)
    u_ref[0:HALO, :] = zeros
    u_ref[HALO + seq:HALO + seq + HALO, :] = zeros

    def fill(i, c):
        rows = pl.ds(pl.multiple_of(i * ROW_CHUNK, ROW_CHUNK), ROW_CHUNK)
        xa = p_ref[rows, 0:CONV_CH]
        gc = p_ref[rows, 2 * CONV_CH:3 * CONV_CH]
        u_ref[pl.ds(pl.multiple_of(i * ROW_CHUNK + HALO, HALO), ROW_CHUNK), :] = gc * xa
        return c
    lax.fori_loop(0, nchunk, fill, 0)

    for i in range(nchunk):
        start = i * ROW_CHUNK
        gb = p_ref[start:start + ROW_CHUNK, CONV_CH:2 * CONV_CH]
        y = gb * _conv3_ref(u_ref, (), start + HALO, ROW_CHUNK, cw_ref[...])
        sq = y * y
        hi = _bf(sq)
        lo = _bf(sq - hi.astype(jnp.float32))
        ms = (_dot(hi, gm_ref[...]) + _dot(lo, gm_ref[...])) * (1.0 / (CONV_CH // CONV_GROUPS))
        o_ref[start:start + ROW_CHUNK, :] = _bf(y * lax.rsqrt(ms + EPS) * ng_ref[...])


def _tri_inverse_group(a_list, lowers, xor_ij, eye):
    n = a_list[0].shape[0]
    d_list = [eye - jnp.where(xor_ij == 1, a, 0.0) for a in a_list]
    level = 1
    while (1 << level) < n:
        s = 1 << level
        mask = (xor_ij >> level) == 1
        e_list = [_bf(jnp.where(mask, a, 0.0)) for a in a_list]
        d_bf = [_bf(d) for d in d_list]
        if s < SUBLANES:
            de_list = [_dot(db, e) for db, e in zip(d_bf, e_list)]
            d_list = [d - _dot(_bf(de), db) for d, de, db in zip(d_list, de_list, d_bf)]
        else:
            blocks = [[d[b * s:(b + 1) * s] for b in range(n // s)] for d in d_list]
            picks = [range(1, n // s, 2) if lower else range(0, n // s, 2) for lower in lowers]
            sel = [_bf(jnp.concatenate([blk[b] for b in pk], axis=0)) for blk, pk in zip(blocks, picks)]
            de_list = [_dot(x, e) for x, e in zip(sel, e_list)]
            upd = [_dot(_bf(de), db) for de, db in zip(de_list, d_bf)]
            d_list = []
            for blk, pk, u in zip(blocks, picks, upd):
                for k, b in enumerate(pk):
                    blk[b] = blk[b] - u[k * s:(k + 1) * s]
                d_list.append(jnp.concatenate(blk, axis=0))
        level += 1
    return d_list


def _dn_intra_group(items, ii, jj, xor_ij, eye):
    c = items[0][0].shape[0]
    wide = lambda col: jnp.concatenate([col] * (c // LANES), axis=1)
    grams = []
    for q, k in [(item[0], item[1]) for item in items[::2]]:
        k_bf = _bf(k)
        grams.append((_dot_nt(k_bf, k_bf), _dot_nt(_bf(q), k_bf)))
    pre = []
    for idx, (q, k, v, beta, gcol, grow, lower) in enumerate(items):
        if lower:
            incl, strict, glast = ii >= jj, ii > jj, gcol[c - 1:c, :]
        else:
            incl, strict, glast = ii <= jj, ii < jj, gcol[0:1, :]
        kk, qk_raw = grams[idx // 2]
        dec = jnp.where(incl, jnp.exp(jnp.where(incl, wide(gcol) - grow, 0.0)), 0.0)
        a = jnp.where(strict, kk * wide(beta) * dec, 0.0)
        pre.append((a, _bf(qk_raw * dec), k * beta, glast))
    lhs_list = [jnp.concatenate([_bf((item[1] * jnp.exp(p[3] - item[4])).T), p[1]], axis=0)
                for item, p in zip(items, pre)]
    tinv_list = _tri_inverse_group([p[0] for p in pre], [item[6] for item in items], xor_ij, eye)
    hd = DN_HEAD_DIM
    eg_list = [jnp.exp(item[4]) for item in items]
    sol_list = [_bf(_dot(_bf(tinv), _bf(jnp.concatenate([item[2] * item[3], p[2] * eg], axis=1))))
                for item, p, eg, tinv in zip(items, pre, eg_list, tinv_list)]
    r_list = [_dot(lhs, sol) for lhs, sol in zip(lhs_list, sol_list)]
    return [(r[:hd, :hd], _bf(r[:hd, hd:]), r[hd:, :hd], _bf(item[0] * eg - r[hd:, hd:]), jnp.exp(p[3]))
            for item, p, eg, r in zip(items, pre, eg_list, r_list)]


def _dn_kernel(q_ref, k_ref, v_ref, gate_ref, small_ref, cq_ref, ck_ref, cv_ref, alog_ref, dtb_ref,
               ng_ref, o_ref,
               pad_ref, qn_ref, kn_ref, vn_ref, bf_ref, bb_ref, gf_ref, gb_ref, rf_ref, rb_ref,
               of_ref, ob_ref, b_ref, p_ref, qp_ref, gl_ref, beta_ref, cf_ref, cb_ref):
    seq = o_ref.shape[0]
    head = pl.program_id(1)
    nrow = seq // ROW_CHUNK
    nchunk = seq // DN_CHUNK

    lane = lax.broadcasted_iota(jnp.int32, (DN_CHUNK, LANES), 1)
    ii = lax.broadcasted_iota(jnp.int32, (DN_CHUNK, DN_CHUNK), 0)
    jj = lax.broadcasted_iota(jnp.int32, (DN_CHUNK, DN_CHUNK), 1)
    tri2 = jnp.concatenate([jnp.where(ii >= jj, 1.0, 0.0), jnp.where(ii <= jj, 1.0, 0.0)],
                           axis=0).astype(jnp.bfloat16)

    def pick(x, col):
        sel = jnp.sum(jnp.where(lane == col, x, 0.0), axis=-1, keepdims=True)
        return jnp.broadcast_to(sel, x.shape)

    @pl.when(head == 0)
    def _():
        neg_a = -jnp.exp(alog_ref[...])

        def gates(i, c):
            cis = [DN_GATES_UNROLL * i + k for k in range(DN_GATES_UNROLL)]
            rows = [pl.ds(pl.multiple_of(ci * DN_CHUNK, DN_CHUNK), DN_CHUNK) for ci in cis]
            sm = [small_ref[r, :] for r in rows]
            z = [s + dtb_ref[...] for s in sm]
            g = [neg_a * (jnp.maximum(zz, 0.0) + jnp.log1p(jnp.exp(-jnp.abs(zz)))) for zz in z]
            g1 = [_bf(x) for x in g]
            r1 = [x - x1.astype(jnp.float32) for x, x1 in zip(g, g1)]
            g2 = [_bf(x) for x in r1]
            g3 = [_bf(x - x2.astype(jnp.float32)) for x, x2 in zip(r1, g2)]
            r = [_dot(tri2, jnp.concatenate(parts, axis=1)) for parts in zip(g1, g2, g3)]
            cf = [x[:DN_CHUNK, 2 * LANES:] + x[:DN_CHUNK, LANES:2 * LANES] + x[:DN_CHUNK, :LANES] for x in r]
            cb = [x[DN_CHUNK:, 2 * LANES:] + x[DN_CHUNK:, LANES:2 * LANES] + x[DN_CHUNK:, :LANES] for x in r]
            for k in range(DN_GATES_UNROLL):
                beta_ref[rows[k], :] = 1.0 / (1.0 + jnp.exp(-sm[k]))
                cf_ref[rows[k], :] = cf[k]
                cb_ref[rows[k], :] = cb[k]
                rf_ref[cis[k]] = cf[k].T
                rb_ref[cis[k]] = cb[k].T
            return c
        lax.fori_loop(0, nchunk // DN_GATES_UNROLL, gates, 0)

    assert DN_CHUNK == ROW_CHUNK
    zeros = jnp.zeros((HALO, DN_HEAD_DIM), jnp.float32)
    pad_ref[0:HALO, :] = zeros
    pad_ref[HALO + seq:HALO + seq + HALO, :] = zeros

    def conv_silu(src_ref, cw_ref, dst_ref, mode, picks):
        def fill(i, c):
            rows = pl.ds(pl.multiple_of(i * ROW_CHUNK, ROW_CHUNK), ROW_CHUNK)
            pad_ref[pl.ds(pl.multiple_of(i * ROW_CHUNK + HALO, HALO), ROW_CHUNK), :] = src_ref[rows, :]
            return c
        lax.fori_loop(0, nrow, fill, 0)

        for i in range(nrow):
            start = i * ROW_CHUNK
            rows = slice(start, start + ROW_CHUNK)
            y = _silu(_conv3_ref(pad_ref, (), start + HALO, ROW_CHUNK, cw_ref[...]))
            if mode != "v":
                y = y * lax.rsqrt(jnp.sum(y * y, axis=-1, keepdims=True) + EPS)
            if mode == "q":
                y = y * (DN_HEAD_DIM ** -0.5)
            dst_ref[rows, :] = y
            for src, col, dst in picks:
                dst[rows, :] = pick(src[rows, :], col + head)

    conv_silu(q_ref, cq_ref, qn_ref, "q", [(beta_ref, SMALL_BF, bf_ref), (beta_ref, SMALL_BB, bb_ref)])
    conv_silu(k_ref, ck_ref, kn_ref, "k", [(cf_ref, SMALL_AF, gf_ref)])
    conv_silu(v_ref, cv_ref, vn_ref, "v", [(cb_ref, SMALL_AB, gb_ref)])

    xor_ij = ii ^ jj
    eye = jnp.where(ii == jj, 1.0, 0.0)

    def intra(i, c):
        items, where = [], []
        for sub in range(DN_INTRA_UNROLL):
            ci = i * DN_INTRA_UNROLL + sub
            rows = pl.ds(pl.multiple_of(ci * DN_CHUNK, DN_CHUNK), DN_CHUNK)
            for direction in (0, 1):
                if direction == 0:
                    beta, gcol = bf_ref[rows, :], gf_ref[rows, :]
                    grow = rf_ref[ci, pl.ds(SMALL_AF + head, 1), :]
                else:
                    beta, gcol = bb_ref[rows, :], gb_ref[rows, :]
                    grow = rb_ref[ci, pl.ds(SMALL_AB + head, 1), :]
                items.append((qn_ref[rows, :], kn_ref[rows, :], vn_ref[rows, :], beta, gcol, grow, direction == 0))
                where.append((direction, ci, rows))
        outs = (of_ref, ob_ref)
        for (direction, ci, rows), (b, p, o_part, qp, gl) in zip(where, _dn_intra_group(items, ii, jj, xor_ij, eye)):
            b_ref[direction, ci] = b
            p_ref[direction, ci] = p
            outs[direction][rows, :] = o_part
            qp_ref[direction, rows, :] = qp
            gl_ref[direction, ci] = jnp.broadcast_to(gl, (SUBLANES, LANES))
        return c
    lax.fori_loop(0, nchunk // DN_INTRA_UNROLL, intra, 0)

    def fin(ci):
        rows = slice(ci * DN_CHUNK, (ci + 1) * DN_CHUNK)
        o = of_ref[rows, :] + ob_ref[rows, :]
        o = o * lax.rsqrt(jnp.mean(o * o, axis=-1, keepdims=True) + EPS) * ng_ref[...]
        o_ref[rows, :] = _bf(o * _silu(gate_ref[rows, :]))

    state = [jnp.zeros((DN_HEAD_DIM, DN_HEAD_DIM), jnp.float32)] * 2
    for step in range(nchunk):
        cis = (step, nchunk - 1 - step)
        rows = [slice(ci * DN_CHUNK, (ci + 1) * DN_CHUNK) for ci in cis]
        s_bf = [_bf(s) for s in state]
        s_new = [state[d] * gl_ref[d, cis[d], 0:1, :] + b_ref[d, cis[d]] - _dot(p_ref[d, cis[d]], s_bf[d])
                 for d in (0, 1)]
        o_add = [_dot(qp_ref[d, rows[d], :], s_bf[d]) for d in (0, 1)]
        for d, out_ref in ((0, of_ref), (1, ob_ref)):
            out_ref[rows[d], :] += o_add[d]
        state = s_new
        for ci in range(nchunk):
            if max(ci, nchunk - 1 - ci) == step:
                fin(ci)


def _deltanet(proj, conv_qkv, alog_row, dtb_row, norm_dn_row, layer, batch, seq):
    nchunk = seq // DN_CHUNK
    hd = DN_HEAD_DIM
    qb, gb, sb = ((off - MIXA_COLS) // hd for off in (OFF_DN_QKV, OFF_DN_GATE, OFF_SMALL))
    seq_buf = pltpu.VMEM((seq, hd), jnp.float32)
    return pl.pallas_call(
        _dn_kernel,
        grid=(batch, DN_HEADS),
        in_specs=[
            pl.BlockSpec((seq, hd), lambda b, h: (b, qb + h)),
            pl.BlockSpec((seq, hd), lambda b, h: (b, qb + DN_HEADS + h)),
            pl.BlockSpec((seq, hd), lambda b, h: (b, qb + 2 * DN_HEADS + h)),
            pl.BlockSpec((seq, hd), lambda b, h: (b, gb + h)),
            pl.BlockSpec((seq, LANES), lambda b, h: (b, sb)),
            pl.BlockSpec((None, 3, hd), lambda b, h: (layer, 0, h)),
            pl.BlockSpec((None, 3, hd), lambda b, h: (layer, 0, DN_HEADS + h)),
            pl.BlockSpec((None, 3, hd), lambda b, h: (layer, 0, 2 * DN_HEADS + h)),
            pl.BlockSpec((None, 1, LANES), lambda b, h: (layer, 0, 0)),
            pl.BlockSpec((None, 1, LANES), lambda b, h: (layer, 0, 0)),
            pl.BlockSpec((None, 1, hd), lambda b, h: (layer, 0, 0)),
        ],
        out_specs=pl.BlockSpec((seq, hd), lambda b, h: (b, h)),
        out_shape=jax.ShapeDtypeStruct((batch * seq, DN_WIDTH), MIX_DTYPE),
        scratch_shapes=[
            pltpu.VMEM((seq + 2 * HALO, hd), jnp.float32),
            seq_buf, seq_buf, seq_buf,
            seq_buf, seq_buf, seq_buf, seq_buf,
            pltpu.VMEM((nchunk, LANES, DN_CHUNK), jnp.float32),
            pltpu.VMEM((nchunk, LANES, DN_CHUNK), jnp.float32),
            seq_buf, seq_buf,
            pltpu.VMEM((2, nchunk, hd, hd), jnp.float32),
            pltpu.VMEM((2, nchunk, hd, hd), jnp.bfloat16),
            pltpu.VMEM((2, seq, hd), jnp.bfloat16),
            pltpu.VMEM((2, nchunk, SUBLANES, LANES), jnp.float32),
            seq_buf, seq_buf, seq_buf,
        ],
        compiler_params=_cparams(("arbitrary", "arbitrary")),
        name="deltanet",
    )(proj, proj, proj, proj, proj, conv_qkv, conv_qkv, conv_qkv, alog_row, dtb_row, norm_dn_row)


def _rope_table_kernel(pos_ref, invf_ref, cos_ref, sin_ref):
    def body(i, c):
        rows = pl.ds(pl.multiple_of(i * ROW_CHUNK, ROW_CHUNK), ROW_CHUNK)
        ang = pos_ref[rows, :].astype(jnp.float32) * invf_ref[...]
        cos_ref[rows, :] = jnp.cos(ang)
        sin_ref[rows, :] = jnp.sin(ang)
        return c
    lax.fori_loop(0, cos_ref.shape[0] // ROW_CHUNK, body, 0)


def _rope_tables(pos_col, invf_row, batch, seq):
    table = jax.ShapeDtypeStruct((batch * seq, LANES), jnp.float32)
    return pl.pallas_call(
        _rope_table_kernel,
        grid=(batch,),
        in_specs=[pl.BlockSpec((seq, 1), lambda b: (b, 0)), pl.BlockSpec((1, LANES), lambda b: (0, 0))],
        out_specs=[pl.BlockSpec((seq, LANES), lambda b: (b, 0))] * 2,
        out_shape=[table, table],
        compiler_params=_cparams(("arbitrary",)),
        name="rope_tables",
    )(pos_col, invf_row)


def _swa_kernel(q_ref, k_ref, v_ref, cos_ref, sin_ref, rot_ref, ng_ref, gm_ref, o_ref,
                qr_ref, kr_ref, m_ref, a_ref, bias_ref):
    seq = o_ref.shape[0]
    step = pl.program_id(2)
    nrow = seq // ROW_CHUNK

    chunks = [slice(i * ROPE_ROWS, (i + 1) * ROPE_ROWS) for i in range(seq // ROPE_ROWS)]
    xs = [jnp.concatenate([q_ref[r, :], k_ref[r, :]], axis=1) for r in chunks]
    xrs = [_dot(_bf(x), rot_ref[...]) for x in xs]
    for r, x, xr in zip(chunks, xs, xrs):
        cs, sn = cos_ref[r, :], sin_ref[r, :]
        y = x * jnp.concatenate([cs, cs], axis=1) + xr * jnp.concatenate([sn, sn], axis=1)
        qr_ref[r, :] = y[:, :LANES] * SWA_Q_SCALE
        kr_ref[r, :] = y[:, LANES:]

    def run_pattern(dil, first):
        sub_len = seq // dil
        nblk = sub_len // SWA_QB
        win = min(sub_len, SWA_QB + 2 * SWA_RADIUS)
        lane_q = lax.broadcasted_iota(jnp.int32, (SWA_QB, LANES), 1)
        head_lo = lane_q < SWA_HEAD_DIM
        lane_w = lax.broadcasted_iota(jnp.int32, (win, LANES), 1)
        own_v = (lane_w < SWA_HEAD_DIM, lane_w >= SWA_HEAD_DIM)
        dij = (lax.broadcasted_iota(jnp.int32, (SWA_QB, win), 0)
               - lax.broadcasted_iota(jnp.int32, (SWA_QB, win), 1))
        for k in range(3 if sub_len > win else 1):
            bias_ref[k, :, 0:win] = jnp.where(jnp.abs(dij + k * SWA_RADIUS) <= SWA_RADIUS, 0.0, NEG_INF)

        def sl(start, size):
            return pl.ds(start, size) if dil == 1 else pl.ds(start, size, stride=dil)

        def body(i, c):
            blocks = []
            for sub in range(SWA_UNROLL):
                it = i * SWA_UNROLL + sub
                res = it // nblk
                qs = (it % nblk) * SWA_QB
                ws = jnp.clip(qs - SWA_RADIUS, 0, sub_len - win)
                rq = sl(res + qs * dil, SWA_QB)
                rk = sl(res + ws * dil, win)
                bias = bias_ref[(qs - ws) // SWA_RADIUS, :, 0:win]
                blocks.append((rq, qr_ref[rq, :], _bf(kr_ref[rk, :]), v_ref[rk, :], bias))
            heads = [(b, hh) for b in range(SWA_UNROLL) for hh in (0, 1)]
            own = (head_lo, jnp.logical_not(head_lo))
            s_all = [_dot_nt(_bf(jnp.where(own[hh], blocks[b][1], 0.0)), blocks[b][2]) + blocks[b][4]
                     for b, hh in heads]
            v_ext = [_bf(jnp.where(own_v[hh], blocks[b][3], 1.0)) for b, hh in heads]
            m_cur = [jnp.broadcast_to(jnp.max(s, axis=-1, keepdims=True), (SWA_QB, LANES)) for s in s_all]
            if first:
                m_new = m_cur
            else:
                m_old = [m_ref[hh, blocks[b][0], :] for b, hh in heads]
                m_new = [jnp.maximum(m_o, m_c) for m_o, m_c in zip(m_old, m_cur)]
            p_all = [jnp.exp2(s - jnp.concatenate([m_n] * (win // LANES), axis=1)) for s, m_n in zip(s_all, m_new)]
            pv_all = [_dot(_bf(p), v) for p, v in zip(p_all, v_ext)]
            for idx, (b, hh) in enumerate(heads):
                rq = blocks[b][0]
                if first:
                    a_ref[hh, rq, :] = pv_all[idx]
                else:
                    a_ref[hh, rq, :] = jnp.exp2(m_old[idx] - m_new[idx]) * a_ref[hh, rq, :] + pv_all[idx]
                m_ref[hh, rq, :] = m_new[idx]
            return c
        lax.fori_loop(0, dil * nblk // SWA_UNROLL, body, 0)

    for s_idx, p_idx in enumerate(SWA_ORDER):
        pl.when(step == s_idx)(functools.partial(run_pattern, SWA_PATTERNS[p_idx][1], s_idx == 0))

    @pl.when(step == SWA_N_PAT - 1)
    def _():
        lo = lax.broadcasted_iota(jnp.int32, (ROW_CHUNK, LANES), 1) < SWA_HEAD_DIM

        def fin(i, c):
            rows = [pl.ds(pl.multiple_of((i * SWA_FIN_UNROLL + k) * ROW_CHUNK, ROW_CHUNK), ROW_CHUNK)
                    for k in range(SWA_FIN_UNROLL)]
            o = [jnp.where(lo, a_ref[0, r, :], a_ref[1, r, :])
                 / pltpu.roll(jnp.where(lo, a_ref[1, r, :], a_ref[0, r, :]), SWA_HEAD_DIM, 1) for r in rows]
            sq = [x * x for x in o]
            hi = [_bf(x) for x in sq]
            rest = [_bf(x - h.astype(jnp.float32)) for x, h in zip(sq, hi)]
            ms = [(_dot(h, gm_ref[...]) + _dot(rs, gm_ref[...])) * (1.0 / SWA_HEAD_DIM)
                  for h, rs in zip(hi, rest)]
            for r, x, m in zip(rows, o, ms):
                o_ref[r, :] = _bf(x * lax.rsqrt(m + EPS) * ng_ref[...])
            return c
        lax.fori_loop(0, nrow // SWA_FIN_UNROLL, fin, 0)


def _swa(proj, cos_tab, sin_tab, rot_mat, norm_c_row, group_mat, layer, batch, seq):
    base = (OFF_SWA - MIXA_COLS) // LANES
    per_pat = SWA_WIDTH // LANES
    qkv = SWA_QKV_WIDTH // LANES
    pairs = SWA_HEADS // 2
    seq_buf = pltpu.VMEM((seq, LANES), jnp.float32)
    first, stride = SWA_ORDER[0], SWA_ORDER[1] - SWA_ORDER[0]
    assert tuple(first + p * stride for p in range(SWA_N_PAT)) == SWA_ORDER
    assert CONV_CH // CONV_GROUPS == SWA_HEAD_DIM

    def spec(which):
        return pl.BlockSpec((seq, LANES),
                            lambda b, hp, p: (b, base + which * qkv + (first + p * stride) * per_pat + hp))

    return pl.pallas_call(
        _swa_kernel,
        grid=(batch, pairs, SWA_N_PAT),
        in_specs=[
            spec(0), spec(1), spec(2),
            pl.BlockSpec((seq, LANES), lambda b, hp, p: (b, 0)),
            pl.BlockSpec((seq, LANES), lambda b, hp, p: (b, 0)),
            pl.BlockSpec((2 * LANES, 2 * LANES), lambda b, hp, p: (0, 0)),
            pl.BlockSpec((None, 1, LANES), lambda b, hp, p: (layer, 0, hp)),
            pl.BlockSpec((LANES, LANES), lambda b, hp, p: (0, 0)),
        ],
        out_specs=pl.BlockSpec((seq, LANES), lambda b, hp, p: (b, hp)),
        out_shape=jax.ShapeDtypeStruct((batch * seq, SWA_WIDTH), MIX_DTYPE),
        scratch_shapes=([seq_buf] * 2 + [pltpu.VMEM((2, seq, LANES), jnp.float32)] * 2
                        + [pltpu.VMEM((3, SWA_QB, SWA_QB + 2 * SWA_RADIUS), jnp.float32)]),
        compiler_params=_cparams(("arbitrary", "arbitrary", "arbitrary")),
        name="swa",
    )(proj, proj, proj, cos_tab, sin_tab, rot_mat, norm_c_row, group_mat)


def _outproj_kernel(x_ref, ya_ref, yb_ref, yc_ref, w_ref, o_ref):
    a0, a1, a2 = CONV_CH, CONV_CH + DN_WIDTH, MIX_WIDTH
    acc = _dot(ya_ref[...], w_ref[0:a0, :])
    acc = acc + _dot(yb_ref[...], w_ref[a0:a1, :])
    acc = acc + _dot(yc_ref[...], w_ref[a1:a2, :])
    o_ref[...] = x_ref[...] + acc


def _out_proj(x2d, y_a, y_b, y_c, w_bf, layer):
    n = x2d.shape[0]
    tm = OUT_TM
    return pl.pallas_call(
        _outproj_kernel,
        grid=(n // tm,),
        in_specs=[
            pl.BlockSpec((tm, D_MODEL), lambda i: (i, 0)),
            pl.BlockSpec((tm, CONV_CH), lambda i: (i, 0)),
            pl.BlockSpec((tm, DN_WIDTH), lambda i: (i, 0)),
            pl.BlockSpec((tm, SWA_WIDTH), lambda i: (i, 0)),
            pl.BlockSpec((None, MIX_WIDTH, D_MODEL), lambda i: (layer, 0, 0)),
        ],
        out_specs=pl.BlockSpec((tm, D_MODEL), lambda i: (i, 0)),
        out_shape=jax.ShapeDtypeStruct((n, D_MODEL), jnp.float32),
        compiler_params=_cparams(("arbitrary",)),
        name="out_proj",
    )(x2d, y_a, y_b, y_c, w_bf)


def _ffn_kernel(xp_ref, x_ref, xn_ref, g_ref, wu_ref, cf_ref, wd_ref, gf_ref, o_ref,
                h_ref, a_ref, *, tiles_per_seq, final_norm):
    tm = x_ref.shape[0]
    t = pl.program_id(0) % tiles_per_seq

    def norm(x):
        return x * lax.rsqrt(jnp.mean(x * x, axis=-1, keepdims=True) + EPS) * g_ref[...]

    halo = HALO_BF16
    hp = norm(xp_ref[...])
    hn = norm(xn_ref[...])
    h_ref[0:halo, :] = _bf(jnp.where(t == 0, 0.0, hp))
    h_ref[halo + tm:halo + tm + halo, :] = _bf(jnp.where(t == tiles_per_seq - 1, 0.0, hn))

    def fill(i, c):
        rows = pl.ds(pl.multiple_of(i * ROW_CHUNK, ROW_CHUNK), ROW_CHUNK)
        h_ref[pl.ds(pl.multiple_of(i * ROW_CHUNK + halo, halo), ROW_CHUNK), :] = _bf(norm(x_ref[rows, :]))
        return c
    lax.fori_loop(0, tm // ROW_CHUNK, fill, 0)

    for start in range(0, D_FF, FF_CHUNK):
        h = h_ref[...]
        stop = min(start + FF_CHUNK, D_FF)
        gcols = slice(start, stop)
        vcols = slice(D_FF + start, D_FF + stop)
        gate = _conv3_val(_dot(h, wu_ref[:, gcols]), cf_ref[:, gcols], tm, halo)
        val = _conv3_val(_dot(h, wu_ref[:, vcols]), cf_ref[:, vcols], tm, halo)
        a_ref[:, gcols] = _bf(_silu(gate) * val)
    o_ref[...] = _dot(a_ref[...], wd_ref[...])

    def fin(i, c):
        rows = pl.ds(pl.multiple_of(i * ROW_CHUNK, ROW_CHUNK), ROW_CHUNK)
        y = x_ref[rows, :] + o_ref[rows, :]
        if final_norm:
            y = y * lax.rsqrt(jnp.mean(y * y, axis=-1, keepdims=True) + EPS) * gf_ref[...]
        o_ref[rows, :] = y
        return c
    lax.fori_loop(0, tm // ROW_CHUNK, fin, 0)


def _ffn(x2d, g_row, wu, cf, wd, gf_row, layer, seq, final_norm):
    n = x2d.shape[0]
    tm = FFN_TM
    tiles_per_seq = seq // tm
    halo = HALO_BF16
    hb = tm // halo
    nhb = n // halo
    whole = lambda shape: pl.BlockSpec((None,) + shape, lambda i: (layer,) + (0,) * len(shape),
                                       pipeline_mode=pl.Buffered(1))
    kern = functools.partial(_ffn_kernel, tiles_per_seq=tiles_per_seq, final_norm=final_norm)
    return pl.pallas_call(
        kern,
        grid=(n // tm,),
        in_specs=[
            pl.BlockSpec((halo, D_MODEL), lambda i: (jnp.maximum(i * hb - 1, 0), 0)),
            pl.BlockSpec((tm, D_MODEL), lambda i: (i, 0)),
            pl.BlockSpec((halo, D_MODEL), lambda i: (jnp.minimum((i + 1) * hb, nhb - 1), 0)),
            whole((1, D_MODEL)),
            whole((D_MODEL, 2 * D_FF)),
            whole((3, 2 * D_FF)),
            whole((D_FF, D_MODEL)),
            pl.BlockSpec((1, D_MODEL), lambda i: (0, 0)),
        ],
        out_specs=pl.BlockSpec((tm, D_MODEL), lambda i: (i, 0)),
        out_shape=jax.ShapeDtypeStruct((n, D_MODEL), jnp.float32),
        scratch_shapes=[
            pltpu.VMEM((tm + 2 * halo, D_MODEL), jnp.bfloat16),
            pltpu.VMEM((tm, D_FF), jnp.bfloat16),
        ],
        compiler_params=_cparams(("arbitrary",)),
        name="ffn",
    )(x2d, x2d, x2d, g_row, wu, cf, wd, gf_row)


def _win_prep_kernel(w_ref, o_ref):
    n_small = 4 * DN_HEADS
    n_swa = 3 * SWA_QKV_WIDTH
    o_ref[:, 0:OFF_SWA] = _bf(w_ref[:, 0:OFF_SWA])
    o_ref[:, OFF_SWA:OFF_SMALL] = _bf(w_ref[:, OFF_SWA + n_small:OFF_SWA + n_small + n_swa])
    o_ref[:, OFF_SMALL:OFF_SMALL + LANES] = _bf(jnp.concatenate(
        [w_ref[:, OFF_SWA:OFF_SWA + n_small], jnp.zeros((w_ref.shape[0], LANES - n_small), jnp.float32)], axis=1))
    o_ref[:, OFF_SMALL + LANES:] = jnp.zeros((w_ref.shape[0], IN_COLS - OFF_SMALL - LANES), jnp.bfloat16)


def _reorder_w_in(w):
    depth, rows, cols = w.shape
    return pl.pallas_call(
        _win_prep_kernel,
        grid=(depth, rows // ROW_CHUNK),
        in_specs=[pl.BlockSpec((None, ROW_CHUNK, cols), lambda l, i: (l, i, 0))],
        out_specs=pl.BlockSpec((None, ROW_CHUNK, IN_COLS), lambda l, i: (l, i, 0)),
        out_shape=jax.ShapeDtypeStruct((depth, rows, IN_COLS), jnp.bfloat16),
        compiler_params=_cparams(("arbitrary", "arbitrary")),
        name="w_in_prep",
    )(w)


def _small_rows(f_vals, b_vals):
    row = jnp.concatenate([f_vals, b_vals], axis=-1).astype(jnp.float32)
    return jnp.pad(row, ((0, 0), (SMALL_AF, LANES - SMALL_AF - row.shape[-1])))[:, None, :]


def kernel(x, positions, norm_mix, w_in, conv_a, norm_a, conv_qkv, a_log_f, a_log_b, dt_bias_f, dt_bias_b,
           norm_dn, norm_c, w_o, norm_ffn, w_up, conv_ffn, w_down, norm_final):
    batch, seq, _ = x.shape
    depth = w_in.shape[0]
    x2d = x.reshape(batch * seq, D_MODEL)
    pos_col = positions.reshape(batch * seq, 1)

    half = ROPE_DIM // 2
    inv_freq = ROPE_THETA ** (-jnp.arange(half, dtype=jnp.float32) / half)
    d_idx = np.arange(LANES) % SWA_HEAD_DIM
    invf_row = jnp.where(jnp.asarray(d_idx < ROPE_DIM), inv_freq[d_idx % half], 0.0).reshape(1, LANES)
    rot_np = np.zeros((2 * LANES, 2 * LANES), np.float32)
    for lane_idx in range(2 * LANES):
        if d_idx[lane_idx % LANES] < half:
            rot_np[lane_idx + half, lane_idx] = -1.0
        elif d_idx[lane_idx % LANES] < ROPE_DIM:
            rot_np[lane_idx - half, lane_idx] = 1.0
    rot_mat = jnp.asarray(rot_np, jnp.bfloat16)
    grp = np.arange(CONV_CH) // (CONV_CH // CONV_GROUPS)
    group_mat = _bf(jnp.asarray(grp[:, None] == grp[None, :], jnp.float32))

    rows = lambda a: a[:, None, :]
    w_in_bf, w_o_bf, w_up_bf, w_down_bf = _reorder_w_in(w_in), _bf(w_o), _bf(w_up), _bf(w_down)
    alog_rows, dtb_rows = _small_rows(a_log_f, a_log_b), _small_rows(dt_bias_f, dt_bias_b)
    cos_tab, sin_tab = _rope_tables(pos_col, invf_row, batch, seq)
    for l in range(depth):
        proj, y_a = _proj_in(x2d, rows(norm_mix), w_in_bf, conv_a, rows(norm_a), group_mat, l, batch, seq)
        y_b = _deltanet(proj, conv_qkv, alog_rows, dtb_rows, rows(norm_dn), l, batch, seq)
        y_c = _swa(proj, cos_tab, sin_tab, rot_mat, rows(norm_c), group_mat, l, batch, seq)
        x2d = _out_proj(x2d, y_a, y_b, y_c, w_o_bf, l)
        x2d = _ffn(x2d, rows(norm_ffn), w_up_bf, conv_ffn, w_down_bf, norm_final.reshape(1, D_MODEL), l, seq,
                   l == depth - 1)
    return x2d.reshape(batch, seq, D_MODEL)
```

```python
import functools

import numpy as np
import jax
import jax.numpy as jnp
from jax import lax
from jax.experimental import pallas as pl
from jax.experimental.pallas import tpu as pltpu

D_MODEL = 1024
EPS = 1e-6
NEG_INF = -1e30
CONV_CH = 256
CONV_GROUPS = 4
DN_HEADS = 4
DN_HEAD_DIM = 128
DN_WIDTH = DN_HEADS * DN_HEAD_DIM
SWA_HEAD_DIM = 64
SWA_HEADS = 4
SWA_PATTERNS = ((128, 1), (512, 4), (2048, 16))
SWA_N_PAT = 3
SWA_WIDTH = SWA_HEADS * SWA_HEAD_DIM
SWA_QKV_WIDTH = SWA_N_PAT * SWA_WIDTH
ROPE_THETA = 500000.0
ROPE_DIM = SWA_HEAD_DIM // 4
D_FF = 2816
MIX_WIDTH = CONV_CH + DN_WIDTH + SWA_WIDTH

MIX_DTYPE = jnp.bfloat16
LANES = 128
SUBLANES = 8
VMEM_LIMIT = 56 * 1024 * 1024

OFF_A = 0
MIXA_COLS = 3 * CONV_CH
OFF_DN_QKV = OFF_A + MIXA_COLS
OFF_DN_GATE = OFF_DN_QKV + 3 * DN_WIDTH
OFF_SWA = OFF_DN_GATE + DN_WIDTH
OFF_SMALL = OFF_SWA + 3 * SWA_QKV_WIDTH
IN_COLS = 5376
PROJ_CN = 768
SMALL_BF, SMALL_BB, SMALL_AF, SMALL_AB = 0, 4, 8, 12

DN_CHUNK = 256
DN_INTRA_UNROLL = 4
DN_GATES_UNROLL = 4
ROW_CHUNK = 256
HALO = SUBLANES
HALO_BF16 = 2 * SUBLANES
FF_CHUNK = 512
FFN_TM = 1024
OUT_TM = 2048
SWA_QB = 128
SWA_RADIUS = 64
SWA_UNROLL = 8
ROPE_ROWS = 512
SWA_FIN_UNROLL = 4
SWA_Q_SCALE = SWA_HEAD_DIM ** -0.5 * float(np.log2(np.e))
SWA_ORDER = (2, 1, 0)


def _cparams(sem, vmem_limit=VMEM_LIMIT):
    return pltpu.CompilerParams(dimension_semantics=sem, vmem_limit_bytes=vmem_limit)


def _bf(x):
    return x.astype(jnp.bfloat16)


def _dot(a, b):
    return jnp.dot(a, b, preferred_element_type=jnp.float32)


def _dot_nt(a, b):
    return lax.dot_general(a, b, (((1,), (1,)), ((), ())), preferred_element_type=jnp.float32)


def _silu(x):
    return x * (1.0 / (1.0 + jnp.exp(-x)))


def _conv3_val(ext, w, rows, halo):
    n = ext.shape[0]
    prev = pltpu.roll(ext, 1, 0)[halo:halo + rows]
    nxt = pltpu.roll(ext, n - 1, 0)[halo:halo + rows]
    return prev * w[0:1] + ext[halo:halo + rows] * w[1:2] + nxt * w[2:3]


def _conv3_ref(ref, lead, start, rows, w):
    def tap(off):
        return ref[lead + (pl.ds(start + off, rows), slice(None))]
    return tap(-1) * w[0:1] + tap(0) * w[1:2] + tap(1) * w[2:3]


def _proj_kernel(x_ref, g_ref, w_ref, cw_ref, ng_ref, gm_ref, o_ref, ya_ref, h_ref, a_ref, u_ref):
    col = pl.program_id(1)

    @pl.when(col == 0)
    def _():
        def body(i, c):
            rows = pl.ds(pl.multiple_of(i * ROW_CHUNK, ROW_CHUNK), ROW_CHUNK)
            x = x_ref[rows, :]
            y = x * lax.rsqrt(jnp.mean(x * x, axis=-1, keepdims=True) + EPS)
            h_ref[rows, :] = _bf(y * g_ref[...])
            return c
        lax.fori_loop(0, x_ref.shape[0] // ROW_CHUNK, body, 0)
        a_ref[...] = _dot(h_ref[...], w_ref[...])
        _mixa_body(a_ref, cw_ref, ng_ref, gm_ref, ya_ref, u_ref)

    @pl.when(col > 0)
    def _():
        o_ref[...] = _dot(h_ref[...], w_ref[...])


def _proj_in(x2d, g_row, w_bf, conv_a, norm_a_row, group_mat, layer, batch, seq):
    ncol = IN_COLS // PROJ_CN
    assert PROJ_CN == MIXA_COLS
    return pl.pallas_call(
        _proj_kernel,
        grid=(batch, ncol),
        in_specs=[
            pl.BlockSpec((seq, D_MODEL), lambda b, j: (b, 0)),
            pl.BlockSpec((None, 1, D_MODEL), lambda b, j: (layer, 0, 0)),
            pl.BlockSpec((None, D_MODEL, PROJ_CN), lambda b, j: (layer, 0, j)),
            pl.BlockSpec((None, 3, CONV_CH), lambda b, j: (layer, 0, 0)),
            pl.BlockSpec((None, 1, CONV_CH), lambda b, j: (layer, 0, 0)),
            pl.BlockSpec((CONV_CH, CONV_CH), lambda b, j: (0, 0)),
        ],
        out_specs=[pl.BlockSpec((seq, PROJ_CN), lambda b, j: (b, jnp.maximum(j - 1, 0))),
                   pl.BlockSpec((seq, CONV_CH), lambda b, j: (b, 0))],
        out_shape=[jax.ShapeDtypeStruct((batch * seq, IN_COLS - MIXA_COLS), jnp.float32),
                   jax.ShapeDtypeStruct((batch * seq, CONV_CH), MIX_DTYPE)],
        scratch_shapes=[pltpu.VMEM((seq, D_MODEL), jnp.bfloat16),
                        pltpu.VMEM((seq, MIXA_COLS), jnp.float32),
                        pltpu.VMEM((seq + 2 * HALO, CONV_CH), jnp.float32)],
        compiler_params=_cparams(("arbitrary", "arbitrary")),
        name="proj_in",
    )(x2d, g_row, w_bf, conv_a, norm_a_row, group_mat)


def _mixa_body(p_ref, cw_ref, ng_ref, gm_ref, o_ref, u_ref):
    seq = o_ref.shape[0]
    nchunk = seq // ROW_CHUNK
    zeros = jnp.zeros((HALO, CONV_CH), jnp.float32)
    u_ref[0:HALO, :] = zeros
    u_ref[HALO + seq:HALO + seq + HALO, :] = zeros

    def fill(i, c):
        rows = pl.ds(pl.multiple_of(i * ROW_CHUNK, ROW_CHUNK), ROW_CHUNK)
        xa = p_ref[rows, 0:CONV_CH]
        gc = p_ref[rows, 2 * CONV_CH:3 * CONV_CH]
        u_ref[pl.ds(pl.multiple_of(i * ROW_CHUNK + HALO, HALO), ROW_CHUNK), :] = gc * xa
        return c
    lax.fori_loop(0, nchunk, fill, 0)

    for i in range(nchunk):
        start = i * ROW_CHUNK
        gb = p_ref[start:start + ROW_CHUNK, CONV_CH:2 * CONV_CH]
        y = gb * _conv3_ref(u_ref, (), start + HALO, ROW_CHUNK, cw_ref[...])
        sq = y * y
        hi = _bf(sq)
        lo = _bf(sq - hi.astype(jnp.float32))
        ms = (_dot(hi, gm_ref[...]) + _dot(lo, gm_ref[...])) * (1.0 / (CONV_CH // CONV_GROUPS))
        o_ref[start:start + ROW_CHUNK, :] = _bf(y * lax.rsqrt(ms + EPS) * ng_ref[...])


def _tri_inverse_group(a_list, lowers, xor_ij, eye):
    n = a_list[0].shape[0]
    d_list = [eye - jnp.where(xor_ij == 1, a, 0.0) for a in a_list]
    level = 1
    while (1 << level) < n:
        s = 1 << level
        mask = (xor_ij >> level) == 1
        e_list = [_bf(jnp.where(mask, a, 0.0)) for a in a_list]
        d_bf = [_bf(d) for d in d_list]
        if s < SUBLANES:
            de_list = [_dot(db, e) for db, e in zip(d_bf, e_list)]
            d_list = [d - _dot(_bf(de), db) for d, de, db in zip(d_list, de_list, d_bf)]
        else:
            blocks = [[d[b * s:(b + 1) * s] for b in range(n // s)] for d in d_list]
            picks = [range(1, n // s, 2) if lower else range(0, n // s, 2) for lower in lowers]
            sel = [_bf(jnp.concatenate([blk[b] for b in pk], axis=0)) for blk, pk in zip(blocks, picks)]
            de_list = [_dot(x, e) for x, e in zip(sel, e_list)]
            upd = [_dot(_bf(de), db) for de, db in zip(de_list, d_bf)]
            d_list = []
            for blk, pk, u in zip(blocks, picks, upd):
                for k, b in enumerate(pk):
                    blk[b] = blk[b] - u[k * s:(k + 1) * s]
                d_list.append(jnp.concatenate(blk, axis=0))
        level += 1
    return d_list


def _dn_intra_group(items, ii, jj, xor_ij, eye):
    c = items[0][0].shape[0]
    wide = lambda col: jnp.concatenate([col] * (c // LANES), axis=1)
    grams = []
    for q, k in [(item[0], item[1]) for item in items[::2]]:
        k_bf = _bf(k)
        grams.append((_dot_nt(k_bf, k_bf), _dot_nt(_bf(q), k_bf)))
    pre = []
    for idx, (q, k, v, beta, gcol, grow, lower) in enumerate(items):
        if lower:
            incl, strict, glast = ii >= jj, ii > jj, gcol[c - 1:c, :]
        else:
            incl, strict, glast = ii <= jj, ii < jj, gcol[0:1, :]
        kk, qk_raw = grams[idx // 2]
        dec = jnp.where(incl, jnp.exp(jnp.where(incl, wide(gcol) - grow, 0.0)), 0.0)
        a = jnp.where(strict, kk * wide(beta) * dec, 0.0)
        pre.append((a, _bf(qk_raw * dec), k * beta, glast))
    lhs_list = [jnp.concatenate([_bf((item[1] * jnp.exp(p[3] - item[4])).T), p[1]], axis=0)
                for item, p in zip(items, pre)]
    tinv_list = _tri_inverse_group([p[0] for p in pre], [item[6] for item in items], xor_ij, eye)
    hd = DN_HEAD_DIM
    eg_list = [jnp.exp(item[4]) for item in items]
    sol_list = [_bf(_dot(_bf(tinv), _bf(jnp.concatenate([item[2] * item[3], p[2] * eg], axis=1))))
                for item, p, eg, tinv in zip(items, pre, eg_list, tinv_list)]
    r_list = [_dot(lhs, sol) for lhs, sol in zip(lhs_list, sol_list)]
    return [(r[:hd, :hd], _bf(r[:hd, hd:]), r[hd:, :hd], _bf(item[0] * eg - r[hd:, hd:]), jnp.exp(p[3]))
            for item, p, eg, r in zip(items, pre, eg_list, r_list)]


def _dn_kernel(q_ref, k_ref, v_ref, gate_ref, small_ref, cq_ref, ck_ref, cv_ref, alog_ref, dtb_ref,
               ng_ref, o_ref,
               pad_ref, qn_ref, kn_ref, vn_ref, bf_ref, bb_ref, gf_ref, gb_ref, rf_ref, rb_ref,
               of_ref, ob_ref, b_ref, p_ref, qp_ref, gl_ref, beta_ref, cf_ref, cb_ref):
    seq = o_ref.shape[0]
    head = pl.program_id(1)
    nrow = seq // ROW_CHUNK
    nchunk = seq // DN_CHUNK

    lane = lax.broadcasted_iota(jnp.int32, (DN_CHUNK, LANES), 1)
    ii = lax.broadcasted_iota(jnp.int32, (DN_CHUNK, DN_CHUNK), 0)
    jj = lax.broadcasted_iota(jnp.int32, (DN_CHUNK, DN_CHUNK), 1)
    tri2 = jnp.concatenate([jnp.where(ii >= jj, 1.0, 0.0), jnp.where(ii <= jj, 1.0, 0.0)],
                           axis=0).astype(jnp.bfloat16)

    def pick(x, col):
        sel = jnp.sum(jnp.where(lane == col, x, 0.0), axis=-1, keepdims=True)
        return jnp.broadcast_to(sel, x.shape)

    @pl.when(head == 0)
    def _():
        neg_a = -jnp.exp(alog_ref[...])

        def gates(i, c):
            cis = [DN_GATES_UNROLL * i + k for k in range(DN_GATES_UNROLL)]
            rows = [pl.ds(pl.multiple_of(ci * DN_CHUNK, DN_CHUNK), DN_CHUNK) for ci in cis]
            sm = [small_ref[r, :] for r in rows]
            z = [s + dtb_ref[...] for s in sm]
            g = [neg_a * (jnp.maximum(zz, 0.0) + jnp.log1p(jnp.exp(-jnp.abs(zz)))) for zz in z]
            g1 = [_bf(x) for x in g]
            r1 = [x - x1.astype(jnp.float32) for x, x1 in zip(g, g1)]
            g2 = [_bf(x) for x in r1]
            g3 = [_bf(x - x2.astype(jnp.float32)) for x, x2 in zip(r1, g2)]
            r = [_dot(tri2, jnp.concatenate(parts, axis=1)) for parts in zip(g1, g2, g3)]
            cf = [x[:DN_CHUNK, 2 * LANES:] + x[:DN_CHUNK, LANES:2 * LANES] + x[:DN_CHUNK, :LANES] for x in r]
            cb = [x[DN_CHUNK:, 2 * LANES:] + x[DN_CHUNK:, LANES:2 * LANES] + x[DN_CHUNK:, :LANES] for x in r]
            for k in range(DN_GATES_UNROLL):
                beta_ref[rows[k], :] = 1.0 / (1.0 + jnp.exp(-sm[k]))
                cf_ref[rows[k], :] = cf[k]
                cb_ref[rows[k], :] = cb[k]
                rf_ref[cis[k]] = cf[k].T
                rb_ref[cis[k]] = cb[k].T
            return c
        lax.fori_loop(0, nchunk // DN_GATES_UNROLL, gates, 0)

    assert DN_CHUNK == ROW_CHUNK
    zeros = jnp.zeros((HALO, DN_HEAD_DIM), jnp.float32)
    pad_ref[0:HALO, :] = zeros
    pad_ref[HALO + seq:HALO + seq + HALO, :] = zeros

    def conv_silu(src_ref, cw_ref, dst_ref, mode, picks):
        def fill(i, c):
            rows = pl.ds(pl.multiple_of(i * ROW_CHUNK, ROW_CHUNK), ROW_CHUNK)
            pad_ref[pl.ds(pl.multiple_of(i * ROW_CHUNK + HALO, HALO), ROW_CHUNK), :] = src_ref[rows, :]
            return c
        lax.fori_loop(0, nrow, fill, 0)

        for i in range(nrow):
            start = i * ROW_CHUNK
            rows = slice(start, start + ROW_CHUNK)
            y = _silu(_conv3_ref(pad_ref, (), start + HALO, ROW_CHUNK, cw_ref[...]))
            if mode != "v":
                y = y * lax.rsqrt(jnp.sum(y * y, axis=-1, keepdims=True) + EPS)
            if mode == "q":
                y = y * (DN_HEAD_DIM ** -0.5)
            dst_ref[rows, :] = y
            for src, col, dst in picks:
                dst[rows, :] = pick(src[rows, :], col + head)

    conv_silu(q_ref, cq_ref, qn_ref, "q", [(beta_ref, SMALL_BF, bf_ref), (beta_ref, SMALL_BB, bb_ref)])
    conv_silu(k_ref, ck_ref, kn_ref, "k", [(cf_ref, SMALL_AF, gf_ref)])
    conv_silu(v_ref, cv_ref, vn_ref, "v", [(cb_ref, SMALL_AB, gb_ref)])

    xor_ij = ii ^ jj
    eye = jnp.where(ii == jj, 1.0, 0.0)

    def intra(i, c):
        items, where = [], []
        for sub in range(DN_INTRA_UNROLL):
            ci = i * DN_INTRA_UNROLL + sub
            rows = pl.ds(pl.multiple_of(ci * DN_CHUNK, DN_CHUNK), DN_CHUNK)
            for direction in (0, 1):
                if direction == 0:
                    beta, gcol = bf_ref[rows, :], gf_ref[rows, :]
                    grow = rf_ref[ci, pl.ds(SMALL_AF + head, 1), :]
                else:
                    beta, gcol = bb_ref[rows, :], gb_ref[rows, :]
                    grow = rb_ref[ci, pl.ds(SMALL_AB + head, 1), :]
                items.append((qn_ref[rows, :], kn_ref[rows, :], vn_ref[rows, :], beta, gcol, grow, direction == 0))
                where.append((direction, ci, rows))
        outs = (of_ref, ob_ref)
        for (direction, ci, rows), (b, p, o_part, qp, gl) in zip(where, _dn_intra_group(items, ii, jj, xor_ij, eye)):
            b_ref[direction, ci] = b
            p_ref[direction, ci] = p
            outs[direction][rows, :] = o_part
            qp_ref[direction, rows, :] = qp
            gl_ref[direction, ci] = jnp.broadcast_to(gl, (SUBLANES, LANES))
        return c
    for group in range(nchunk // DN_INTRA_UNROLL):
        intra(group, 0)

    def fin(ci):
        rows = slice(ci * DN_CHUNK, (ci + 1) * DN_CHUNK)
        o = of_ref[rows, :] + ob_ref[rows, :]
        o = o * lax.rsqrt(jnp.mean(o * o, axis=-1, keepdims=True) + EPS) * ng_ref[...]
        o_ref[rows, :] = _bf(o * _silu(gate_ref[rows, :]))

    state = [jnp.zeros((DN_HEAD_DIM, DN_HEAD_DIM), jnp.float32)] * 2
    for step in range(nchunk):
        cis = (step, nchunk - 1 - step)
        rows = [slice(ci * DN_CHUNK, (ci + 1) * DN_CHUNK) for ci in cis]
        s_bf = [_bf(s) for s in state]
        s_new = [state[d] * gl_ref[d, cis[d], 0:1, :] + b_ref[d, cis[d]] - _dot(p_ref[d, cis[d]], s_bf[d])
                 for d in (0, 1)]
        o_add = [_dot(qp_ref[d, rows[d], :], s_bf[d]) for d in (0, 1)]
        for d, out_ref in ((0, of_ref), (1, ob_ref)):
            out_ref[rows[d], :] += o_add[d]
        state = s_new
        for ci in range(nchunk):
            if max(ci, nchunk - 1 - ci) == step:
                fin(ci)


def _deltanet(proj, conv_qkv, alog_row, dtb_row, norm_dn_row, layer, batch, seq):
    nchunk = seq // DN_CHUNK
    hd = DN_HEAD_DIM
    qb, gb, sb = ((off - MIXA_COLS) // hd for off in (OFF_DN_QKV, OFF_DN_GATE, OFF_SMALL))
    seq_buf = pltpu.VMEM((seq, hd), jnp.float32)
    return pl.pallas_call(
        _dn_kernel,
        grid=(batch, DN_HEADS),
        in_specs=[
            pl.BlockSpec((seq, hd), lambda b, h: (b, qb + h)),
            pl.BlockSpec((seq, hd), lambda b, h: (b, qb + DN_HEADS + h)),
            pl.BlockSpec((seq, hd), lambda b, h: (b, qb + 2 * DN_HEADS + h)),
            pl.BlockSpec((seq, hd), lambda b, h: (b, gb + h)),
            pl.BlockSpec((seq, LANES), lambda b, h: (b, sb)),
            pl.BlockSpec((None, 3, hd), lambda b, h: (layer, 0, h)),
            pl.BlockSpec((None, 3, hd), lambda b, h: (layer, 0, DN_HEADS + h)),
            pl.BlockSpec((None, 3, hd), lambda b, h: (layer, 0, 2 * DN_HEADS + h)),
            pl.BlockSpec((None, 1, LANES), lambda b, h: (layer, 0, 0)),
            pl.BlockSpec((None, 1, LANES), lambda b, h: (layer, 0, 0)),
            pl.BlockSpec((None, 1, hd), lambda b, h: (layer, 0, 0)),
        ],
        out_specs=pl.BlockSpec((seq, hd), lambda b, h: (b, h)),
        out_shape=jax.ShapeDtypeStruct((batch * seq, DN_WIDTH), MIX_DTYPE),
        scratch_shapes=[
            pltpu.VMEM((seq + 2 * HALO, hd), jnp.float32),
            seq_buf, seq_buf, seq_buf,
            seq_buf, seq_buf, seq_buf, seq_buf,
            pltpu.VMEM((nchunk, LANES, DN_CHUNK), jnp.float32),
            pltpu.VMEM((nchunk, LANES, DN_CHUNK), jnp.float32),
            seq_buf, seq_buf,
            pltpu.VMEM((2, nchunk, hd, hd), jnp.float32),
            pltpu.VMEM((2, nchunk, hd, hd), jnp.bfloat16),
            pltpu.VMEM((2, seq, hd), jnp.bfloat16),
            pltpu.VMEM((2, nchunk, SUBLANES, LANES), jnp.float32),
            seq_buf, seq_buf, seq_buf,
        ],
        compiler_params=_cparams(("arbitrary", "arbitrary")),
        name="deltanet",
    )(proj, proj, proj, proj, proj, conv_qkv, conv_qkv, conv_qkv, alog_row, dtb_row, norm_dn_row)


def _rope_table_kernel(pos_ref, invf_ref, cos_ref, sin_ref):
    def body(i, c):
        rows = pl.ds(pl.multiple_of(i * ROW_CHUNK, ROW_CHUNK), ROW_CHUNK)
        ang = pos_ref[rows, :].astype(jnp.float32) * invf_ref[...]
        cos_ref[rows, :] = jnp.cos(ang)
        sin_ref[rows, :] = jnp.sin(ang)
        return c
    lax.fori_loop(0, cos_ref.shape[0] // ROW_CHUNK, body, 0)


def _rope_tables(pos_col, invf_row, batch, seq):
    table = jax.ShapeDtypeStruct((batch * seq, LANES), jnp.float32)
    return pl.pallas_call(
        _rope_table_kernel,
        grid=(batch,),
        in_specs=[pl.BlockSpec((seq, 1), lambda b: (b, 0)), pl.BlockSpec((1, LANES), lambda b: (0, 0))],
        out_specs=[pl.BlockSpec((seq, LANES), lambda b: (b, 0))] * 2,
        out_shape=[table, table],
        compiler_params=_cparams(("arbitrary",)),
        name="rope_tables",
    )(pos_col, invf_row)


def _swa_kernel(q_ref, k_ref, v_ref, cos_ref, sin_ref, rot_ref, ng_ref, gm_ref, o_ref,
                qr_ref, kr_ref, m_ref, a_ref, bias_ref):
    seq = o_ref.shape[0]
    step = pl.program_id(2)
    nrow = seq // ROW_CHUNK

    chunks = [slice(i * ROPE_ROWS, (i + 1) * ROPE_ROWS) for i in range(seq // ROPE_ROWS)]
    xs = [jnp.concatenate([q_ref[r, :], k_ref[r, :]], axis=1) for r in chunks]
    xrs = [_dot(_bf(x), rot_ref[...]) for x in xs]
    for r, x, xr in zip(chunks, xs, xrs):
        cs, sn = cos_ref[r, :], sin_ref[r, :]
        y = x * jnp.concatenate([cs, cs], axis=1) + xr * jnp.concatenate([sn, sn], axis=1)
        qr_ref[r, :] = y[:, :LANES] * SWA_Q_SCALE
        kr_ref[r, :] = y[:, LANES:]

    def run_pattern(dil, first):
        sub_len = seq // dil
        nblk = sub_len // SWA_QB
        win = min(sub_len, SWA_QB + 2 * SWA_RADIUS)
        lane_q = lax.broadcasted_iota(jnp.int32, (SWA_QB, LANES), 1)
        head_lo = lane_q < SWA_HEAD_DIM
        lane_w = lax.broadcasted_iota(jnp.int32, (win, LANES), 1)
        own_v = (lane_w < SWA_HEAD_DIM, lane_w >= SWA_HEAD_DIM)
        dij = (lax.broadcasted_iota(jnp.int32, (SWA_QB, win), 0)
               - lax.broadcasted_iota(jnp.int32, (SWA_QB, win), 1))
        for k in range(3 if sub_len > win else 1):
            bias_ref[k, :, 0:win] = jnp.where(jnp.abs(dij + k * SWA_RADIUS) <= SWA_RADIUS, 0.0, NEG_INF)

        def sl(start, size):
            return pl.ds(start, size) if dil == 1 else pl.ds(start, size, stride=dil)

        def body(i, c):
            blocks = []
            for sub in range(SWA_UNROLL):
                it = i * SWA_UNROLL + sub
                res = it // nblk
                qs = (it % nblk) * SWA_QB
                ws = jnp.clip(qs - SWA_RADIUS, 0, sub_len - win)
                rq = sl(res + qs * dil, SWA_QB)
                rk = sl(res + ws * dil, win)
                bias = bias_ref[(qs - ws) // SWA_RADIUS, :, 0:win]
                blocks.append((rq, qr_ref[rq, :], _bf(kr_ref[rk, :]), v_ref[rk, :], bias))
            heads = [(b, hh) for b in range(SWA_UNROLL) for hh in (0, 1)]
            own = (head_lo, jnp.logical_not(head_lo))
            s_all = [_dot_nt(_bf(jnp.where(own[hh], blocks[b][1], 0.0)), blocks[b][2]) + blocks[b][4]
                     for b, hh in heads]
            v_ext = [_bf(jnp.where(own_v[hh], blocks[b][3], 1.0)) for b, hh in heads]
            m_cur = [jnp.broadcast_to(jnp.max(s, axis=-1, keepdims=True), (SWA_QB, LANES)) for s in s_all]
            if first:
                m_new = m_cur
            else:
                m_old = [m_ref[hh, blocks[b][0], :] for b, hh in heads]
                m_new = [jnp.maximum(m_o, m_c) for m_o, m_c in zip(m_old, m_cur)]
            p_all = [jnp.exp2(s - jnp.concatenate([m_n] * (win // LANES), axis=1)) for s, m_n in zip(s_all, m_new)]
            pv_all = [_dot(_bf(p), v) for p, v in zip(p_all, v_ext)]
            for idx, (b, hh) in enumerate(heads):
                rq = blocks[b][0]
                if first:
                    a_ref[hh, rq, :] = pv_all[idx]
                else:
                    a_ref[hh, rq, :] = jnp.exp2(m_old[idx] - m_new[idx]) * a_ref[hh, rq, :] + pv_all[idx]
                m_ref[hh, rq, :] = m_new[idx]
            return c
        lax.fori_loop(0, dil * nblk // SWA_UNROLL, body, 0)

    for s_idx, p_idx in enumerate(SWA_ORDER):
        pl.when(step == s_idx)(functools.partial(run_pattern, SWA_PATTERNS[p_idx][1], s_idx == 0))

    @pl.when(step == SWA_N_PAT - 1)
    def _():
        lo = lax.broadcasted_iota(jnp.int32, (ROW_CHUNK, LANES), 1) < SWA_HEAD_DIM

        def fin(i, c):
            rows = [pl.ds(pl.multiple_of((i * SWA_FIN_UNROLL + k) * ROW_CHUNK, ROW_CHUNK), ROW_CHUNK)
                    for k in range(SWA_FIN_UNROLL)]
            o = [jnp.where(lo, a_ref[0, r, :], a_ref[1, r, :])
                 / pltpu.roll(jnp.where(lo, a_ref[1, r, :], a_ref[0, r, :]), SWA_HEAD_DIM, 1) for r in rows]
            sq = [x * x for x in o]
            hi = [_bf(x) for x in sq]
            rest = [_bf(x - h.astype(jnp.float32)) for x, h in zip(sq, hi)]
            ms = [(_dot(h, gm_ref[...]) + _dot(rs, gm_ref[...])) * (1.0 / SWA_HEAD_DIM)
                  for h, rs in zip(hi, rest)]
            for r, x, m in zip(rows, o, ms):
                o_ref[r, :] = _bf(x * lax.rsqrt(m + EPS) * ng_ref[...])
            return c
        lax.fori_loop(0, nrow // SWA_FIN_UNROLL, fin, 0)


def _swa(proj, cos_tab, sin_tab, rot_mat, norm_c_row, group_mat, layer, batch, seq):
    base = (OFF_SWA - MIXA_COLS) // LANES
    per_pat = SWA_WIDTH // LANES
    qkv = SWA_QKV_WIDTH // LANES
    pairs = SWA_HEADS // 2
    seq_buf = pltpu.VMEM((seq, LANES), jnp.float32)
    first, stride = SWA_ORDER[0], SWA_ORDER[1] - SWA_ORDER[0]
    assert tuple(first + p * stride for p in range(SWA_N_PAT)) == SWA_ORDER
    assert CONV_CH // CONV_GROUPS == SWA_HEAD_DIM

    def spec(which):
        return pl.BlockSpec((seq, LANES),
                            lambda b, hp, p: (b, base + which * qkv + (first + p * stride) * per_pat + hp))

    return pl.pallas_call(
        _swa_kernel,
        grid=(batch, pairs, SWA_N_PAT),
        in_specs=[
            spec(0), spec(1), spec(2),
            pl.BlockSpec((seq, LANES), lambda b, hp, p: (b, 0)),
            pl.BlockSpec((seq, LANES), lambda b, hp, p: (b, 0)),
            pl.BlockSpec((2 * LANES, 2 * LANES), lambda b, hp, p: (0, 0)),
            pl.BlockSpec((None, 1, LANES), lambda b, hp, p: (layer, 0, hp)),
            pl.BlockSpec((LANES, LANES), lambda b, hp, p: (0, 0)),
        ],
        out_specs=pl.BlockSpec((seq, LANES), lambda b, hp, p: (b, hp)),
        out_shape=jax.ShapeDtypeStruct((batch * seq, SWA_WIDTH), MIX_DTYPE),
        scratch_shapes=([seq_buf] * 2 + [pltpu.VMEM((2, seq, LANES), jnp.float32)] * 2
                        + [pltpu.VMEM((3, SWA_QB, SWA_QB + 2 * SWA_RADIUS), jnp.float32)]),
        compiler_params=_cparams(("arbitrary", "arbitrary", "arbitrary")),
        name="swa",
    )(proj, proj, proj, cos_tab, sin_tab, rot_mat, norm_c_row, group_mat)


def _outproj_kernel(x_ref, ya_ref, yb_ref, yc_ref, w_ref, o_ref):
    a0, a1, a2 = CONV_CH, CONV_CH + DN_WIDTH, MIX_WIDTH
    acc = _dot(ya_ref[...], w_ref[0:a0, :])
    acc = acc + _dot(yb_ref[...], w_ref[a0:a1, :])
    acc = acc + _dot(yc_ref[...], w_ref[a1:a2, :])
    o_ref[...] = x_ref[...] + acc


def _out_proj(x2d, y_a, y_b, y_c, w_bf, layer):
    n = x2d.shape[0]
    tm = OUT_TM
    return pl.pallas_call(
        _outproj_kernel,
        grid=(n // tm,),
        in_specs=[
            pl.BlockSpec((tm, D_MODEL), lambda i: (i, 0)),
            pl.BlockSpec((tm, CONV_CH), lambda i: (i, 0)),
            pl.BlockSpec((tm, DN_WIDTH), lambda i: (i, 0)),
            pl.BlockSpec((tm, SWA_WIDTH), lambda i: (i, 0)),
            pl.BlockSpec((None, MIX_WIDTH, D_MODEL), lambda i: (layer, 0, 0)),
        ],
        out_specs=pl.BlockSpec((tm, D_MODEL), lambda i: (i, 0)),
        out_shape=jax.ShapeDtypeStruct((n, D_MODEL), jnp.float32),
        compiler_params=_cparams(("arbitrary",)),
        name="out_proj",
    )(x2d, y_a, y_b, y_c, w_bf)


def _ffn_kernel(xp_ref, x_ref, xn_ref, g_ref, wu_ref, cf_ref, wd_ref, gf_ref, o_ref,
                h_ref, a_ref, *, tiles_per_seq, final_norm):
    tm = x_ref.shape[0]
    t = pl.program_id(0) % tiles_per_seq

    def norm(x):
        return x * lax.rsqrt(jnp.mean(x * x, axis=-1, keepdims=True) + EPS) * g_ref[...]

    halo = HALO_BF16
    hp = norm(xp_ref[...])
    hn = norm(xn_ref[...])
    h_ref[0:halo, :] = _bf(jnp.where(t == 0, 0.0, hp))
    h_ref[halo + tm:halo + tm + halo, :] = _bf(jnp.where(t == tiles_per_seq - 1, 0.0, hn))

    def fill(i, c):
        rows = pl.ds(pl.multiple_of(i * ROW_CHUNK, ROW_CHUNK), ROW_CHUNK)
        h_ref[pl.ds(pl.multiple_of(i * ROW_CHUNK + halo, halo), ROW_CHUNK), :] = _bf(norm(x_ref[rows, :]))
        return c
    lax.fori_loop(0, tm // ROW_CHUNK, fill, 0)

    for start in range(0, D_FF, FF_CHUNK):
        h = h_ref[...]
        stop = min(start + FF_CHUNK, D_FF)
        gcols = slice(start, stop)
        vcols = slice(D_FF + start, D_FF + stop)
        gate = _conv3_val(_dot(h, wu_ref[:, gcols]), cf_ref[:, gcols], tm, halo)
        val = _conv3_val(_dot(h, wu_ref[:, vcols]), cf_ref[:, vcols], tm, halo)
        a_ref[:, gcols] = _bf(_silu(gate) * val)
    o_ref[...] = _dot(a_ref[...], wd_ref[...])

    def fin(i, c):
        rows = pl.ds(pl.multiple_of(i * ROW_CHUNK, ROW_CHUNK), ROW_CHUNK)
        y = x_ref[rows, :] + o_ref[rows, :]
        if final_norm:
            y = y * lax.rsqrt(jnp.mean(y * y, axis=-1, keepdims=True) + EPS) * gf_ref[...]
        o_ref[rows, :] = y
        return c
    lax.fori_loop(0, tm // ROW_CHUNK, fin, 0)


def _ffn(x2d, g_row, wu, cf, wd, gf_row, layer, seq, final_norm):
    n = x2d.shape[0]
    tm = FFN_TM
    tiles_per_seq = seq // tm
    halo = HALO_BF16
    hb = tm // halo
    nhb = n // halo
    whole = lambda shape: pl.BlockSpec((None,) + shape, lambda i: (layer,) + (0,) * len(shape),
                                       pipeline_mode=pl.Buffered(1))
    kern = functools.partial(_ffn_kernel, tiles_per_seq=tiles_per_seq, final_norm=final_norm)
    return pl.pallas_call(
        kern,
        grid=(n // tm,),
        in_specs=[
            pl.BlockSpec((halo, D_MODEL), lambda i: (jnp.maximum(i * hb - 1, 0), 0)),
            pl.BlockSpec((tm, D_MODEL), lambda i: (i, 0)),
            pl.BlockSpec((halo, D_MODEL), lambda i: (jnp.minimum((i + 1) * hb, nhb - 1), 0)),
            whole((1, D_MODEL)),
            whole((D_MODEL, 2 * D_FF)),
            whole((3, 2 * D_FF)),
            whole((D_FF, D_MODEL)),
            pl.BlockSpec((1, D_MODEL), lambda i: (0, 0)),
        ],
        out_specs=pl.BlockSpec((tm, D_MODEL), lambda i: (i, 0)),
        out_shape=jax.ShapeDtypeStruct((n, D_MODEL), jnp.float32),
        scratch_shapes=[
            pltpu.VMEM((tm + 2 * halo, D_MODEL), jnp.bfloat16),
            pltpu.VMEM((tm, D_FF), jnp.bfloat16),
        ],
        compiler_params=_cparams(("arbitrary",)),
        name="ffn",
    )(x2d, x2d, x2d, g_row, wu, cf, wd, gf_row)


def _win_prep_kernel(w_ref, o_ref):
    n_small = 4 * DN_HEADS
    n_swa = 3 * SWA_QKV_WIDTH
    o_ref[:, 0:OFF_SWA] = _bf(w_ref[:, 0:OFF_SWA])
    o_ref[:, OFF_SWA:OFF_SMALL] = _bf(w_ref[:, OFF_SWA + n_small:OFF_SWA + n_small + n_swa])
    o_ref[:, OFF_SMALL:OFF_SMALL + LANES] = _bf(jnp.concatenate(
        [w_ref[:, OFF_SWA:OFF_SWA + n_small], jnp.zeros((w_ref.shape[0], LANES - n_small), jnp.float32)], axis=1))
    o_ref[:, OFF_SMALL + LANES:] = jnp.zeros((w_ref.shape[0], IN_COLS - OFF_SMALL - LANES), jnp.bfloat16)


def _reorder_w_in(w):
    depth, rows, cols = w.shape
    return pl.pallas_call(
        _win_prep_kernel,
        grid=(depth, rows // ROW_CHUNK),
        in_specs=[pl.BlockSpec((None, ROW_CHUNK, cols), lambda l, i: (l, i, 0))],
        out_specs=pl.BlockSpec((None, ROW_CHUNK, IN_COLS), lambda l, i: (l, i, 0)),
        out_shape=jax.ShapeDtypeStruct((depth, rows, IN_COLS), jnp.bfloat16),
        compiler_params=_cparams(("arbitrary", "arbitrary")),
        name="w_in_prep",
    )(w)


def _small_rows(f_vals, b_vals):
    row = jnp.concatenate([f_vals, b_vals], axis=-1).astype(jnp.float32)
    return jnp.pad(row, ((0, 0), (SMALL_AF, LANES - SMALL_AF - row.shape[-1])))[:, None, :]


def kernel(x, positions, norm_mix, w_in, conv_a, norm_a, conv_qkv, a_log_f, a_log_b, dt_bias_f, dt_bias_b,
           norm_dn, norm_c, w_o, norm_ffn, w_up, conv_ffn, w_down, norm_final):
    batch, seq, _ = x.shape
    depth = w_in.shape[0]
    x2d = x.reshape(batch * seq, D_MODEL)
    pos_col = positions.reshape(batch * seq, 1)

    half = ROPE_DIM // 2
    inv_freq = ROPE_THETA ** (-jnp.arange(half, dtype=jnp.float32) / half)
    d_idx = np.arange(LANES) % SWA_HEAD_DIM
    invf_row = jnp.where(jnp.asarray(d_idx < ROPE_DIM), inv_freq[d_idx % half], 0.0).reshape(1, LANES)
    rot_np = np.zeros((2 * LANES, 2 * LANES), np.float32)
    for lane_idx in range(2 * LANES):
        if d_idx[lane_idx % LANES] < half:
            rot_np[lane_idx + half, lane_idx] = -1.0
        elif d_idx[lane_idx % LANES] < ROPE_DIM:
            rot_np[lane_idx - half, lane_idx] = 1.0
    rot_mat = jnp.asarray(rot_np, jnp.bfloat16)
    grp = np.arange(CONV_CH) // (CONV_CH // CONV_GROUPS)
    group_mat = _bf(jnp.asarray(grp[:, None] == grp[None, :], jnp.float32))

    rows = lambda a: a[:, None, :]
    w_in_bf, w_o_bf, w_up_bf, w_down_bf = _reorder_w_in(w_in), _bf(w_o), _bf(w_up), _bf(w_down)
    alog_rows, dtb_rows = _small_rows(a_log_f, a_log_b), _small_rows(dt_bias_f, dt_bias_b)
    cos_tab, sin_tab = _rope_tables(pos_col, invf_row, batch, seq)
    for l in range(depth):
        proj, y_a = _proj_in(x2d, rows(norm_mix), w_in_bf, conv_a, rows(norm_a), group_mat, l, batch, seq)
        y_b = _deltanet(proj, conv_qkv, alog_rows, dtb_rows, rows(norm_dn), l, batch, seq)
        y_c = _swa(proj, cos_tab, sin_tab, rot_mat, rows(norm_c), group_mat, l, batch, seq)
        x2d = _out_proj(x2d, y_a, y_b, y_c, w_o_bf, l)
        x2d = _ffn(x2d, rows(norm_ffn), w_up_bf, conv_ffn, w_down_bf, norm_final.reshape(1, D_MODEL), l, seq,
                   l == depth - 1)
    return x2d.reshape(batch, seq, D_MODEL)
```
